```python
import math
import jax
import jax.numpy as jnp
from jax import lax
import numpy as np

D_MODEL = 2048
BATCH = 16
SEQ = 256
DEPTH = 2
DEC_BATCH = 4
DEC_SEQ = 2048
PAST_LEN = 256

GRID_W = 64
EPS = 1e-6
HEAD_DIM = 128
A_HEADS = D_MODEL // 256
A_KV_HEADS = A_HEADS // 4
A_WIDTH = A_HEADS * HEAD_DIM
KV_WIDTH = A_KV_HEADS * HEAD_DIM
Q_BLOCK = 128
ROPE_THETA = 10000.0
ATTN_SCALE = HEAD_DIM ** -0.5
CHUNK = 128
B_GROUPS = 4
B_WIDTH = D_MODEL // 4
B_GC = B_WIDTH // B_GROUPS
POOL_WINDOWS = (2, 4, 8, 16)
C_GROUPS = 4
C_WIDTH = D_MODEL // 4
C_GC = C_WIDTH // C_GROUPS
N_BRANCH = 3
IN_SPLITS = (A_WIDTH, KV_WIDTH, KV_WIDTH, A_WIDTH, B_WIDTH, B_WIDTH, B_WIDTH, C_WIDTH, C_WIDTH)
IN_COLS = 2 * A_WIDTH + 2 * KV_WIDTH + 3 * B_WIDTH + 2 * C_WIDTH

kernel_name = "hybrid_ctx_prefix_gqa_sgu_pool_step"


def _rmsnorm(x, w):
    xf = x.astype(jnp.float32)
    y = xf * lax.rsqrt(jnp.mean(xf * xf, axis=-1, keepdims=True) + EPS)
    return (y * w.astype(jnp.float32)).astype(x.dtype)


def _split_cols(proj):
    idx = []
    acc = 0
    for s in IN_SPLITS[:-1]:
        acc += s
        idx.append(acc)
    return jnp.split(proj, idx, axis=-1)


def _axial_rope_tables(n_tokens):
    rows = n_tokens // GRID_W
    row = jnp.repeat(jnp.arange(rows), GRID_W).astype(jnp.float32)
    col = jnp.tile(jnp.arange(GRID_W), rows).astype(jnp.float32)
    n_freq = HEAD_DIM // 4
    inv = ROPE_THETA ** (-jnp.arange(n_freq, dtype=jnp.float32) / n_freq)
    ang_r = row[:, None] * inv[None, :]
    ang_c = col[:, None] * inv[None, :]
    return (jnp.cos(ang_r), jnp.sin(ang_r), jnp.cos(ang_c), jnp.sin(ang_c))


def _rotate(xh, cos, sin):
    f = xh.shape[-1] // 2
    x1, x2 = xh[..., :f], xh[..., f:]
    cos = cos[None, :, None, :]
    sin = sin[None, :, None, :]
    return jnp.concatenate([x1 * cos - x2 * sin, x2 * cos + x1 * sin], axis=-1)


def _apply_axial_rope(x, tables):
    cr, sr, cc, sc = tables
    xf = x.astype(jnp.float32)
    half = HEAD_DIM // 2
    out = jnp.concatenate([_rotate(xf[..., :half], cr, sr), _rotate(xf[..., half:], cc, sc)], axis=-1)
    return out.astype(x.dtype)


def _block_attention(q, k, v):
    b, s = q.shape[0], q.shape[1]
    g = A_HEADS // A_KV_HEADS
    nb = s // Q_BLOCK
    qb = q.reshape(b, nb, Q_BLOCK, A_KV_HEADS, g, HEAD_DIM).transpose(1, 0, 2, 3, 4, 5)
    kf = k.astype(jnp.float32)
    vf = v.astype(jnp.float32)

    def one_block(qblk):
        sc = jnp.einsum('bqkgd,btkd->bkgqt', qblk.astype(jnp.float32), kf) * ATTN_SCALE
        p = jax.nn.softmax(sc, axis=-1)
        return jnp.einsum('bkgqt,btkd->bqkgd', p, vf).astype(q.dtype)

    o = lax.map(one_block, qb)
    return o.transpose(1, 0, 2, 3, 4, 5).reshape(b, s, A_WIDTH)


def _chunk_sgu(u, v, norm_w, w_s, b_s):
    b, s, _ = v.shape
    vc = _rmsnorm(v, norm_w).reshape(b, s // CHUNK, CHUNK, B_GROUPS, B_GC)
    mixed = jnp.einsum('gpq,bnqgc->bnpgc', w_s, vc) + b_s.T[None, None, :, :, None]
    return u * mixed.reshape(b, s, B_WIDTH)


def _multiscale_pool(z, w_pool, pool_scale):
    b, s, _ = z.shape
    zf = z.astype(jnp.float32)
    cs = jnp.concatenate([jnp.zeros((b, 1, C_WIDTH), jnp.float32), jnp.cumsum(zf, axis=1)], axis=1)
    t = jnp.arange(s)
    outs = []
    for gi, w in enumerate(POOL_WINDOWS):
        lo = jnp.clip(t - w // 2, 0, s)
        hi = jnp.clip(t + w - w // 2, 0, s)
        sl = slice(gi * C_GC, (gi + 1) * C_GC)
        cnt = (hi - lo).astype(jnp.float32)[None, :, None]
        d = (cs[:, hi, sl] - cs[:, lo, sl]) / cnt - zf[..., sl]
        outs.append(jnp.einsum('bsc,cd->bsd', d, w_pool[gi].astype(jnp.float32)))
    out = jnp.concatenate(outs, axis=-1) * pool_scale.astype(jnp.float32)
    return out.astype(z.dtype)


def _mixer_layer(x, mod, rope_tables, ctx_k, ctx_v, norm_w, w_in, q_norm_w, k_norm_w,
                 sgu_norm_w, w_sgu, b_sgu, w_pool, pool_scale, w_br_a, w_br_b, w_br_c,
                 w_merge, b_merge, w_out):
    b, s, _ = x.shape
    shift, scale, gate = jnp.split(mod.astype(x.dtype), 3, axis=-1)
    h = _rmsnorm(x, norm_w) * (1.0 + scale) + shift
    q, k, v, ga, u, vb, gb, z, gc = _split_cols(h @ w_in)
    q = _rmsnorm(q.reshape(b, s, A_HEADS, HEAD_DIM), q_norm_w)
    k = _rmsnorm(k.reshape(b, s, A_KV_HEADS, HEAD_DIM), k_norm_w)
    v = v.reshape(b, s, A_KV_HEADS, HEAD_DIM)
    k_plain = k
    if rope_tables is None:
        keys, vals = k, v
    else:
        q = _apply_axial_rope(q, rope_tables)
        keys = jnp.concatenate([_apply_axial_rope(k, rope_tables), ctx_k.astype(x.dtype)], axis=1)
        vals = jnp.concatenate([v, ctx_v.astype(x.dtype)], axis=1)
    attn = _block_attention(q, keys, vals) * jax.nn.silu(ga)
    bout = _chunk_sgu(u, vb, sgu_norm_w, w_sgu, b_sgu) * jax.nn.silu(gb)
    cout = _multiscale_pool(z, w_pool, pool_scale) * jax.nn.silu(gc)
    gates = jax.nn.sigmoid((h @ w_merge + b_merge).astype(jnp.float32)).astype(x.dtype)
    gates = gates.reshape(b, s, N_BRANCH, D_MODEL)
    merged = (gates[:, :, 0] * (attn @ w_br_a)
              + gates[:, :, 1] * (bout @ w_br_b)
              + gates[:, :, 2] * (cout @ w_br_c))
    out = merged @ w_out
    return x + gate * out, k_plain, v


def setup_inputs(seed: int = 0) -> dict:
    key = jax.random.key(seed)
    ks = jax.random.split(key, 32)
    nrm = jax.random.normal
    f32 = jnp.float32
    d = D_MODEL
    return {
        "x_prompt": nrm(ks[0], (BATCH, SEQ, d), f32),
        "x_sample": nrm(ks[1], (DEC_BATCH, DEC_SEQ, d), f32),
        "cache_k": nrm(ks[2], (DEC_BATCH, DEPTH, PAST_LEN, A_KV_HEADS, HEAD_DIM), f32),
        "cache_v": nrm(ks[3], (DEC_BATCH, DEPTH, PAST_LEN, A_KV_HEADS, HEAD_DIM), f32),
        "c": nrm(ks[4], (DEC_BATCH, d), f32),
        "c_ctx": nrm(ks[5], (d,), f32),
        "norm_w": 1.0 + 0.02 * nrm(ks[6], (DEPTH, d), f32),
        "w_ada": 0.5 * d ** -0.5 * nrm(ks[7], (DEPTH, d, 3 * d), f32),
        "b_ada": 0.01 * nrm(ks[8], (DEPTH, 3 * d), f32),
        "w_in": d ** -0.5 * nrm(ks[9], (DEPTH, d, IN_COLS), f32),
        "q_norm_w": 1.0 + 0.02 * nrm(ks[10], (DEPTH, HEAD_DIM), f32),
        "k_norm_w": 1.0 + 0.02 * nrm(ks[11], (DEPTH, HEAD_DIM), f32),
        "sgu_norm_w": 1.0 + 0.02 * nrm(ks[12], (DEPTH, B_WIDTH), f32),
        "w_sgu": CHUNK ** -0.5 * nrm(ks[13], (DEPTH, B_GROUPS, CHUNK, CHUNK), f32),
        "b_sgu": 0.02 * nrm(ks[14], (DEPTH, B_GROUPS, CHUNK), f32),
        "w_pool": C_GC ** -0.5 * nrm(ks[15], (DEPTH, C_GROUPS, C_GC, C_GC), f32),
        "pool_scale": 1.0 + 0.02 * nrm(ks[16], (DEPTH, C_WIDTH), f32),
        "w_br_a": A_WIDTH ** -0.5 * nrm(ks[17], (DEPTH, A_WIDTH, d), f32),
        "w_br_b": B_WIDTH ** -0.5 * nrm(ks[18], (DEPTH, B_WIDTH, d), f32),
        "w_br_c": C_WIDTH ** -0.5 * nrm(ks[19], (DEPTH, C_WIDTH, d), f32),
        "w_merge": d ** -0.5 * nrm(ks[20], (DEPTH, d, N_BRANCH * d), f32),
        "b_merge": 0.01 * nrm(ks[21], (DEPTH, N_BRANCH * d), f32),
        "w_out": d ** -0.5 * nrm(ks[22], (DEPTH, d, d), f32),
        "final_norm_w": 1.0 + 0.02 * nrm(ks[23], (d,), f32),
    }


def reference(x_prompt, x_sample, cache_k, cache_v, c, c_ctx, norm_w, w_ada, b_ada, w_in,
              q_norm_w, k_norm_w, sgu_norm_w, w_sgu, b_sgu, w_pool, pool_scale,
              w_br_a, w_br_b, w_br_c, w_merge, b_merge, w_out, final_norm_w):
    rope_tables = _axial_rope_tables(x_sample.shape[1])
    xp = x_prompt
    xs = x_sample
    ks_out = []
    vs_out = []
    for l in range(DEPTH):
        lp = (norm_w[l], w_in[l], q_norm_w[l], k_norm_w[l], sgu_norm_w[l], w_sgu[l], b_sgu[l],
              w_pool[l], pool_scale[l], w_br_a[l], w_br_b[l], w_br_c[l], w_merge[l], b_merge[l],
              w_out[l])
        mod_ctx = (jax.nn.silu(c_ctx) @ w_ada[l] + b_ada[l])[None, None, :]
        xp, k_l, v_l = _mixer_layer(xp, mod_ctx, None, None, None, *lp)
        ks_out.append(k_l)
        vs_out.append(v_l)
        mod_lat = (jax.nn.silu(c) @ w_ada[l] + b_ada[l])[:, None, :]
        xs, _, _ = _mixer_layer(xs, mod_lat, rope_tables, cache_k[:, l], cache_v[:, l], *lp)
    y_prompt = _rmsnorm(xp, final_norm_w)
    y_sample = _rmsnorm(xs, final_norm_w)
    state_k = jnp.stack(ks_out, axis=1)
    state_v = jnp.stack(vs_out, axis=1)
    return (y_prompt, y_sample, state_k, state_v)
```

```python
import functools

import jax
import jax.numpy as jnp
from jax import lax
from jax.experimental import pallas as pl
from jax.experimental.pallas import tpu as pltpu

F32 = jnp.float32
BF16 = jnp.bfloat16

D_MODEL = 2048
DEPTH = 2
GRID_W = 64
EPS = 1e-6
HEAD_DIM = 128
A_HEADS = 8
A_KV_HEADS = 2
A_WIDTH = A_HEADS * HEAD_DIM
KV_WIDTH = A_KV_HEADS * HEAD_DIM
ROPE_THETA = 10000.0
ATTN_SCALE = HEAD_DIM ** -0.5
CHUNK = 128
B_GROUPS = 4
B_WIDTH = 512
C_WIDTH = 512
POOL_WINDOWS = (2, 4, 8, 16)
POOL_HALO = 16
N_BRANCH = 3
GATE_COLS = N_BRANCH * D_MODEL
IN_COLS = 2 * A_WIDTH + 2 * KV_WIDTH + 3 * B_WIDTH + 2 * C_WIDTH
CAT_COLS = GATE_COLS + IN_COLS

COL_TILE = 512
N_COL_TILES = CAT_COLS // COL_TILE
GATE_TILES = GATE_COLS // COL_TILE
Q_TILE0 = GATE_TILES
KV_TILE = Q_TILE0 + A_WIDTH // COL_TILE
GA_TILE0 = KV_TILE + 1
U_TILE = GA_TILE0 + A_WIDTH // COL_TILE
VB_TILE = U_TILE + 1
GB_TILE = VB_TILE + 1
Z_TILE = GB_TILE + 1
GC_TILE = Z_TILE + 1
HEADS_PER_TILE = COL_TILE // HEAD_DIM
K_BLOCK = (KV_TILE * COL_TILE) // HEAD_DIM
V_BLOCK = K_BLOCK + A_KV_HEADS

MOD_ROWS = 8
ADA_TILE = 1024
IN_TOKEN_TILE = 1024
OUT_TOKEN_TILE = 256
ATTN_Q_TILE = 256
MIX_TILE = 256
VMEM_LIMIT = 56 * 1024 * 1024


def _params(*sem):
    return pltpu.CompilerParams(dimension_semantics=sem, vmem_limit_bytes=VMEM_LIMIT)


def _silu(x):
    return x * jax.nn.sigmoid(x)


def _rms(x, w):
    ms = jnp.mean(x * x, axis=-1, keepdims=True)
    return x * lax.rsqrt(ms + EPS) * w


def _ada_kernel(cv_ref, w_ref, b_ref, o_ref):
    a = _silu(cv_ref[...]).astype(BF16)
    o_ref[...] = jnp.dot(a, w_ref[...].astype(BF16), preferred_element_type=F32) + b_ref[...]


def _ada_call(cv, w_ada, b_ada):
    return pl.pallas_call(
        _ada_kernel,
        out_shape=jax.ShapeDtypeStruct((DEPTH, MOD_ROWS, GATE_COLS), F32),
        grid=(DEPTH, GATE_COLS // ADA_TILE),
        in_specs=[
            pl.BlockSpec((MOD_ROWS, D_MODEL), lambda l, j: (0, 0)),
            pl.BlockSpec((None, D_MODEL, ADA_TILE), lambda l, j: (l, 0, j)),
            pl.BlockSpec((None, 1, ADA_TILE), lambda l, j: (l, 0, j)),
        ],
        out_specs=pl.BlockSpec((None, MOD_ROWS, ADA_TILE), lambda l, j: (l, 0, j)),
        compiler_params=_params("arbitrary", "arbitrary"),
        name="ada_mod",
    )(cv, w_ada, b_ada)


def _rope(y, cos, sa, sb):
    return y * cos + pltpu.roll(y, 96, 1) * sa + pltpu.roll(y, 32, 1) * sb


def _inproj_kernel(*refs, rope, state):
    x_ref, mod_ref, nw_ref, w_ref, bm_ref, qnw_ref, knw_ref = refs[:7]
    refs = refs[7:]
    if rope:
        cos_ref, sa_ref, sb_ref = refs[:3]
        refs = refs[3:]
    proj_ref = refs[0]
    refs = refs[1:]
    if state:
        kv_ref = refs[0]
        refs = refs[1:]
    (h_scr,) = refs
    j = pl.program_id(1)

    @pl.when(j == 0)
    def _():
        shift = mod_ref[:, 0:D_MODEL]
        scale = mod_ref[:, D_MODEL:2 * D_MODEL]
        h = _rms(x_ref[...], nw_ref[...]) * (1.0 + scale) + shift
        h_scr[...] = h.astype(BF16)

    acc = jnp.dot(h_scr[...], w_ref[...], preferred_element_type=F32)

    def head(xh, w):
        y = _rms(xh, w)
        if rope:
            y = _rope(y, cos_ref[...], sa_ref[...], sb_ref[...])
        return y

    @pl.when(j < GATE_TILES)
    def _():
        proj_ref[...] = jax.nn.sigmoid(acc + bm_ref[...]).astype(BF16)

    @pl.when((j >= Q_TILE0) & (j < KV_TILE))
    def _():
        w = qnw_ref[...] * ATTN_SCALE
        for hh in range(HEADS_PER_TILE):
            cs = slice(hh * HEAD_DIM, (hh + 1) * HEAD_DIM)
            proj_ref[:, cs] = head(acc[:, cs], w).astype(BF16)

    @pl.when(j == KV_TILE)
    def _():
        for hh in range(A_KV_HEADS):
            cs = slice(hh * HEAD_DIM, (hh + 1) * HEAD_DIM)
            xh = acc[:, cs]
            if state:
                kv_ref[:, cs] = _rms(xh, knw_ref[...])
            proj_ref[:, cs] = head(xh, knw_ref[...]).astype(BF16)
        v = acc[:, KV_WIDTH:]
        if state:
            kv_ref[:, KV_WIDTH:] = v
        proj_ref[:, KV_WIDTH:] = v.astype(BF16)

    is_silu = ((j == GA_TILE0) | (j == GA_TILE0 + 1) | (j == GB_TILE) | (j == GC_TILE))

    @pl.when(is_silu)
    def _():
        proj_ref[...] = _silu(acc).astype(BF16)

    @pl.when((j == U_TILE) | (j == VB_TILE) | (j == Z_TILE))
    def _():
        proj_ref[...] = acc.astype(BF16)


def _inproj_call(x2d, mod3, norm_w, w_cat, b_merge, qnw, knw, rope_tabs, *, seq, n_batch_rows, state):
    t = x2d.shape[0]
    tm = IN_TOKEN_TILE
    rope = rope_tabs is not None
    tiles_per_seq = max(seq // tm, 1)

    def mod_idx(i, j):
        if n_batch_rows == 0:
            return (0, 0, 0)
        return (1 + i // tiles_per_seq, 0, 0)

    in_specs = [
        pl.BlockSpec((tm, D_MODEL), lambda i, j: (i, 0)),
        pl.BlockSpec((None, 1, GATE_COLS), mod_idx),
        pl.BlockSpec((1, D_MODEL), lambda i, j: (0, 0)),
        pl.BlockSpec((D_MODEL, COL_TILE), lambda i, j: (0, j)),
        pl.BlockSpec((1, COL_TILE), lambda i, j: (0, jnp.minimum(j, GATE_TILES - 1))),
        pl.BlockSpec((1, HEAD_DIM), lambda i, j: (0, 0)),
        pl.BlockSpec((1, HEAD_DIM), lambda i, j: (0, 0)),
    ]
    args = [x2d, mod3, norm_w, w_cat, b_merge, qnw, knw]
    if rope:
        for tab in rope_tabs:
            in_specs.append(pl.BlockSpec((tm, HEAD_DIM), lambda i, j: (i % tiles_per_seq, 0)))
            args.append(tab)
    out_shape = [jax.ShapeDtypeStruct((t, CAT_COLS), BF16)]
    out_specs = [pl.BlockSpec((tm, COL_TILE), lambda i, j: (i, j))]
    if state:
        out_shape.append(jax.ShapeDtypeStruct((t, 2 * KV_WIDTH), F32))
        out_specs.append(pl.BlockSpec((tm, 2 * KV_WIDTH), lambda i, j: (i, 0)))
    res = pl.pallas_call(
        functools.partial(_inproj_kernel, rope=rope, state=state),
        out_shape=out_shape,
        grid=(t // tm, N_COL_TILES),
        in_specs=in_specs,
        out_specs=out_specs,
        scratch_shapes=[pltpu.VMEM((tm, D_MODEL), BF16)],
        compiler_params=_params("arbitrary", "arbitrary"),
        name="in_proj_lat" if rope else "in_proj_ctx",
    )(*args)
    return res


def _attn_kernel(*refs, ctx):
    q_ref, k_ref, v_ref, ga_ref = refs[:4]
    refs = refs[4:]
    if ctx:
        ck_ref, cv_ref = refs[:2]
        refs = refs[2:]
        ck = ck_ref[...].astype(BF16)
        cv = cv_ref[...].astype(BF16)
    (o_ref,) = refs
    nt = (((1,), (1,)), ((), ()))
    k = k_ref[...]
    v = v_ref[...]
    for hh in range(HEADS_PER_TILE):
        cs = slice(hh * HEAD_DIM, (hh + 1) * HEAD_DIM)
        q = q_ref[:, cs]
        s1 = lax.dot_general(q, k, nt, preferred_element_type=F32)
        m = jnp.max(s1, axis=-1, keepdims=True)
        if ctx:
            s2 = lax.dot_general(q, ck, nt, preferred_element_type=F32)
            m = jnp.maximum(m, jnp.max(s2, axis=-1, keepdims=True))
        p1 = jnp.exp(s1 - m)
        l = jnp.sum(p1, axis=-1, keepdims=True)
        o = jnp.dot(p1.astype(BF16), v, preferred_element_type=F32)
        if ctx:
            p2 = jnp.exp(s2 - m)
            l = l + jnp.sum(p2, axis=-1, keepdims=True)
            o = o + jnp.dot(p2.astype(BF16), cv, preferred_element_type=F32)
        o_ref[:, cs] = (o / l * ga_ref[:, cs].astype(F32)).astype(BF16)


def _attn_call(proj, cache, *, n_batch, seq, layer):
    t = proj.shape[0]
    tq = min(ATTN_Q_TILE, seq)
    nq = seq // tq
    ctx = cache is not None
    in_specs = [
        pl.BlockSpec((tq, COL_TILE), lambda b, kh, qi: (b * nq + qi, Q_TILE0 + kh)),
        pl.BlockSpec((seq, HEAD_DIM), lambda b, kh, qi: (b, K_BLOCK + kh)),
        pl.BlockSpec((seq, HEAD_DIM), lambda b, kh, qi: (b, V_BLOCK + kh)),
        pl.BlockSpec((tq, COL_TILE), lambda b, kh, qi: (b * nq + qi, GA_TILE0 + kh)),
    ]
    args = [proj, proj, proj, proj]
    if ctx:
        past = cache[0].shape[2]
        for c in cache:
            in_specs.append(pl.BlockSpec((None, None, past, HEAD_DIM), lambda b, kh, qi: (b, layer, 0, kh)))
            args.append(c)
    return pl.pallas_call(
        functools.partial(_attn_kernel, ctx=ctx),
        out_shape=jax.ShapeDtypeStruct((t, A_WIDTH), BF16),
        grid=(n_batch, A_KV_HEADS, nq),
        in_specs=in_specs,
        out_specs=pl.BlockSpec((tq, COL_TILE), lambda b, kh, qi: (b * nq + qi, kh)),
        compiler_params=_params("arbitrary", "arbitrary", "arbitrary"),
        name="attention_lat" if ctx else "attention_ctx",
    )(*args)


def _mix_kernel(u_ref, vb_ref, gb_ref, z_ref, gc_ref, snw_ref, ws_ref, bs_ref, wp_ref, ps_ref,
                bo_ref, co_ref, zp_scr, *, seq):
    r = MIX_TILE
    c = pl.program_id(1)
    nc = pl.num_programs(1)
    base = pl.multiple_of(c * r, r)

    for cc in range(r // CHUNK):
        rs = slice(cc * CHUNK, (cc + 1) * CHUNK)
        vbn = _rms(vb_ref[rs, :].astype(F32), snw_ref[...]).astype(BF16)
        for g in range(B_GROUPS):
            cs = slice(g * HEAD_DIM, (g + 1) * HEAD_DIM)
            mixed = jnp.dot(ws_ref[g], vbn[:, cs], preferred_element_type=F32) + bs_ref[g]
            bo_ref[rs, cs] = (u_ref[rs, cs].astype(F32) * mixed * gb_ref[rs, cs].astype(F32)).astype(BF16)

    zp_scr[POOL_HALO:POOL_HALO + r, :] = z_ref[pl.ds(base, r), :].astype(F32)

    @pl.when(c == 0)
    def _():
        zp_scr[0:POOL_HALO, :] = jnp.zeros((POOL_HALO, C_WIDTH), F32)

    @pl.when(c > 0)
    def _():
        zp_scr[0:POOL_HALO, :] = z_ref[pl.ds(pl.multiple_of(base - POOL_HALO, POOL_HALO), POOL_HALO), :].astype(F32)

    @pl.when(c == nc - 1)
    def _():
        zp_scr[POOL_HALO + r:, :] = jnp.zeros((POOL_HALO, C_WIDTH), F32)

    @pl.when(c < nc - 1)
    def _():
        zp_scr[POOL_HALO + r:, :] = z_ref[pl.ds(pl.multiple_of(base + r, POOL_HALO), POOL_HALO), :].astype(F32)

    t = (base + lax.broadcasted_iota(jnp.int32, (r, HEAD_DIM), 0))
    for g, w in enumerate(POOL_WINDOWS):
        cs = slice(g * HEAD_DIM, (g + 1) * HEAD_DIM)
        half = w // 2
        acc = zp_scr[POOL_HALO - half:POOL_HALO - half + r, cs]
        for o in range(-half + 1, half):
            acc = acc + zp_scr[POOL_HALO + o:POOL_HALO + o + r, cs]
        cnt = (jnp.minimum(t + half, seq) - jnp.maximum(t - half, 0)).astype(F32)
        d = acc / cnt - zp_scr[POOL_HALO:POOL_HALO + r, cs]
        dm = jnp.dot(d.astype(BF16), wp_ref[g], preferred_element_type=F32)
        co_ref[:, cs] = (dm * ps_ref[:, cs] * gc_ref[:, cs].astype(F32)).astype(BF16)


def _mix_call(proj, snw, ws, bs, wp, ps, *, n_batch, seq):
    t = proj.shape[0]
    r = MIX_TILE
    nc = seq // r
    row = lambda b, c: b * nc + c
    tile = lambda col: pl.BlockSpec((r, COL_TILE), lambda b, c: (row(b, c), col))
    const3 = lambda shape: pl.BlockSpec(shape, lambda b, c: (0, 0, 0))
    return pl.pallas_call(
        functools.partial(_mix_kernel, seq=seq),
        out_shape=[jax.ShapeDtypeStruct((t, B_WIDTH), BF16), jax.ShapeDtypeStruct((t, C_WIDTH), BF16)],
        grid=(n_batch, nc),
        in_specs=[
            tile(U_TILE), tile(VB_TILE), tile(GB_TILE),
            pl.BlockSpec((seq, COL_TILE), lambda b, c: (b, Z_TILE)),
            tile(GC_TILE),
            pl.BlockSpec((1, B_WIDTH), lambda b, c: (0, 0)),
            const3((B_GROUPS, CHUNK, CHUNK)),
            const3((B_GROUPS, CHUNK, 1)),
            const3((B_GROUPS, HEAD_DIM, HEAD_DIM)),
            pl.BlockSpec((1, C_WIDTH), lambda b, c: (0, 0)),
        ],
        out_specs=[pl.BlockSpec((r, B_WIDTH), lambda b, c: (row(b, c), 0)),
                   pl.BlockSpec((r, C_WIDTH), lambda b, c: (row(b, c), 0))],
        scratch_shapes=[pltpu.VMEM((r + 2 * POOL_HALO, C_WIDTH), F32)],
        compiler_params=_params("arbitrary", "arbitrary"),
        name="sgu_pool",
    )(proj, proj, proj, proj, proj, snw, ws, bs, wp, ps)


def _outproj_kernel(*refs, final):
    (attn_ref, b_ref, c_ref, g0_ref, g1_ref, g2_ref, x_ref, mod_ref,
     wa_ref, wb_ref, wc_ref, wo_ref) = refs[:12]
    refs = refs[12:]
    if final:
        fnw_ref = refs[0]
        refs = refs[1:]
    y_ref, m_scr = refs
    for n in range(D_MODEL // COL_TILE):
        cs = slice(n * COL_TILE, (n + 1) * COL_TILE)
        a = jnp.dot(attn_ref[...], wa_ref[:, cs], preferred_element_type=F32)
        b = jnp.dot(b_ref[...], wb_ref[:, cs], preferred_element_type=F32)
        c = jnp.dot(c_ref[...], wc_ref[:, cs], preferred_element_type=F32)
        m = (g0_ref[:, cs].astype(F32) * a + g1_ref[:, cs].astype(F32) * b
             + g2_ref[:, cs].astype(F32) * c)
        m_scr[:, cs] = m.astype(BF16)
    out = jnp.dot(m_scr[...], wo_ref[...], preferred_element_type=F32)
    y = x_ref[...] + mod_ref[:, 2 * D_MODEL:] * out
    if final:
        y = _rms(y, fnw_ref[...])
    y_ref[...] = y


def _outproj_call(attn, bout, cout, proj, x2d, mod3, wa, wb, wc, wo, fnw, *, seq, n_batch_rows):
    t = x2d.shape[0]
    tm = OUT_TOKEN_TILE
    final = fnw is not None
    tiles_per_seq = max(seq // tm, 1)

    def mod_idx(i):
        if n_batch_rows == 0:
            return (0, 0, 0)
        return (1 + i // tiles_per_seq, 0, 0)

    resident = lambda shape: pl.BlockSpec(shape, lambda i: (0, 0), pipeline_mode=pl.Buffered(1))
    gate = lambda g: pl.BlockSpec((tm, D_MODEL), lambda i: (i, g))
    in_specs = [
        pl.BlockSpec((tm, A_WIDTH), lambda i: (i, 0)),
        pl.BlockSpec((tm, B_WIDTH), lambda i: (i, 0)),
        pl.BlockSpec((tm, C_WIDTH), lambda i: (i, 0)),
        gate(0), gate(1), gate(2),
        pl.BlockSpec((tm, D_MODEL), lambda i: (i, 0)),
        pl.BlockSpec((None, 1, GATE_COLS), mod_idx),
        resident((A_WIDTH, D_MODEL)), resident((B_WIDTH, D_MODEL)), resident((C_WIDTH, D_MODEL)),
        resident((D_MODEL, D_MODEL)),
    ]
    args = [attn, bout, cout, proj, proj, proj, x2d, mod3, wa, wb, wc, wo]
    if final:
        in_specs.append(pl.BlockSpec((1, D_MODEL), lambda i: (0, 0)))
        args.append(fnw)
    return pl.pallas_call(
        functools.partial(_outproj_kernel, final=final),
        out_shape=jax.ShapeDtypeStruct((t, D_MODEL), F32),
        grid=(t // tm,),
        in_specs=in_specs,
        out_specs=pl.BlockSpec((tm, D_MODEL), lambda i: (i, 0)),
        scratch_shapes=[pltpu.VMEM((tm, D_MODEL), BF16)],
        compiler_params=_params("arbitrary"),
        name="out_proj_final" if final else "out_proj",
    )(*args)


def _rope_tables(n_tokens):
    rows = n_tokens // GRID_W
    row = jnp.repeat(jnp.arange(rows), GRID_W).astype(F32)
    col = jnp.tile(jnp.arange(GRID_W), rows).astype(F32)
    n_freq = HEAD_DIM // 4
    inv = ROPE_THETA ** (-jnp.arange(n_freq, dtype=F32) / n_freq)
    ar = row[:, None] * inv[None, :]
    ac = col[:, None] * inv[None, :]
    zero = jnp.zeros_like(ar)
    cos = jnp.concatenate([jnp.cos(ar), jnp.cos(ar), jnp.cos(ac), jnp.cos(ac)], axis=-1)
    sa = jnp.concatenate([-jnp.sin(ar), zero, -jnp.sin(ac), zero], axis=-1)
    sb = jnp.concatenate([zero, jnp.sin(ar), zero, jnp.sin(ac)], axis=-1)
    return cos, sa, sb


def kernel(x_prompt, x_sample, cache_k, cache_v, c, c_ctx, norm_w, w_ada, b_ada, w_in, q_norm_w,
           k_norm_w, sgu_norm_w, w_sgu, b_sgu, w_pool, pool_scale, w_br_a, w_br_b, w_br_c, w_merge,
           b_merge, w_out, final_norm_w):
    nb_p, seq_p, d = x_prompt.shape
    nb_s, seq_s, _ = x_sample.shape
    past = cache_k.shape[2]
    assert d == D_MODEL and nb_s + 1 <= MOD_ROWS

    cv = jnp.concatenate([c_ctx[None, :], c, jnp.zeros((MOD_ROWS - 1 - nb_s, d), F32)], axis=0)
    mod = _ada_call(cv, w_ada, b_ada.reshape(DEPTH, 1, GATE_COLS))
    rope_tabs = _rope_tables(seq_s)
    cache = (cache_k.reshape(nb_s, DEPTH, past, KV_WIDTH), cache_v.reshape(nb_s, DEPTH, past, KV_WIDTH))

    xp = x_prompt.reshape(nb_p * seq_p, d)
    xs = x_sample.reshape(nb_s * seq_s, d)
    states = []
    for l in range(DEPTH):
        w_cat = jnp.concatenate([w_merge[l], w_in[l]], axis=1).astype(BF16)
        wa, wb, wc, wo = (w.astype(BF16) for w in (w_br_a[l], w_br_b[l], w_br_c[l], w_out[l]))
        mod3 = mod[l].reshape(MOD_ROWS, 1, GATE_COLS)
        nw = norm_w[l].reshape(1, d)
        bm = b_merge[l].reshape(1, GATE_COLS)
        qnw = q_norm_w[l].reshape(1, HEAD_DIM)
        knw = k_norm_w[l].reshape(1, HEAD_DIM)
        mix_w = (sgu_norm_w[l].reshape(1, B_WIDTH), w_sgu[l].astype(BF16),
                 b_sgu[l].reshape(B_GROUPS, CHUNK, 1), w_pool[l].astype(BF16),
                 pool_scale[l].reshape(1, C_WIDTH))
        fnw = final_norm_w.reshape(1, d) if l == DEPTH - 1 else None

        proj, kv = _inproj_call(xp, mod3, nw, w_cat, bm, qnw, knw, None,
                                seq=seq_p, n_batch_rows=0, state=True)
        states.append(kv)
        attn = _attn_call(proj, None, n_batch=nb_p, seq=seq_p, layer=l)
        bout, cout = _mix_call(proj, *mix_w, n_batch=nb_p, seq=seq_p)
        xp = _outproj_call(attn, bout, cout, proj, xp, mod3, wa, wb, wc, wo, fnw,
                           seq=seq_p, n_batch_rows=0)

        (proj,) = _inproj_call(xs, mod3, nw, w_cat, bm, qnw, knw, rope_tabs,
                               seq=seq_s, n_batch_rows=nb_s, state=False)
        attn = _attn_call(proj, cache, n_batch=nb_s, seq=seq_s, layer=l)
        bout, cout = _mix_call(proj, *mix_w, n_batch=nb_s, seq=seq_s)
        xs = _outproj_call(attn, bout, cout, proj, xs, mod3, wa, wb, wc, wo, fnw,
                           seq=seq_s, n_batch_rows=nb_s)

    kv = jnp.stack([s.reshape(nb_p, seq_p, 2 * KV_WIDTH) for s in states], axis=1)
    state_k = kv[..., :KV_WIDTH].reshape(nb_p, DEPTH, seq_p, A_KV_HEADS, HEAD_DIM)
    state_v = kv[..., KV_WIDTH:].reshape(nb_p, DEPTH, seq_p, A_KV_HEADS, HEAD_DIM)
    return (xp.reshape(nb_p, seq_p, d), xs.reshape(nb_s, seq_s, d), state_k, state_v)
```

```python
import functools

import jax
import jax.numpy as jnp
from jax import lax
from jax.experimental import pallas as pl
from jax.experimental.pallas import tpu as pltpu

F32 = jnp.float32
BF16 = jnp.bfloat16

D_MODEL = 2048
DEPTH = 2
GRID_W = 64
EPS = 1e-6
HEAD_DIM = 128
A_HEADS = 8
A_KV_HEADS = 2
A_WIDTH = A_HEADS * HEAD_DIM
KV_WIDTH = A_KV_HEADS * HEAD_DIM
ROPE_THETA = 10000.0
ATTN_SCALE = HEAD_DIM ** -0.5
CHUNK = 128
B_GROUPS = 4
B_WIDTH = 512
C_WIDTH = 512
POOL_WINDOWS = (2, 4, 8, 16)
POOL_HALO = 16
N_BRANCH = 3
GATE_COLS = N_BRANCH * D_MODEL
IN_COLS = 2 * A_WIDTH + 2 * KV_WIDTH + 3 * B_WIDTH + 2 * C_WIDTH

COL_TILE = 512
Q_TILE0 = 0
KV_TILE = Q_TILE0 + A_WIDTH // COL_TILE
GA_TILE0 = KV_TILE + 1
U_TILE = GA_TILE0 + A_WIDTH // COL_TILE
VB_TILE = U_TILE + 1
GB_TILE = VB_TILE + 1
Z_TILE = GB_TILE + 1
GC_TILE = Z_TILE + 1
HEADS_PER_TILE = COL_TILE // HEAD_DIM
K_BLOCK = (KV_TILE * COL_TILE) // HEAD_DIM
V_BLOCK = K_BLOCK + A_KV_HEADS

MXU_COLS = 256
MOD_ROWS = 8
ADA_TILE = 1024
IN_TOKEN_TILE = 512
OUT_TOKEN_TILE = 256
ATTN_Q_TILE = 256
MIX_TILE = 256
VMEM_LIMIT = 56 * 1024 * 1024


def _params(*sem):
    return pltpu.CompilerParams(dimension_semantics=sem, vmem_limit_bytes=VMEM_LIMIT)


def _resident(shape):
    return pl.BlockSpec(shape, lambda *_: (0,) * len(shape), pipeline_mode=pl.Buffered(1))


def _silu(x):
    return x * jax.nn.sigmoid(x)


def _rms(x, w):
    ms = jnp.mean(x * x, axis=-1, keepdims=True)
    return x * lax.rsqrt(ms + EPS) * w


def _mod_index(n_batch_rows, tiles_per_seq):
    def idx(i):
        if n_batch_rows == 0:
            return (0, 0, 0)
        return (1 + i // tiles_per_seq, 0, 0)
    return idx


def _ada_kernel(cv_ref, w_ref, b_ref, o_ref):
    a = _silu(cv_ref[...]).astype(BF16)
    o_ref[...] = jnp.dot(a, w_ref[...].astype(BF16), preferred_element_type=F32) + b_ref[...]


def _ada_call(cv, w_ada, b_ada):
    return pl.pallas_call(
        _ada_kernel,
        out_shape=jax.ShapeDtypeStruct((DEPTH, MOD_ROWS, GATE_COLS), F32),
        grid=(DEPTH, GATE_COLS // ADA_TILE),
        in_specs=[
            pl.BlockSpec((MOD_ROWS, D_MODEL), lambda l, j: (0, 0)),
            pl.BlockSpec((None, D_MODEL, ADA_TILE), lambda l, j: (l, 0, j)),
            pl.BlockSpec((None, 1, ADA_TILE), lambda l, j: (l, 0, j)),
        ],
        out_specs=pl.BlockSpec((None, MOD_ROWS, ADA_TILE), lambda l, j: (l, 0, j)),
        compiler_params=_params("arbitrary", "arbitrary"),
        name="ada_mod",
    )(cv, w_ada, b_ada)


def _normmod_kernel(x_ref, mod_ref, nw_ref, h_ref):
    shift = mod_ref[:, 0:D_MODEL]
    scale = mod_ref[:, D_MODEL:2 * D_MODEL]
    h_ref[...] = (_rms(x_ref[...], nw_ref[...]) * (1.0 + scale) + shift).astype(BF16)


def _normmod_call(x2d, mod3, norm_w, *, seq, n_batch_rows):
    t = x2d.shape[0]
    tm = IN_TOKEN_TILE
    return pl.pallas_call(
        _normmod_kernel,
        out_shape=jax.ShapeDtypeStruct((t, D_MODEL), BF16),
        grid=(t // tm,),
        in_specs=[
            pl.BlockSpec((tm, D_MODEL), lambda i: (i, 0)),
            pl.BlockSpec((None, 1, GATE_COLS), _mod_index(n_batch_rows, max(seq // tm, 1))),
            pl.BlockSpec((1, D_MODEL), lambda i: (0, 0)),
        ],
        out_specs=pl.BlockSpec((tm, D_MODEL), lambda i: (i, 0)),
        compiler_params=_params("arbitrary"),
        name="norm_mod",
    )(x2d, mod3, norm_w)


def _gates_kernel(h_ref, w_ref, b_ref, o_ref):
    for s in range(GATE_COLS // MXU_COLS):
        cs = slice(s * MXU_COLS, (s + 1) * MXU_COLS)
        acc = jnp.dot(h_ref[...], w_ref[:, cs], preferred_element_type=F32)
        o_ref[:, cs] = jax.nn.sigmoid(acc + b_ref[:, cs]).astype(BF16)


def _gates_call(h, w_merge, b_merge):
    t = h.shape[0]
    tm = IN_TOKEN_TILE
    return pl.pallas_call(
        _gates_kernel,
        out_shape=jax.ShapeDtypeStruct((t, GATE_COLS), BF16),
        grid=(t // tm,),
        in_specs=[
            pl.BlockSpec((tm, D_MODEL), lambda i: (i, 0)),
            _resident((D_MODEL, GATE_COLS)),
            _resident((1, GATE_COLS)),
        ],
        out_specs=pl.BlockSpec((tm, GATE_COLS), lambda i: (i, 0)),
        compiler_params=_params("arbitrary"),
        name="gates",
    )(h, w_merge, b_merge)


def _rope(y, cos, sa, sb):
    return y * cos + pltpu.roll(y, 96, 1) * sa + pltpu.roll(y, 32, 1) * sb


def _proj_kernel(*refs, rope, state):
    h_ref, w_ref, qnw_ref, knw_ref = refs[:4]
    refs = refs[4:]
    if rope:
        cos_ref, sa_ref, sb_ref = refs[:3]
        refs = refs[3:]
    proj_ref = refs[0]
    if state:
        kv_ref = refs[1]

    def head(xh, w):
        y = _rms(xh, w)
        if rope:
            y = _rope(y, cos_ref[...], sa_ref[...], sb_ref[...])
        return y

    qw = qnw_ref[...] * ATTN_SCALE
    k0 = KV_TILE * COL_TILE
    v0 = k0 + KV_WIDTH
    silu_cols = ((GA_TILE0 * COL_TILE, U_TILE * COL_TILE), (GB_TILE * COL_TILE, Z_TILE * COL_TILE),
                 (GC_TILE * COL_TILE, IN_COLS))
    for s in range(IN_COLS // MXU_COLS):
        c0 = s * MXU_COLS
        cs = slice(c0, c0 + MXU_COLS)
        acc = jnp.dot(h_ref[...], w_ref[:, cs], preferred_element_type=F32)
        if c0 < k0:
            for hh in range(MXU_COLS // HEAD_DIM):
                hs = slice(hh * HEAD_DIM, (hh + 1) * HEAD_DIM)
                proj_ref[:, c0 + hh * HEAD_DIM:c0 + (hh + 1) * HEAD_DIM] = head(acc[:, hs], qw).astype(BF16)
        elif c0 < v0:
            for hh in range(MXU_COLS // HEAD_DIM):
                hs = slice(hh * HEAD_DIM, (hh + 1) * HEAD_DIM)
                os = slice(c0 + hh * HEAD_DIM, c0 + (hh + 1) * HEAD_DIM)
                if state:
                    kv_ref[:, hh * HEAD_DIM:(hh + 1) * HEAD_DIM] = _rms(acc[:, hs], knw_ref[...])
                proj_ref[:, os] = head(acc[:, hs], knw_ref[...]).astype(BF16)
        elif c0 < v0 + KV_WIDTH:
            if state:
                kv_ref[:, KV_WIDTH:] = acc
            proj_ref[:, cs] = acc.astype(BF16)
        elif any(lo <= c0 < hi for lo, hi in silu_cols):
            proj_ref[:, cs] = _silu(acc).astype(BF16)
        else:
            proj_ref[:, cs] = acc.astype(BF16)


def _proj_call(h, w_in, qnw, knw, rope_tabs, *, seq, state):
    t = h.shape[0]
    tm = IN_TOKEN_TILE
    rope = rope_tabs is not None
    tiles_per_seq = max(seq // tm, 1)
    in_specs = [
        pl.BlockSpec((tm, D_MODEL), lambda i: (i, 0)),
        _resident((D_MODEL, IN_COLS)),
        pl.BlockSpec((1, HEAD_DIM), lambda i: (0, 0)),
        pl.BlockSpec((1, HEAD_DIM), lambda i: (0, 0)),
    ]
    args = [h, w_in, qnw, knw]
    if rope:
        for tab in rope_tabs:
            in_specs.append(pl.BlockSpec((tm, HEAD_DIM), lambda i: (i % tiles_per_seq, 0)))
            args.append(tab)
    out_shape = [jax.ShapeDtypeStruct((t, IN_COLS), BF16)]
    out_specs = [pl.BlockSpec((tm, IN_COLS), lambda i: (i, 0))]
    if state:
        out_shape.append(jax.ShapeDtypeStruct((t, 2 * KV_WIDTH), F32))
        out_specs.append(pl.BlockSpec((tm, 2 * KV_WIDTH), lambda i: (i, 0)))
    return pl.pallas_call(
        functools.partial(_proj_kernel, rope=rope, state=state),
        out_shape=out_shape,
        grid=(t // tm,),
        in_specs=in_specs,
        out_specs=out_specs,
        compiler_params=_params("arbitrary"),
        name="proj_lat" if rope else "proj_ctx",
    )(*args)


def _attn_kernel(*refs, ctx):
    q_ref, k_ref, v_ref, ga_ref = refs[:4]
    refs = refs[4:]
    if ctx:
        ck_ref, cv_ref = refs[:2]
        refs = refs[2:]
        ck = ck_ref[...].astype(BF16)
        cv = cv_ref[...].astype(BF16)
    (o_ref,) = refs
    nt = (((1,), (1,)), ((), ()))
    k = k_ref[...]
    v = v_ref[...]
    for hh in range(HEADS_PER_TILE):
        cs = slice(hh * HEAD_DIM, (hh + 1) * HEAD_DIM)
        q = q_ref[:, cs]
        s1 = lax.dot_general(q, k, nt, preferred_element_type=F32)
        m = jnp.max(s1, axis=-1, keepdims=True)
        if ctx:
            s2 = lax.dot_general(q, ck, nt, preferred_element_type=F32)
            m = jnp.maximum(m, jnp.max(s2, axis=-1, keepdims=True))
        p1 = jnp.exp(s1 - m)
        l = jnp.sum(p1, axis=-1, keepdims=True)
        o = jnp.dot(p1.astype(BF16), v, preferred_element_type=F32)
        if ctx:
            p2 = jnp.exp(s2 - m)
            l = l + jnp.sum(p2, axis=-1, keepdims=True)
            o = o + jnp.dot(p2.astype(BF16), cv, preferred_element_type=F32)
        o_ref[:, cs] = (o / l * ga_ref[:, cs].astype(F32)).astype(BF16)


def _attn_call(proj, cache, *, n_batch, seq, layer):
    t = proj.shape[0]
    tq = min(ATTN_Q_TILE, seq)
    nq = seq // tq
    ctx = cache is not None
    in_specs = [
        pl.BlockSpec((tq, COL_TILE), lambda b, kh, qi: (b * nq + qi, Q_TILE0 + kh)),
        pl.BlockSpec((seq, HEAD_DIM), lambda b, kh, qi: (b, K_BLOCK + kh)),
        pl.BlockSpec((seq, HEAD_DIM), lambda b, kh, qi: (b, V_BLOCK + kh)),
        pl.BlockSpec((tq, COL_TILE), lambda b, kh, qi: (b * nq + qi, GA_TILE0 + kh)),
    ]
    args = [proj, proj, proj, proj]
    if ctx:
        past = cache[0].shape[2]
        for c in cache:
            in_specs.append(pl.BlockSpec((None, None, past, HEAD_DIM), lambda b, kh, qi: (b, layer, 0, kh)))
            args.append(c)
    return pl.pallas_call(
        functools.partial(_attn_kernel, ctx=ctx),
        out_shape=jax.ShapeDtypeStruct((t, A_WIDTH), BF16),
        grid=(n_batch, A_KV_HEADS, nq),
        in_specs=in_specs,
        out_specs=pl.BlockSpec((tq, COL_TILE), lambda b, kh, qi: (b * nq + qi, kh)),
        compiler_params=_params("arbitrary", "arbitrary", "arbitrary"),
        name="attention_lat" if ctx else "attention_ctx",
    )(*args)


def _mix_kernel(u_ref, vb_ref, gb_ref, z_ref, gc_ref, snw_ref, ws_ref, bs_ref, wp_ref, ps_ref,
                bo_ref, co_ref, zp_scr, *, seq):
    r = MIX_TILE
    c = pl.program_id(1)
    nc = pl.num_programs(1)
    base = pl.multiple_of(c * r, r)

    for cc in range(r // CHUNK):
        rs = slice(cc * CHUNK, (cc + 1) * CHUNK)
        vbn = _rms(vb_ref[rs, :].astype(F32), snw_ref[...]).astype(BF16)
        for g in range(B_GROUPS):
            cs = slice(g * HEAD_DIM, (g + 1) * HEAD_DIM)
            mixed = jnp.dot(ws_ref[g], vbn[:, cs], preferred_element_type=F32) + bs_ref[g]
            bo_ref[rs, cs] = (u_ref[rs, cs].astype(F32) * mixed * gb_ref[rs, cs].astype(F32)).astype(BF16)

    zp_scr[POOL_HALO:POOL_HALO + r, :] = z_ref[pl.ds(base, r), :].astype(F32)

    @pl.when(c == 0)
    def _():
        zp_scr[0:POOL_HALO, :] = jnp.zeros((POOL_HALO, C_WIDTH), F32)

    @pl.when(c > 0)
    def _():
        zp_scr[0:POOL_HALO, :] = z_ref[pl.ds(pl.multiple_of(base - POOL_HALO, POOL_HALO), POOL_HALO), :].astype(F32)

    @pl.when(c == nc - 1)
    def _():
        zp_scr[POOL_HALO + r:, :] = jnp.zeros((POOL_HALO, C_WIDTH), F32)

    @pl.when(c < nc - 1)
    def _():
        zp_scr[POOL_HALO + r:, :] = z_ref[pl.ds(pl.multiple_of(base + r, POOL_HALO), POOL_HALO), :].astype(F32)

    t = (base + lax.broadcasted_iota(jnp.int32, (r, HEAD_DIM), 0))
    for g, w in enumerate(POOL_WINDOWS):
        cs = slice(g * HEAD_DIM, (g + 1) * HEAD_DIM)
        half = w // 2
        acc = zp_scr[POOL_HALO - half:POOL_HALO - half + r, cs]
        for o in range(-half + 1, half):
            acc = acc + zp_scr[POOL_HALO + o:POOL_HALO + o + r, cs]
        cnt = (jnp.minimum(t + half, seq) - jnp.maximum(t - half, 0)).astype(F32)
        d = acc / cnt - zp_scr[POOL_HALO:POOL_HALO + r, cs]
        dm = jnp.dot(d.astype(BF16), wp_ref[g], preferred_element_type=F32)
        co_ref[:, cs] = (dm * ps_ref[:, cs] * gc_ref[:, cs].astype(F32)).astype(BF16)


def _mix_call(proj, snw, ws, bs, wp, ps, *, n_batch, seq):
    t = proj.shape[0]
    r = MIX_TILE
    nc = seq // r
    row = lambda b, c: b * nc + c
    tile = lambda col: pl.BlockSpec((r, COL_TILE), lambda b, c: (row(b, c), col))
    const3 = lambda shape: pl.BlockSpec(shape, lambda b, c: (0, 0, 0))
    return pl.pallas_call(
        functools.partial(_mix_kernel, seq=seq),
        out_shape=[jax.ShapeDtypeStruct((t, B_WIDTH), BF16), jax.ShapeDtypeStruct((t, C_WIDTH), BF16)],
        grid=(n_batch, nc),
        in_specs=[
            tile(U_TILE), tile(VB_TILE), tile(GB_TILE),
            pl.BlockSpec((seq, COL_TILE), lambda b, c: (b, Z_TILE)),
            tile(GC_TILE),
            pl.BlockSpec((1, B_WIDTH), lambda b, c: (0, 0)),
            const3((B_GROUPS, CHUNK, CHUNK)),
            const3((B_GROUPS, CHUNK, 1)),
            const3((B_GROUPS, HEAD_DIM, HEAD_DIM)),
            pl.BlockSpec((1, C_WIDTH), lambda b, c: (0, 0)),
        ],
        out_specs=[pl.BlockSpec((r, B_WIDTH), lambda b, c: (row(b, c), 0)),
                   pl.BlockSpec((r, C_WIDTH), lambda b, c: (row(b, c), 0))],
        scratch_shapes=[pltpu.VMEM((r + 2 * POOL_HALO, C_WIDTH), F32)],
        compiler_params=_params("arbitrary", "arbitrary"),
        name="sgu_pool",
    )(proj, proj, proj, proj, proj, snw, ws, bs, wp, ps)


def _outproj_kernel(attn_ref, b_ref, c_ref, g0_ref, g1_ref, g2_ref, x_ref, mod_ref,
                    wa_ref, wb_ref, wc_ref, wo_ref, nw_ref, *rest, final):
    if final:
        y_ref, m_scr = rest
    else:
        modn_ref, y_ref, hn_ref, m_scr = rest
    for n in range(D_MODEL // COL_TILE):
        cs = slice(n * COL_TILE, (n + 1) * COL_TILE)
        a = jnp.dot(attn_ref[...], wa_ref[:, cs], preferred_element_type=F32)
        b = jnp.dot(b_ref[...], wb_ref[:, cs], preferred_element_type=F32)
        c = jnp.dot(c_ref[...], wc_ref[:, cs], preferred_element_type=F32)
        m = (g0_ref[:, cs].astype(F32) * a + g1_ref[:, cs].astype(F32) * b
             + g2_ref[:, cs].astype(F32) * c)
        m_scr[:, cs] = m.astype(BF16)
    ssq = jnp.zeros((x_ref.shape[0], 1), F32)
    for n in range(D_MODEL // COL_TILE):
        cs = slice(n * COL_TILE, (n + 1) * COL_TILE)
        out = jnp.dot(m_scr[...], wo_ref[:, cs], preferred_element_type=F32)
        y = x_ref[:, cs] + mod_ref[:, 2 * D_MODEL + n * COL_TILE:2 * D_MODEL + (n + 1) * COL_TILE] * out
        y_ref[:, cs] = y
        ssq = ssq + jnp.sum(y * y, axis=-1, keepdims=True)
    yn = y_ref[...] * lax.rsqrt(ssq * (1.0 / D_MODEL) + EPS) * nw_ref[...]
    if final:
        y_ref[...] = yn
    else:
        hn_ref[...] = (yn * (1.0 + modn_ref[:, D_MODEL:2 * D_MODEL]) + modn_ref[:, 0:D_MODEL]).astype(BF16)


def _outproj_call(attn, bout, cout, gates, x2d, mod3, wa, wb, wc, wo, nw, modn3, *, seq, n_batch_rows):
    t = x2d.shape[0]
    tm = OUT_TOKEN_TILE
    final = modn3 is None
    mod_idx = _mod_index(n_batch_rows, max(seq // tm, 1))
    gate = lambda g: pl.BlockSpec((tm, D_MODEL), lambda i: (i, g))
    in_specs = [
        pl.BlockSpec((tm, A_WIDTH), lambda i: (i, 0)),
        pl.BlockSpec((tm, B_WIDTH), lambda i: (i, 0)),
        pl.BlockSpec((tm, C_WIDTH), lambda i: (i, 0)),
        gate(0), gate(1), gate(2),
        pl.BlockSpec((tm, D_MODEL), lambda i: (i, 0)),
        pl.BlockSpec((None, 1, GATE_COLS), mod_idx),
        _resident((A_WIDTH, D_MODEL)), _resident((B_WIDTH, D_MODEL)), _resident((C_WIDTH, D_MODEL)),
        _resident((D_MODEL, D_MODEL)),
        pl.BlockSpec((1, D_MODEL), lambda i: (0, 0)),
    ]
    args = [attn, bout, cout, gates, gates, gates, x2d, mod3, wa, wb, wc, wo, nw]
    out_shape = [jax.ShapeDtypeStruct((t, D_MODEL), F32)]
    out_specs = [pl.BlockSpec((tm, D_MODEL), lambda i: (i, 0))]
    if not final:
        in_specs.append(pl.BlockSpec((None, 1, GATE_COLS), mod_idx))
        args.append(modn3)
        out_shape.append(jax.ShapeDtypeStruct((t, D_MODEL), BF16))
        out_specs.append(pl.BlockSpec((tm, D_MODEL), lambda i: (i, 0)))
    return pl.pallas_call(
        functools.partial(_outproj_kernel, final=final),
        out_shape=out_shape,
        grid=(t // tm,),
        in_specs=in_specs,
        out_specs=out_specs,
        scratch_shapes=[pltpu.VMEM((tm, D_MODEL), BF16)],
        compiler_params=_params("arbitrary"),
        name="out_proj_final" if final else "out_proj",
    )(*args)


def _rope_tables(n_tokens):
    rows = n_tokens // GRID_W
    row = jnp.repeat(jnp.arange(rows), GRID_W).astype(F32)
    col = jnp.tile(jnp.arange(GRID_W), rows).astype(F32)
    n_freq = HEAD_DIM // 4
    inv = ROPE_THETA ** (-jnp.arange(n_freq, dtype=F32) / n_freq)
    ar = row[:, None] * inv[None, :]
    ac = col[:, None] * inv[None, :]
    zero = jnp.zeros_like(ar)
    cos = jnp.concatenate([jnp.cos(ar), jnp.cos(ar), jnp.cos(ac), jnp.cos(ac)], axis=-1)
    sa = jnp.concatenate([-jnp.sin(ar), zero, -jnp.sin(ac), zero], axis=-1)
    sb = jnp.concatenate([zero, jnp.sin(ar), zero, jnp.sin(ac)], axis=-1)
    return cos, sa, sb


def kernel(x_prompt, x_sample, cache_k, cache_v, c, c_ctx, norm_w, w_ada, b_ada, w_in, q_norm_w,
           k_norm_w, sgu_norm_w, w_sgu, b_sgu, w_pool, pool_scale, w_br_a, w_br_b, w_br_c, w_merge,
           b_merge, w_out, final_norm_w):
    nb_p, seq_p, d = x_prompt.shape
    nb_s, seq_s, _ = x_sample.shape
    past = cache_k.shape[2]
    assert d == D_MODEL and nb_s + 1 <= MOD_ROWS

    cv = jnp.concatenate([c_ctx[None, :], c, jnp.zeros((MOD_ROWS - 1 - nb_s, d), F32)], axis=0)
    mod = _ada_call(cv, w_ada, b_ada.reshape(DEPTH, 1, GATE_COLS))
    mod3 = [mod[l].reshape(MOD_ROWS, 1, GATE_COLS) for l in range(DEPTH)]
    rope_tabs = _rope_tables(seq_s)
    cache = (cache_k.reshape(nb_s, DEPTH, past, KV_WIDTH), cache_v.reshape(nb_s, DEPTH, past, KV_WIDTH))

    groups = [
        dict(x=x_prompt.reshape(nb_p * seq_p, d), nb=nb_p, seq=seq_p, rows=0, rope=None, cache=None),
        dict(x=x_sample.reshape(nb_s * seq_s, d), nb=nb_s, seq=seq_s, rows=nb_s, rope=rope_tabs, cache=cache),
    ]
    for g in groups:
        g["h"] = _normmod_call(g["x"], mod3[0], norm_w[0].reshape(1, d), seq=g["seq"], n_batch_rows=g["rows"])
    states = []
    for l in range(DEPTH):
        last = l == DEPTH - 1
        w_mg = w_merge[l].astype(BF16)
        w_i = w_in[l].astype(BF16)
        wa, wb, wc, wo = (w.astype(BF16) for w in (w_br_a[l], w_br_b[l], w_br_c[l], w_out[l]))
        bm = b_merge[l].reshape(1, GATE_COLS)
        qnw = q_norm_w[l].reshape(1, HEAD_DIM)
        knw = k_norm_w[l].reshape(1, HEAD_DIM)
        mix_w = (sgu_norm_w[l].reshape(1, B_WIDTH), w_sgu[l].astype(BF16),
                 b_sgu[l].reshape(B_GROUPS, CHUNK, 1), w_pool[l].astype(BF16),
                 pool_scale[l].reshape(1, C_WIDTH))
        nw_next = (final_norm_w if last else norm_w[l + 1]).reshape(1, d)
        for g in groups:
            keep = g["cache"] is None
            gates = _gates_call(g["h"], w_mg, bm)
            res = _proj_call(g["h"], w_i, qnw, knw, g["rope"], seq=g["seq"], state=keep)
            proj = res[0]
            if keep:
                states.append(res[1])
            attn = _attn_call(proj, g["cache"], n_batch=g["nb"], seq=g["seq"], layer=l)
            bout, cout = _mix_call(proj, *mix_w, n_batch=g["nb"], seq=g["seq"])
            res = _outproj_call(attn, bout, cout, gates, g["x"], mod3[l], wa, wb, wc, wo, nw_next,
                                None if last else mod3[l + 1], seq=g["seq"], n_batch_rows=g["rows"])
            g["x"] = res[0]
            if not last:
                g["h"] = res[1]

    kv = jnp.stack([s.reshape(nb_p, seq_p, 2 * KV_WIDTH) for s in states], axis=1)
    state_k = kv[..., :KV_WIDTH].reshape(nb_p, DEPTH, seq_p, A_KV_HEADS, HEAD_DIM)
    state_v = kv[..., KV_WIDTH:].reshape(nb_p, DEPTH, seq_p, A_KV_HEADS, HEAD_DIM)
    return (groups[0]["x"].reshape(nb_p, seq_p, d), groups[1]["x"].reshape(nb_s, seq_s, d), state_k, state_v)
```

```python
import functools

import jax
import jax.numpy as jnp
import numpy as np
from jax import lax
from jax.experimental import pallas as pl
from jax.experimental.pallas import tpu as pltpu

F32 = jnp.float32
BF16 = jnp.bfloat16

D_MODEL = 2048
DEPTH = 2
GRID_W = 64
EPS = 1e-6
HEAD_DIM = 128
A_HEADS = 8
A_KV_HEADS = 2
A_WIDTH = A_HEADS * HEAD_DIM
KV_WIDTH = A_KV_HEADS * HEAD_DIM
ROPE_THETA = 10000.0
ATTN_SCALE = HEAD_DIM ** -0.5
LOG2_E = 1.4426950408889634
CHUNK = 128
B_GROUPS = 4
B_WIDTH = 512
C_WIDTH = 512
POOL_WINDOWS = (2, 4, 8, 16)
POOL_HALO = 16
N_BRANCH = 3
GATE_COLS = N_BRANCH * D_MODEL
IN_COLS = 2 * A_WIDTH + 2 * KV_WIDTH + 3 * B_WIDTH + 2 * C_WIDTH

COL_TILE = 512
Q_TILE0 = 0
KV_TILE = Q_TILE0 + A_WIDTH // COL_TILE
GA_TILE0 = KV_TILE + 1
U_TILE = GA_TILE0 + A_WIDTH // COL_TILE
VB_TILE = U_TILE + 1
GB_TILE = VB_TILE + 1
Z_TILE = GB_TILE + 1
GC_TILE = Z_TILE + 1
HEADS_PER_TILE = COL_TILE // HEAD_DIM
K_BLOCK = (KV_TILE * COL_TILE) // HEAD_DIM
V_BLOCK = K_BLOCK + A_KV_HEADS

MXU_COLS = 256
MOD_ROWS = 8
ADA_TILE = 1024
IN_TOKEN_TILE = 512
OUT_TOKEN_TILE = 256
ATTN_Q_TILE = 256
MIX_TILE = 512
VMEM_LIMIT = 56 * 1024 * 1024


def _params(*sem):
    return pltpu.CompilerParams(dimension_semantics=sem, vmem_limit_bytes=VMEM_LIMIT)


def _resident(shape):
    return pl.BlockSpec(shape, lambda *_: (0,) * len(shape), pipeline_mode=pl.Buffered(1))


def _silu(x):
    return x * jax.nn.sigmoid(x)


def _rms(x, w):
    ms = jnp.mean(x * x, axis=-1, keepdims=True)
    return x * lax.rsqrt(ms + EPS) * w


def _mod_index(n_batch_rows, tiles_per_seq):
    def idx(i):
        if n_batch_rows == 0:
            return (0, 0, 0)
        return (1 + i // tiles_per_seq, 0, 0)
    return idx


def _ada_kernel(cv_ref, w_ref, b_ref, o_ref):
    a = _silu(cv_ref[...]).astype(BF16)
    o_ref[...] = jnp.dot(a, w_ref[...].astype(BF16), preferred_element_type=F32) + b_ref[...]


def _ada_call(cv, w_ada, b_ada):
    return pl.pallas_call(
        _ada_kernel,
        out_shape=jax.ShapeDtypeStruct((DEPTH, MOD_ROWS, GATE_COLS), F32),
        grid=(DEPTH, GATE_COLS // ADA_TILE),
        in_specs=[
            pl.BlockSpec((MOD_ROWS, D_MODEL), lambda l, j: (0, 0)),
            pl.BlockSpec((None, D_MODEL, ADA_TILE), lambda l, j: (l, 0, j)),
            pl.BlockSpec((None, 1, ADA_TILE), lambda l, j: (l, 0, j)),
        ],
        out_specs=pl.BlockSpec((None, MOD_ROWS, ADA_TILE), lambda l, j: (l, 0, j)),
        compiler_params=_params("arbitrary", "arbitrary"),
        name="ada_mod",
    )(cv, w_ada, b_ada)


def _normmod_kernel(x_ref, mod_ref, nw_ref, h_ref):
    shift = mod_ref[:, 0:D_MODEL]
    scale = mod_ref[:, D_MODEL:2 * D_MODEL]
    h_ref[...] = (_rms(x_ref[...], nw_ref[...]) * (1.0 + scale) + shift).astype(BF16)


def _normmod_call(x2d, mod3, norm_w, *, seq, n_batch_rows):
    t = x2d.shape[0]
    tm = IN_TOKEN_TILE
    return pl.pallas_call(
        _normmod_kernel,
        out_shape=jax.ShapeDtypeStruct((t, D_MODEL), BF16),
        grid=(t // tm,),
        in_specs=[
            pl.BlockSpec((tm, D_MODEL), lambda i: (i, 0)),
            pl.BlockSpec((None, 1, GATE_COLS), _mod_index(n_batch_rows, max(seq // tm, 1))),
            pl.BlockSpec((1, D_MODEL), lambda i: (0, 0)),
        ],
        out_specs=pl.BlockSpec((tm, D_MODEL), lambda i: (i, 0)),
        compiler_params=_params("arbitrary"),
        name="norm_mod",
    )(x2d, mod3, norm_w)


def _gates_kernel(h_ref, w_ref, b_ref, o_ref):
    for s in range(GATE_COLS // MXU_COLS):
        cs = slice(s * MXU_COLS, (s + 1) * MXU_COLS)
        acc = jnp.dot(h_ref[...], w_ref[:, cs], preferred_element_type=F32)
        o_ref[:, cs] = jax.nn.sigmoid(acc + b_ref[:, cs]).astype(BF16)


def _gates_call(h, w_merge, b_merge):
    t = h.shape[0]
    tm = IN_TOKEN_TILE
    return pl.pallas_call(
        _gates_kernel,
        out_shape=jax.ShapeDtypeStruct((t, GATE_COLS), BF16),
        grid=(t // tm,),
        in_specs=[
            pl.BlockSpec((tm, D_MODEL), lambda i: (i, 0)),
            _resident((D_MODEL, GATE_COLS)),
            _resident((1, GATE_COLS)),
        ],
        out_specs=pl.BlockSpec((tm, GATE_COLS), lambda i: (i, 0)),
        compiler_params=_params("arbitrary"),
        name="gates",
    )(h, w_merge, b_merge)


def _rope(y, cos, sa, sb):
    return y * cos + pltpu.roll(y, 96, 1) * sa + pltpu.roll(y, 32, 1) * sb


def _proj_kernel(*refs, rope, state):
    h_ref, w_ref, qnw_ref, knw_ref = refs[:4]
    refs = refs[4:]
    if rope:
        cos_ref, sa_ref, sb_ref = refs[:3]
        refs = refs[3:]
    proj_ref = refs[0]
    if state:
        ks_ref, vs_ref = refs[1:]

    def head(xh, w):
        y = _rms(xh, w)
        if rope:
            y = _rope(y, cos_ref[...], sa_ref[...], sb_ref[...])
        return y

    qw = qnw_ref[...] * (ATTN_SCALE * LOG2_E)
    k0 = KV_TILE * COL_TILE
    v0 = k0 + KV_WIDTH
    silu_cols = ((GA_TILE0 * COL_TILE, U_TILE * COL_TILE), (GB_TILE * COL_TILE, Z_TILE * COL_TILE),
                 (GC_TILE * COL_TILE, IN_COLS))
    for s in range(IN_COLS // MXU_COLS):
        c0 = s * MXU_COLS
        cs = slice(c0, c0 + MXU_COLS)
        acc = jnp.dot(h_ref[...], w_ref[:, cs], preferred_element_type=F32)
        if c0 < k0:
            for hh in range(MXU_COLS // HEAD_DIM):
                hs = slice(hh * HEAD_DIM, (hh + 1) * HEAD_DIM)
                proj_ref[:, c0 + hh * HEAD_DIM:c0 + (hh + 1) * HEAD_DIM] = head(acc[:, hs], qw).astype(BF16)
        elif c0 < v0:
            for hh in range(MXU_COLS // HEAD_DIM):
                hs = slice(hh * HEAD_DIM, (hh + 1) * HEAD_DIM)
                os = slice(c0 + hh * HEAD_DIM, c0 + (hh + 1) * HEAD_DIM)
                if state:
                    ks_ref[:, hs] = _rms(acc[:, hs], knw_ref[...])
                proj_ref[:, os] = head(acc[:, hs], knw_ref[...]).astype(BF16)
        elif c0 < v0 + KV_WIDTH:
            if state:
                vs_ref[...] = acc
            proj_ref[:, cs] = acc.astype(BF16)
        elif any(lo <= c0 < hi for lo, hi in silu_cols):
            proj_ref[:, cs] = _silu(acc).astype(BF16)
        else:
            proj_ref[:, cs] = acc.astype(BF16)


def _proj_call(h, w_in, qnw, knw, rope_tabs, *, seq, state):
    t = h.shape[0]
    tm = IN_TOKEN_TILE
    rope = rope_tabs is not None
    tiles_per_seq = max(seq // tm, 1)
    in_specs = [
        pl.BlockSpec((tm, D_MODEL), lambda i: (i, 0)),
        _resident((D_MODEL, IN_COLS)),
        pl.BlockSpec((1, HEAD_DIM), lambda i: (0, 0)),
        pl.BlockSpec((1, HEAD_DIM), lambda i: (0, 0)),
    ]
    args = [h, w_in, qnw, knw]
    if rope:
        for tab in rope_tabs:
            in_specs.append(pl.BlockSpec((tm, HEAD_DIM), lambda i: (i % tiles_per_seq, 0)))
            args.append(tab)
    out_shape = [jax.ShapeDtypeStruct((t, IN_COLS), BF16)]
    out_specs = [pl.BlockSpec((tm, IN_COLS), lambda i: (i, 0))]
    if state:
        for _ in range(2):
            out_shape.append(jax.ShapeDtypeStruct((t, KV_WIDTH), F32))
            out_specs.append(pl.BlockSpec((tm, KV_WIDTH), lambda i: (i, 0)))
    return pl.pallas_call(
        functools.partial(_proj_kernel, rope=rope, state=state),
        out_shape=out_shape,
        grid=(t // tm,),
        in_specs=in_specs,
        out_specs=out_specs,
        compiler_params=_params("arbitrary"),
        name="proj_lat" if rope else "proj_ctx",
    )(*args)


def _attn_kernel(*refs, ctx):
    q_ref, k_ref, v_ref, ga0_ref, ga1_ref = refs[:5]
    ga_refs = (ga0_ref, ga1_ref)
    refs = refs[5:]
    if ctx:
        ck_ref, cv_ref = refs[:2]
        refs = refs[2:]
    o_ref, vx_scr = refs[:2]
    if ctx:
        ckx_scr, cvx_scr = refs[2:]

    @pl.when(pl.program_id(1) == 0)
    def _():
        for kh in range(A_KV_HEADS):
            hs = slice(kh * HEAD_DIM, (kh + 1) * HEAD_DIM)
            vx_scr[kh, :, :HEAD_DIM] = v_ref[:, hs]
            vx_scr[kh, :, HEAD_DIM:] = jnp.ones((vx_scr.shape[1], HEAD_DIM), BF16)
            if ctx:
                ckx_scr[kh] = ck_ref[:, kh, :].astype(BF16)
                cvx_scr[kh, :, :HEAD_DIM] = cv_ref[:, kh, :].astype(BF16)
                cvx_scr[kh, :, HEAD_DIM:] = jnp.ones((cvx_scr.shape[1], HEAD_DIM), BF16)

    nt = (((1,), (1,)), ((), ()))
    for head in range(A_HEADS):
        kh, hh = divmod(head, HEADS_PER_TILE)
        cs = slice(head * HEAD_DIM, (head + 1) * HEAD_DIM)
        q = q_ref[:, cs]
        s1 = lax.dot_general(q, k_ref[:, kh * HEAD_DIM:(kh + 1) * HEAD_DIM], nt, preferred_element_type=F32)
        m = jnp.max(s1, axis=-1, keepdims=True)
        if ctx:
            s2 = lax.dot_general(q, ckx_scr[kh], nt, preferred_element_type=F32)
            m = jnp.maximum(m, jnp.max(s2, axis=-1, keepdims=True))
        ox = jnp.dot(jnp.exp2(s1 - m).astype(BF16), vx_scr[kh], preferred_element_type=F32)
        if ctx:
            ox = ox + jnp.dot(jnp.exp2(s2 - m).astype(BF16), cvx_scr[kh], preferred_element_type=F32)
        o = ox[:, :HEAD_DIM] / ox[:, HEAD_DIM:]
        ga = ga_refs[kh][:, hh * HEAD_DIM:(hh + 1) * HEAD_DIM]
        o_ref[:, cs] = (o * ga.astype(F32)).astype(BF16)


def _attn_call(proj, cache, *, n_batch, seq, layer):
    t = proj.shape[0]
    tq = min(ATTN_Q_TILE, seq)
    nq = seq // tq
    ctx = cache is not None
    in_specs = [
        pl.BlockSpec((tq, A_WIDTH), lambda b, qi: (b * nq + qi, Q_TILE0 * COL_TILE // A_WIDTH)),
        pl.BlockSpec((seq, KV_WIDTH), lambda b, qi: (b, K_BLOCK * HEAD_DIM // KV_WIDTH)),
        pl.BlockSpec((seq, KV_WIDTH), lambda b, qi: (b, V_BLOCK * HEAD_DIM // KV_WIDTH)),
        pl.BlockSpec((tq, COL_TILE), lambda b, qi: (b * nq + qi, GA_TILE0)),
        pl.BlockSpec((tq, COL_TILE), lambda b, qi: (b * nq + qi, GA_TILE0 + 1)),
    ]
    args = [proj, proj, proj, proj, proj]
    scratch = [pltpu.VMEM((A_KV_HEADS, seq, 2 * HEAD_DIM), BF16)]
    if ctx:
        past = cache[0].shape[2]
        for c in cache:
            in_specs.append(pl.BlockSpec((None, None, past, A_KV_HEADS, HEAD_DIM),
                                         lambda b, qi: (b, layer, 0, 0, 0)))
            args.append(c)
        scratch += [pltpu.VMEM((A_KV_HEADS, past, HEAD_DIM), BF16),
                    pltpu.VMEM((A_KV_HEADS, past, 2 * HEAD_DIM), BF16)]
    return pl.pallas_call(
        functools.partial(_attn_kernel, ctx=ctx),
        out_shape=jax.ShapeDtypeStruct((t, A_WIDTH), BF16),
        grid=(n_batch, nq),
        in_specs=in_specs,
        out_specs=pl.BlockSpec((tq, A_WIDTH), lambda b, qi: (b * nq + qi, 0)),
        scratch_shapes=scratch,
        compiler_params=_params("arbitrary", "arbitrary"),
        name="attention_lat" if ctx else "attention_ctx",
    )(*args)


def _mix_kernel(u_ref, vb_ref, gb_ref, z_ref, gc_ref, snw_ref, ws_ref, bs_ref, wp_ref, ps_ref,
                bo_ref, co_ref, zp_scr, *, seq):
    r = u_ref.shape[0]
    c = pl.program_id(1)
    nc = pl.num_programs(1)
    base = pl.multiple_of(c * r, r)

    for cc in range(r // CHUNK):
        rs = slice(cc * CHUNK, (cc + 1) * CHUNK)
        vbn = _rms(vb_ref[rs, :].astype(F32), snw_ref[...]).astype(BF16)
        for g in range(B_GROUPS):
            cs = slice(g * HEAD_DIM, (g + 1) * HEAD_DIM)
            mixed = jnp.dot(ws_ref[g], vbn[:, cs], preferred_element_type=F32) + bs_ref[g]
            bo_ref[rs, cs] = (u_ref[rs, cs].astype(F32) * mixed * gb_ref[rs, cs].astype(F32)).astype(BF16)

    zp_scr[POOL_HALO:POOL_HALO + r, :] = z_ref[pl.ds(base, r), :].astype(F32)

    @pl.when(c == 0)
    def _():
        zp_scr[0:POOL_HALO, :] = jnp.zeros((POOL_HALO, C_WIDTH), F32)

    @pl.when(c > 0)
    def _():
        zp_scr[0:POOL_HALO, :] = z_ref[pl.ds(pl.multiple_of(base - POOL_HALO, POOL_HALO), POOL_HALO), :].astype(F32)

    @pl.when(c == nc - 1)
    def _():
        zp_scr[POOL_HALO + r:, :] = jnp.zeros((POOL_HALO, C_WIDTH), F32)

    @pl.when(c < nc - 1)
    def _():
        zp_scr[POOL_HALO + r:, :] = z_ref[pl.ds(pl.multiple_of(base + r, POOL_HALO), POOL_HALO), :].astype(F32)

    t = (base + lax.broadcasted_iota(jnp.int32, (r, HEAD_DIM), 0))
    for g, w in enumerate(POOL_WINDOWS):
        cs = slice(g * HEAD_DIM, (g + 1) * HEAD_DIM)
        half = w // 2
        acc = zp_scr[POOL_HALO - half:POOL_HALO - half + r, cs]
        for o in range(-half + 1, half):
            acc = acc + zp_scr[POOL_HALO + o:POOL_HALO + o + r, cs]
        cnt = (jnp.minimum(t + half, seq) - jnp.maximum(t - half, 0)).astype(F32)
        d = acc / cnt - zp_scr[POOL_HALO:POOL_HALO + r, cs]
        dm = jnp.dot(d.astype(BF16), wp_ref[g], preferred_element_type=F32)
        co_ref[:, cs] = (dm * ps_ref[:, cs] * gc_ref[:, cs].astype(F32)).astype(BF16)


def _mix_call(proj, snw, ws, bs, wp, ps, *, n_batch, seq):
    t = proj.shape[0]
    r = min(MIX_TILE, seq)
    nc = seq // r
    row = lambda b, c: b * nc + c
    tile = lambda col: pl.BlockSpec((r, COL_TILE), lambda b, c: (row(b, c), col))
    const3 = lambda shape: pl.BlockSpec(shape, lambda b, c: (0, 0, 0))
    return pl.pallas_call(
        functools.partial(_mix_kernel, seq=seq),
        out_shape=[jax.ShapeDtypeStruct((t, B_WIDTH), BF16), jax.ShapeDtypeStruct((t, C_WIDTH), BF16)],
        grid=(n_batch, nc),
        in_specs=[
            tile(U_TILE), tile(VB_TILE), tile(GB_TILE),
            pl.BlockSpec((seq, COL_TILE), lambda b, c: (b, Z_TILE)),
            tile(GC_TILE),
            pl.BlockSpec((1, B_WIDTH), lambda b, c: (0, 0)),
            const3((B_GROUPS, CHUNK, CHUNK)),
            const3((B_GROUPS, CHUNK, 1)),
            const3((B_GROUPS, HEAD_DIM, HEAD_DIM)),
            pl.BlockSpec((1, C_WIDTH), lambda b, c: (0, 0)),
        ],
        out_specs=[pl.BlockSpec((r, B_WIDTH), lambda b, c: (row(b, c), 0)),
                   pl.BlockSpec((r, C_WIDTH), lambda b, c: (row(b, c), 0))],
        scratch_shapes=[pltpu.VMEM((r + 2 * POOL_HALO, C_WIDTH), F32)],
        compiler_params=_params("arbitrary", "arbitrary"),
        name="sgu_pool",
    )(proj, proj, proj, proj, proj, snw, ws, bs, wp, ps)


def _outproj_kernel(attn_ref, b_ref, c_ref, g0_ref, g1_ref, g2_ref, x_ref, mod_ref,
                    wa_ref, wb_ref, wc_ref, wo_ref, nw_ref, *rest, final):
    if final:
        y_ref, m_scr = rest
    else:
        modn_ref, y_ref, hn_ref, m_scr = rest
    for n in range(D_MODEL // COL_TILE):
        cs = slice(n * COL_TILE, (n + 1) * COL_TILE)
        a = jnp.dot(attn_ref[...], wa_ref[:, cs], preferred_element_type=F32)
        b = jnp.dot(b_ref[...], wb_ref[:, cs], preferred_element_type=F32)
        c = jnp.dot(c_ref[...], wc_ref[:, cs], preferred_element_type=F32)
        m = (g0_ref[:, cs].astype(F32) * a + g1_ref[:, cs].astype(F32) * b
             + g2_ref[:, cs].astype(F32) * c)
        m_scr[:, cs] = m.astype(BF16)
    ssq = jnp.zeros((x_ref.shape[0], 1), F32)
    for n in range(D_MODEL // COL_TILE):
        cs = slice(n * COL_TILE, (n + 1) * COL_TILE)
        out = jnp.dot(m_scr[...], wo_ref[:, cs], preferred_element_type=F32)
        y = x_ref[:, cs] + mod_ref[:, 2 * D_MODEL + n * COL_TILE:2 * D_MODEL + (n + 1) * COL_TILE] * out
        y_ref[:, cs] = y
        ssq = ssq + jnp.sum(y * y, axis=-1, keepdims=True)
    yn = y_ref[...] * lax.rsqrt(ssq * (1.0 / D_MODEL) + EPS) * nw_ref[...]
    if final:
        y_ref[...] = yn
    else:
        hn_ref[...] = (yn * (1.0 + modn_ref[:, D_MODEL:2 * D_MODEL]) + modn_ref[:, 0:D_MODEL]).astype(BF16)


def _outproj_call(attn, bout, cout, gates, x2d, mod3, wa, wb, wc, wo, nw, modn3, *, seq, n_batch_rows):
    t = x2d.shape[0]
    tm = OUT_TOKEN_TILE
    final = modn3 is None
    mod_idx = _mod_index(n_batch_rows, max(seq // tm, 1))
    gate = lambda g: pl.BlockSpec((tm, D_MODEL), lambda i: (i, g))
    in_specs = [
        pl.BlockSpec((tm, A_WIDTH), lambda i: (i, 0)),
        pl.BlockSpec((tm, B_WIDTH), lambda i: (i, 0)),
        pl.BlockSpec((tm, C_WIDTH), lambda i: (i, 0)),
        gate(0), gate(1), gate(2),
        pl.BlockSpec((tm, D_MODEL), lambda i: (i, 0)),
        pl.BlockSpec((None, 1, GATE_COLS), mod_idx),
        _resident((A_WIDTH, D_MODEL)), _resident((B_WIDTH, D_MODEL)), _resident((C_WIDTH, D_MODEL)),
        _resident((D_MODEL, D_MODEL)),
        pl.BlockSpec((1, D_MODEL), lambda i: (0, 0)),
    ]
    args = [attn, bout, cout, gates, gates, gates, x2d, mod3, wa, wb, wc, wo, nw]
    out_shape = [jax.ShapeDtypeStruct((t, D_MODEL), F32)]
    out_specs = [pl.BlockSpec((tm, D_MODEL), lambda i: (i, 0))]
    if not final:
        in_specs.append(pl.BlockSpec((None, 1, GATE_COLS), mod_idx))
        args.append(modn3)
        out_shape.append(jax.ShapeDtypeStruct((t, D_MODEL), BF16))
        out_specs.append(pl.BlockSpec((tm, D_MODEL), lambda i: (i, 0)))
    return pl.pallas_call(
        functools.partial(_outproj_kernel, final=final),
        out_shape=out_shape,
        grid=(t // tm,),
        in_specs=in_specs,
        out_specs=out_specs,
        scratch_shapes=[pltpu.VMEM((tm, D_MODEL), BF16)],
        compiler_params=_params("arbitrary"),
        name="out_proj_final" if final else "out_proj",
    )(*args)


def _rope_tables(n_tokens):
    rows = n_tokens // GRID_W
    row = np.repeat(np.arange(rows), GRID_W).astype(np.float64)
    col = np.tile(np.arange(GRID_W), rows).astype(np.float64)
    n_freq = HEAD_DIM // 4
    inv = ROPE_THETA ** (-np.arange(n_freq, dtype=np.float64) / n_freq)
    ar = row[:, None] * inv[None, :]
    ac = col[:, None] * inv[None, :]
    zero = np.zeros_like(ar)
    cos = np.concatenate([np.cos(ar), np.cos(ar), np.cos(ac), np.cos(ac)], axis=-1)
    sa = np.concatenate([-np.sin(ar), zero, -np.sin(ac), zero], axis=-1)
    sb = np.concatenate([zero, np.sin(ar), zero, np.sin(ac)], axis=-1)
    return tuple(jnp.asarray(t, dtype=F32) for t in (cos, sa, sb))


def kernel(x_prompt, x_sample, cache_k, cache_v, c, c_ctx, norm_w, w_ada, b_ada, w_in, q_norm_w,
           k_norm_w, sgu_norm_w, w_sgu, b_sgu, w_pool, pool_scale, w_br_a, w_br_b, w_br_c, w_merge,
           b_merge, w_out, final_norm_w):
    nb_p, seq_p, d = x_prompt.shape
    nb_s, seq_s, _ = x_sample.shape
    past = cache_k.shape[2]
    assert d == D_MODEL and nb_s + 1 <= MOD_ROWS

    cv = jnp.concatenate([c_ctx[None, :], c, jnp.zeros((MOD_ROWS - 1 - nb_s, d), F32)], axis=0)
    mod = _ada_call(cv, w_ada, b_ada.reshape(DEPTH, 1, GATE_COLS))
    mod3 = [mod[l].reshape(MOD_ROWS, 1, GATE_COLS) for l in range(DEPTH)]
    rope_tabs = _rope_tables(seq_s)
    cache = (cache_k, cache_v)

    groups = [
        dict(x=x_prompt.reshape(nb_p * seq_p, d), nb=nb_p, seq=seq_p, rows=0, rope=None, cache=None),
        dict(x=x_sample.reshape(nb_s * seq_s, d), nb=nb_s, seq=seq_s, rows=nb_s, rope=rope_tabs, cache=cache),
    ]
    for g in groups:
        g["h"] = _normmod_call(g["x"], mod3[0], norm_w[0].reshape(1, d), seq=g["seq"], n_batch_rows=g["rows"])
    states = []
    for l in range(DEPTH):
        last = l == DEPTH - 1
        w_mg = w_merge[l].astype(BF16)
        w_i = w_in[l].astype(BF16)
        wa, wb, wc, wo = (w.astype(BF16) for w in (w_br_a[l], w_br_b[l], w_br_c[l], w_out[l]))
        bm = b_merge[l].reshape(1, GATE_COLS)
        qnw = q_norm_w[l].reshape(1, HEAD_DIM)
        knw = k_norm_w[l].reshape(1, HEAD_DIM)
        mix_w = (sgu_norm_w[l].reshape(1, B_WIDTH), w_sgu[l].astype(BF16),
                 b_sgu[l].reshape(B_GROUPS, CHUNK, 1), w_pool[l].astype(BF16),
                 pool_scale[l].reshape(1, C_WIDTH))
        nw_next = (final_norm_w if last else norm_w[l + 1]).reshape(1, d)
        for g in groups:
            keep = g["cache"] is None
            gates = _gates_call(g["h"], w_mg, bm)
            res = _proj_call(g["h"], w_i, qnw, knw, g["rope"], seq=g["seq"], state=keep)
            proj = res[0]
            if keep:
                states.append(res[1:])
            attn = _attn_call(proj, g["cache"], n_batch=g["nb"], seq=g["seq"], layer=l)
            bout, cout = _mix_call(proj, *mix_w, n_batch=g["nb"], seq=g["seq"])
            res = _outproj_call(attn, bout, cout, gates, g["x"], mod3[l], wa, wb, wc, wo, nw_next,
                                None if last else mod3[l + 1], seq=g["seq"], n_batch_rows=g["rows"])
            g["x"] = res[0]
            if not last:
                g["h"] = res[1]

    state_k, state_v = (
        jnp.stack([s[i].reshape(nb_p, seq_p, A_KV_HEADS, HEAD_DIM) for s in states], axis=1) for i in range(2))
    return (groups[0]["x"].reshape(nb_p, seq_p, d), groups[1]["x"].reshape(nb_s, seq_s, d), state_k, state_v)
```

```python
import functools

import jax
import jax.numpy as jnp
import numpy as np
from jax import lax
from jax.experimental import pallas as pl
from jax.experimental.pallas import tpu as pltpu

F32 = jnp.float32
BF16 = jnp.bfloat16

D_MODEL = 2048
DEPTH = 2
GRID_W = 64
EPS = 1e-6
HEAD_DIM = 128
A_HEADS = 8
A_KV_HEADS = 2
A_WIDTH = A_HEADS * HEAD_DIM
KV_WIDTH = A_KV_HEADS * HEAD_DIM
ROPE_THETA = 10000.0
ATTN_SCALE = HEAD_DIM ** -0.5
LOG2_E = 1.4426950408889634
CHUNK = 128
B_GROUPS = 4
B_WIDTH = 512
C_WIDTH = 512
POOL_WINDOWS = (2, 4, 8, 16)
POOL_HALO = 16
N_BRANCH = 3
GATE_COLS = N_BRANCH * D_MODEL
IN_COLS = 2 * A_WIDTH + 2 * KV_WIDTH + 3 * B_WIDTH + 2 * C_WIDTH

COL_TILE = 512
Q_TILE0 = 0
KV_TILE = Q_TILE0 + A_WIDTH // COL_TILE
GA_TILE0 = KV_TILE + 1
U_TILE = GA_TILE0 + A_WIDTH // COL_TILE
VB_TILE = U_TILE + 1
GB_TILE = VB_TILE + 1
Z_TILE = GB_TILE + 1
GC_TILE = Z_TILE + 1
HEADS_PER_TILE = COL_TILE // HEAD_DIM
K_BLOCK = (KV_TILE * COL_TILE) // HEAD_DIM
V_BLOCK = K_BLOCK + A_KV_HEADS

MXU_COLS = 256
MOD_ROWS = 8
ADA_TILE = 1024
IN_TOKEN_TILE = 512
OUT_TOKEN_TILE = 256
ATTN_Q_TILE = 512
MIX_TILE = 512
VMEM_LIMIT = 56 * 1024 * 1024


def _params(*sem):
    return pltpu.CompilerParams(dimension_semantics=sem, vmem_limit_bytes=VMEM_LIMIT)


def _resident(shape):
    return pl.BlockSpec(shape, lambda *_: (0,) * len(shape), pipeline_mode=pl.Buffered(1))


def _layer_resident(shape, layer):
    return pl.BlockSpec((None,) + shape, lambda *_: (layer,) + (0,) * len(shape),
                        pipeline_mode=pl.Buffered(1))


def _silu(x):
    return x * jax.nn.sigmoid(x)


def _rms(x, w):
    ms = jnp.mean(x * x, axis=-1, keepdims=True)
    return x * lax.rsqrt(ms + EPS) * w


def _mod_index(n_batch_rows, tiles_per_seq):
    def idx(i):
        if n_batch_rows == 0:
            return (0, 0, 0)
        return (1 + i // tiles_per_seq, 0, 0)
    return idx


def _ada_kernel(cv_ref, w_ref, b_ref, o_ref):
    a = _silu(cv_ref[...]).astype(BF16)
    o_ref[...] = jnp.dot(a, w_ref[...].astype(BF16), preferred_element_type=F32) + b_ref[...]


def _ada_call(cv, w_ada, b_ada):
    return pl.pallas_call(
        _ada_kernel,
        out_shape=jax.ShapeDtypeStruct((DEPTH, MOD_ROWS, GATE_COLS), F32),
        grid=(DEPTH, GATE_COLS // ADA_TILE),
        in_specs=[
            pl.BlockSpec((MOD_ROWS, D_MODEL), lambda l, j: (0, 0)),
            pl.BlockSpec((None, D_MODEL, ADA_TILE), lambda l, j: (l, 0, j)),
            pl.BlockSpec((None, 1, ADA_TILE), lambda l, j: (l, 0, j)),
        ],
        out_specs=pl.BlockSpec((None, MOD_ROWS, ADA_TILE), lambda l, j: (l, 0, j)),
        compiler_params=_params("arbitrary", "arbitrary"),
        name="ada_mod",
    )(cv, w_ada, b_ada)


def _normmod_kernel(x_ref, mod_ref, nw_ref, h_ref):
    shift = mod_ref[:, 0:D_MODEL]
    scale = mod_ref[:, D_MODEL:2 * D_MODEL]
    h_ref[...] = (_rms(x_ref[...], nw_ref[...]) * (1.0 + scale) + shift).astype(BF16)


def _normmod_call(x2d, mod3, norm_w, *, seq, n_batch_rows):
    t = x2d.shape[0]
    tm = IN_TOKEN_TILE
    return pl.pallas_call(
        _normmod_kernel,
        out_shape=jax.ShapeDtypeStruct((t, D_MODEL), BF16),
        grid=(t // tm,),
        in_specs=[
            pl.BlockSpec((tm, D_MODEL), lambda i: (i, 0)),
            pl.BlockSpec((None, 1, GATE_COLS), _mod_index(n_batch_rows, max(seq // tm, 1))),
            pl.BlockSpec((1, D_MODEL), lambda i: (0, 0)),
        ],
        out_specs=pl.BlockSpec((tm, D_MODEL), lambda i: (i, 0)),
        compiler_params=_params("arbitrary"),
        name="norm_mod",
    )(x2d, mod3, norm_w)


def _gates_kernel(h_ref, w_ref, b_ref, o_ref):
    for s in range(GATE_COLS // MXU_COLS):
        cs = slice(s * MXU_COLS, (s + 1) * MXU_COLS)
        acc = jnp.dot(h_ref[...], w_ref[:, cs], preferred_element_type=F32)
        o_ref[:, cs] = jax.nn.sigmoid(acc + b_ref[:, cs]).astype(BF16)


def _gates_call(h, w_merge, b_merge, *, layer):
    t = h.shape[0]
    tm = IN_TOKEN_TILE
    return pl.pallas_call(
        _gates_kernel,
        out_shape=jax.ShapeDtypeStruct((t, GATE_COLS), BF16),
        grid=(t // tm,),
        in_specs=[
            pl.BlockSpec((tm, D_MODEL), lambda i: (i, 0)),
            _layer_resident((D_MODEL, GATE_COLS), layer),
            _resident((1, GATE_COLS)),
        ],
        out_specs=pl.BlockSpec((tm, GATE_COLS), lambda i: (i, 0)),
        compiler_params=_params("arbitrary"),
        name="gates",
    )(h, w_merge, b_merge)


def _rope(y, cos, sa, sb):
    return y * cos + pltpu.roll(y, 96, 1) * sa + pltpu.roll(y, 32, 1) * sb


def _proj_kernel(*refs, rope, state):
    h_ref, w_ref, qnw_ref, knw_ref = refs[:4]
    refs = refs[4:]
    if rope:
        cos_ref, sa_ref, sb_ref = refs[:3]
        refs = refs[3:]
    proj_ref = refs[0]
    if state:
        ks_ref, vs_ref = refs[1:]

    def head(xh, w):
        y = _rms(xh, w)
        if rope:
            y = _rope(y, cos_ref[...], sa_ref[...], sb_ref[...])
        return y

    qw = qnw_ref[...] * (ATTN_SCALE * LOG2_E)
    k0 = KV_TILE * COL_TILE
    v0 = k0 + KV_WIDTH
    silu_cols = ((GA_TILE0 * COL_TILE, U_TILE * COL_TILE), (GB_TILE * COL_TILE, Z_TILE * COL_TILE),
                 (GC_TILE * COL_TILE, IN_COLS))
    for s in range(IN_COLS // MXU_COLS):
        c0 = s * MXU_COLS
        cs = slice(c0, c0 + MXU_COLS)
        acc = jnp.dot(h_ref[...], w_ref[:, cs], preferred_element_type=F32)
        if c0 < k0:
            for hh in range(MXU_COLS // HEAD_DIM):
                hs = slice(hh * HEAD_DIM, (hh + 1) * HEAD_DIM)
                proj_ref[:, c0 + hh * HEAD_DIM:c0 + (hh + 1) * HEAD_DIM] = head(acc[:, hs], qw).astype(BF16)
        elif c0 < v0:
            for hh in range(MXU_COLS // HEAD_DIM):
                hs = slice(hh * HEAD_DIM, (hh + 1) * HEAD_DIM)
                os = slice(c0 + hh * HEAD_DIM, c0 + (hh + 1) * HEAD_DIM)
                if state:
                    ks_ref[:, hs] = _rms(acc[:, hs], knw_ref[...])
                proj_ref[:, os] = head(acc[:, hs], knw_ref[...]).astype(BF16)
        elif c0 < v0 + KV_WIDTH:
            if state:
                vs_ref[...] = acc
            proj_ref[:, cs] = acc.astype(BF16)
        elif any(lo <= c0 < hi for lo, hi in silu_cols):
            proj_ref[:, cs] = _silu(acc).astype(BF16)
        else:
            proj_ref[:, cs] = acc.astype(BF16)


def _proj_call(h, w_in, qnw, knw, rope_tabs, *, seq, state, layer):
    t = h.shape[0]
    tm = IN_TOKEN_TILE
    rope = rope_tabs is not None
    tiles_per_seq = max(seq // tm, 1)
    in_specs = [
        pl.BlockSpec((tm, D_MODEL), lambda i: (i, 0)),
        _layer_resident((D_MODEL, IN_COLS), layer),
        pl.BlockSpec((1, HEAD_DIM), lambda i: (0, 0)),
        pl.BlockSpec((1, HEAD_DIM), lambda i: (0, 0)),
    ]
    args = [h, w_in, qnw, knw]
    if rope:
        for tab in rope_tabs:
            in_specs.append(pl.BlockSpec((tm, HEAD_DIM), lambda i: (i % tiles_per_seq, 0)))
            args.append(tab)
    out_shape = [jax.ShapeDtypeStruct((t, IN_COLS), BF16)]
    out_specs = [pl.BlockSpec((tm, IN_COLS), lambda i: (i, 0))]
    if state:
        for _ in range(2):
            out_shape.append(jax.ShapeDtypeStruct((t, KV_WIDTH), F32))
            out_specs.append(pl.BlockSpec((tm, KV_WIDTH), lambda i: (i, 0)))
    return pl.pallas_call(
        functools.partial(_proj_kernel, rope=rope, state=state),
        out_shape=out_shape,
        grid=(t // tm,),
        in_specs=in_specs,
        out_specs=out_specs,
        compiler_params=_params("arbitrary"),
        name="proj_lat" if rope else "proj_ctx",
    )(*args)


def _attn_kernel(*refs, ctx):
    q_ref, k_ref, v_ref, ga0_ref, ga1_ref = refs[:5]
    ga_refs = (ga0_ref, ga1_ref)
    refs = refs[5:]
    if ctx:
        ck_ref, cv_ref = refs[:2]
        refs = refs[2:]
    o_ref, vx_scr = refs[:2]
    if ctx:
        ckx_scr, cvx_scr = refs[2:]

    @pl.when(pl.program_id(1) == 0)
    def _():
        for kh in range(A_KV_HEADS):
            hs = slice(kh * HEAD_DIM, (kh + 1) * HEAD_DIM)
            vx_scr[kh, :, :HEAD_DIM] = v_ref[:, hs]
            vx_scr[kh, :, HEAD_DIM:] = jnp.ones((vx_scr.shape[1], HEAD_DIM), BF16)
            if ctx:
                ckx_scr[kh] = ck_ref[:, kh, :].astype(BF16)
                cvx_scr[kh, :, :HEAD_DIM] = cv_ref[:, kh, :].astype(BF16)
                cvx_scr[kh, :, HEAD_DIM:] = jnp.ones((cvx_scr.shape[1], HEAD_DIM), BF16)

    nt = (((1,), (1,)), ((), ()))
    for head in range(A_HEADS):
        kh, hh = divmod(head, HEADS_PER_TILE)
        cs = slice(head * HEAD_DIM, (head + 1) * HEAD_DIM)
        q = q_ref[:, cs]
        s1 = lax.dot_general(q, k_ref[:, kh * HEAD_DIM:(kh + 1) * HEAD_DIM], nt, preferred_element_type=F32)
        m = jnp.max(s1, axis=-1, keepdims=True)
        if ctx:
            s2 = lax.dot_general(q, ckx_scr[kh], nt, preferred_element_type=F32)
            m = jnp.maximum(m, jnp.max(s2, axis=-1, keepdims=True))
        ox = jnp.dot(jnp.exp2(s1 - m).astype(BF16), vx_scr[kh], preferred_element_type=F32)
        if ctx:
            ox = ox + jnp.dot(jnp.exp2(s2 - m).astype(BF16), cvx_scr[kh], preferred_element_type=F32)
        o = ox[:, :HEAD_DIM] / ox[:, HEAD_DIM:]
        ga = ga_refs[kh][:, hh * HEAD_DIM:(hh + 1) * HEAD_DIM]
        o_ref[:, cs] = (o * ga.astype(F32)).astype(BF16)


def _attn_call(proj, cache, *, n_batch, seq, layer):
    t = proj.shape[0]
    tq = min(ATTN_Q_TILE, seq)
    nq = seq // tq
    ctx = cache is not None
    in_specs = [
        pl.BlockSpec((tq, A_WIDTH), lambda b, qi: (b * nq + qi, Q_TILE0 * COL_TILE // A_WIDTH)),
        pl.BlockSpec((seq, KV_WIDTH), lambda b, qi: (b, K_BLOCK * HEAD_DIM // KV_WIDTH)),
        pl.BlockSpec((seq, KV_WIDTH), lambda b, qi: (b, V_BLOCK * HEAD_DIM // KV_WIDTH)),
        pl.BlockSpec((tq, COL_TILE), lambda b, qi: (b * nq + qi, GA_TILE0)),
        pl.BlockSpec((tq, COL_TILE), lambda b, qi: (b * nq + qi, GA_TILE0 + 1)),
    ]
    args = [proj, proj, proj, proj, proj]
    scratch = [pltpu.VMEM((A_KV_HEADS, seq, 2 * HEAD_DIM), BF16)]
    if ctx:
        past = cache[0].shape[2]
        for c in cache:
            in_specs.append(pl.BlockSpec((None, None, past, A_KV_HEADS, HEAD_DIM),
                                         lambda b, qi: (b, layer, 0, 0, 0)))
            args.append(c)
        scratch += [pltpu.VMEM((A_KV_HEADS, past, HEAD_DIM), BF16),
                    pltpu.VMEM((A_KV_HEADS, past, 2 * HEAD_DIM), BF16)]
    return pl.pallas_call(
        functools.partial(_attn_kernel, ctx=ctx),
        out_shape=jax.ShapeDtypeStruct((t, A_WIDTH), BF16),
        grid=(n_batch, nq),
        in_specs=in_specs,
        out_specs=pl.BlockSpec((tq, A_WIDTH), lambda b, qi: (b * nq + qi, 0)),
        scratch_shapes=scratch,
        compiler_params=_params("arbitrary", "arbitrary"),
        name="attention_lat" if ctx else "attention_ctx",
    )(*args)


def _mix_kernel(u_ref, vb_ref, gb_ref, z_ref, gc_ref, snw_ref, ws_ref, bs_ref, wp_ref, ps_ref,
                bo_ref, co_ref, zp_scr, *, seq):
    r = u_ref.shape[0]
    c = pl.program_id(1)
    nc = pl.num_programs(1)
    base = pl.multiple_of(c * r, r)

    for cc in range(r // CHUNK):
        rs = slice(cc * CHUNK, (cc + 1) * CHUNK)
        vbn = _rms(vb_ref[rs, :].astype(F32), snw_ref[...]).astype(BF16)
        for g in range(B_GROUPS):
            cs = slice(g * HEAD_DIM, (g + 1) * HEAD_DIM)
            mixed = jnp.dot(ws_ref[g], vbn[:, cs], preferred_element_type=F32) + bs_ref[g]
            bo_ref[rs, cs] = (u_ref[rs, cs].astype(F32) * mixed * gb_ref[rs, cs].astype(F32)).astype(BF16)

    zp_scr[POOL_HALO:POOL_HALO + r, :] = z_ref[pl.ds(base, r), :].astype(F32)

    @pl.when(c == 0)
    def _():
        zp_scr[0:POOL_HALO, :] = jnp.zeros((POOL_HALO, C_WIDTH), F32)

    @pl.when(c > 0)
    def _():
        zp_scr[0:POOL_HALO, :] = z_ref[pl.ds(pl.multiple_of(base - POOL_HALO, POOL_HALO), POOL_HALO), :].astype(F32)

    @pl.when(c == nc - 1)
    def _():
        zp_scr[POOL_HALO + r:, :] = jnp.zeros((POOL_HALO, C_WIDTH), F32)

    @pl.when(c < nc - 1)
    def _():
        zp_scr[POOL_HALO + r:, :] = z_ref[pl.ds(pl.multiple_of(base + r, POOL_HALO), POOL_HALO), :].astype(F32)

    t = (base + lax.broadcasted_iota(jnp.int32, (r, HEAD_DIM), 0))
    for g, w in enumerate(POOL_WINDOWS):
        cs = slice(g * HEAD_DIM, (g + 1) * HEAD_DIM)
        half = w // 2
        acc = zp_scr[POOL_HALO - half:POOL_HALO - half + r, cs]
        for o in range(-half + 1, half):
            acc = acc + zp_scr[POOL_HALO + o:POOL_HALO + o + r, cs]
        cnt = (jnp.minimum(t + half, seq) - jnp.maximum(t - half, 0)).astype(F32)
        d = acc / cnt - zp_scr[POOL_HALO:POOL_HALO + r, cs]
        dm = jnp.dot(d.astype(BF16), wp_ref[g], preferred_element_type=F32)
        co_ref[:, cs] = (dm * ps_ref[:, cs] * gc_ref[:, cs].astype(F32)).astype(BF16)


def _mix_call(proj, snw, ws, bs, wp, ps, *, n_batch, seq):
    t = proj.shape[0]
    r = min(MIX_TILE, seq)
    nc = seq // r
    row = lambda b, c: b * nc + c
    tile = lambda col: pl.BlockSpec((r, COL_TILE), lambda b, c: (row(b, c), col))
    const3 = lambda shape: pl.BlockSpec(shape, lambda b, c: (0, 0, 0))
    return pl.pallas_call(
        functools.partial(_mix_kernel, seq=seq),
        out_shape=[jax.ShapeDtypeStruct((t, B_WIDTH), BF16), jax.ShapeDtypeStruct((t, C_WIDTH), BF16)],
        grid=(n_batch, nc),
        in_specs=[
            tile(U_TILE), tile(VB_TILE), tile(GB_TILE),
            pl.BlockSpec((seq, COL_TILE), lambda b, c: (b, Z_TILE)),
            tile(GC_TILE),
            pl.BlockSpec((1, B_WIDTH), lambda b, c: (0, 0)),
            const3((B_GROUPS, CHUNK, CHUNK)),
            const3((B_GROUPS, CHUNK, 1)),
            const3((B_GROUPS, HEAD_DIM, HEAD_DIM)),
            pl.BlockSpec((1, C_WIDTH), lambda b, c: (0, 0)),
        ],
        out_specs=[pl.BlockSpec((r, B_WIDTH), lambda b, c: (row(b, c), 0)),
                   pl.BlockSpec((r, C_WIDTH), lambda b, c: (row(b, c), 0))],
        scratch_shapes=[pltpu.VMEM((r + 2 * POOL_HALO, C_WIDTH), F32)],
        compiler_params=_params("arbitrary", "arbitrary"),
        name="sgu_pool",
    )(proj, proj, proj, proj, proj, snw, ws, bs, wp, ps)


def _outproj_kernel(attn_ref, b_ref, c_ref, g0_ref, g1_ref, g2_ref, x_ref, mod_ref,
                    wa_ref, wb_ref, wc_ref, wo_ref, nw_ref, *rest, final):
    if final:
        y_ref, m_scr = rest
    else:
        modn_ref, y_ref, hn_ref, m_scr = rest
    for n in range(D_MODEL // COL_TILE):
        cs = slice(n * COL_TILE, (n + 1) * COL_TILE)
        a = jnp.dot(attn_ref[...], wa_ref[:, cs], preferred_element_type=F32)
        b = jnp.dot(b_ref[...], wb_ref[:, cs], preferred_element_type=F32)
        c = jnp.dot(c_ref[...], wc_ref[:, cs], preferred_element_type=F32)
        m = (g0_ref[:, cs].astype(F32) * a + g1_ref[:, cs].astype(F32) * b
             + g2_ref[:, cs].astype(F32) * c)
        m_scr[:, cs] = m.astype(BF16)
    ssq = jnp.zeros((x_ref.shape[0], 1), F32)
    for n in range(D_MODEL // COL_TILE):
        cs = slice(n * COL_TILE, (n + 1) * COL_TILE)
        out = jnp.dot(m_scr[...], wo_ref[:, cs], preferred_element_type=F32)
        y = x_ref[:, cs] + mod_ref[:, 2 * D_MODEL + n * COL_TILE:2 * D_MODEL + (n + 1) * COL_TILE] * out
        y_ref[:, cs] = y
        ssq = ssq + jnp.sum(y * y, axis=-1, keepdims=True)
    yn = y_ref[...] * lax.rsqrt(ssq * (1.0 / D_MODEL) + EPS) * nw_ref[...]
    if final:
        y_ref[...] = yn
    else:
        hn_ref[...] = (yn * (1.0 + modn_ref[:, D_MODEL:2 * D_MODEL]) + modn_ref[:, 0:D_MODEL]).astype(BF16)


def _outproj_call(attn, bout, cout, gates, x2d, mod3, wa, wb, wc, wo, nw, modn3, *, seq, n_batch_rows, layer):
    t = x2d.shape[0]
    tm = OUT_TOKEN_TILE
    final = modn3 is None
    mod_idx = _mod_index(n_batch_rows, max(seq // tm, 1))
    gate = lambda g: pl.BlockSpec((tm, D_MODEL), lambda i: (i, g))
    in_specs = [
        pl.BlockSpec((tm, A_WIDTH), lambda i: (i, 0)),
        pl.BlockSpec((tm, B_WIDTH), lambda i: (i, 0)),
        pl.BlockSpec((tm, C_WIDTH), lambda i: (i, 0)),
        gate(0), gate(1), gate(2),
        pl.BlockSpec((tm, D_MODEL), lambda i: (i, 0)),
        pl.BlockSpec((None, 1, GATE_COLS), mod_idx),
        _layer_resident((A_WIDTH, D_MODEL), layer), _layer_resident((B_WIDTH, D_MODEL), layer),
        _layer_resident((C_WIDTH, D_MODEL), layer), _layer_resident((D_MODEL, D_MODEL), layer),
        pl.BlockSpec((1, D_MODEL), lambda i: (0, 0)),
    ]
    args = [attn, bout, cout, gates, gates, gates, x2d, mod3, wa, wb, wc, wo, nw]
    out_shape = [jax.ShapeDtypeStruct((t, D_MODEL), F32)]
    out_specs = [pl.BlockSpec((tm, D_MODEL), lambda i: (i, 0))]
    if not final:
        in_specs.append(pl.BlockSpec((None, 1, GATE_COLS), mod_idx))
        args.append(modn3)
        out_shape.append(jax.ShapeDtypeStruct((t, D_MODEL), BF16))
        out_specs.append(pl.BlockSpec((tm, D_MODEL), lambda i: (i, 0)))
    return pl.pallas_call(
        functools.partial(_outproj_kernel, final=final),
        out_shape=out_shape,
        grid=(t // tm,),
        in_specs=in_specs,
        out_specs=out_specs,
        scratch_shapes=[pltpu.VMEM((tm, D_MODEL), BF16)],
        compiler_params=_params("arbitrary"),
        name="out_proj_final" if final else "out_proj",
    )(*args)


def _rope_tables(n_tokens):
    rows = n_tokens // GRID_W
    row = np.repeat(np.arange(rows), GRID_W).astype(np.float64)
    col = np.tile(np.arange(GRID_W), rows).astype(np.float64)
    n_freq = HEAD_DIM // 4
    inv = ROPE_THETA ** (-np.arange(n_freq, dtype=np.float64) / n_freq)
    ar = row[:, None] * inv[None, :]
    ac = col[:, None] * inv[None, :]
    zero = np.zeros_like(ar)
    cos = np.concatenate([np.cos(ar), np.cos(ar), np.cos(ac), np.cos(ac)], axis=-1)
    sa = np.concatenate([-np.sin(ar), zero, -np.sin(ac), zero], axis=-1)
    sb = np.concatenate([zero, np.sin(ar), zero, np.sin(ac)], axis=-1)
    return tuple(jnp.asarray(t, dtype=F32) for t in (cos, sa, sb))


def kernel(x_prompt, x_sample, cache_k, cache_v, c, c_ctx, norm_w, w_ada, b_ada, w_in, q_norm_w,
           k_norm_w, sgu_norm_w, w_sgu, b_sgu, w_pool, pool_scale, w_br_a, w_br_b, w_br_c, w_merge,
           b_merge, w_out, final_norm_w):
    nb_p, seq_p, d = x_prompt.shape
    nb_s, seq_s, _ = x_sample.shape
    past = cache_k.shape[2]
    assert d == D_MODEL and nb_s + 1 <= MOD_ROWS

    cv = jnp.concatenate([c_ctx[None, :], c, jnp.zeros((MOD_ROWS - 1 - nb_s, d), F32)], axis=0)
    mod = _ada_call(cv, w_ada, b_ada.reshape(DEPTH, 1, GATE_COLS))
    mod3 = [mod[l].reshape(MOD_ROWS, 1, GATE_COLS) for l in range(DEPTH)]
    rope_tabs = _rope_tables(seq_s)
    cache = (cache_k, cache_v)

    groups = [
        dict(x=x_prompt.reshape(nb_p * seq_p, d), nb=nb_p, seq=seq_p, rows=0, rope=None, cache=None),
        dict(x=x_sample.reshape(nb_s * seq_s, d), nb=nb_s, seq=seq_s, rows=nb_s, rope=rope_tabs, cache=cache),
    ]
    for g in groups:
        g["h"] = _normmod_call(g["x"], mod3[0], norm_w[0].reshape(1, d), seq=g["seq"], n_batch_rows=g["rows"])
    states = []
    w_mg, w_i, wa, wb, wc, wo = (w.astype(BF16) for w in (w_merge, w_in, w_br_a, w_br_b, w_br_c, w_out))
    for l in range(DEPTH):
        last = l == DEPTH - 1
        bm = b_merge[l].reshape(1, GATE_COLS)
        qnw = q_norm_w[l].reshape(1, HEAD_DIM)
        knw = k_norm_w[l].reshape(1, HEAD_DIM)
        mix_w = (sgu_norm_w[l].reshape(1, B_WIDTH), w_sgu[l].astype(BF16),
                 b_sgu[l].reshape(B_GROUPS, CHUNK, 1), w_pool[l].astype(BF16),
                 pool_scale[l].reshape(1, C_WIDTH))
        nw_next = (final_norm_w if last else norm_w[l + 1]).reshape(1, d)
        for g in groups:
            keep = g["cache"] is None
            gates = _gates_call(g["h"], w_mg, bm, layer=l)
            res = _proj_call(g["h"], w_i, qnw, knw, g["rope"], seq=g["seq"], state=keep, layer=l)
            proj = res[0]
            if keep:
                states.append(res[1:])
            attn = _attn_call(proj, g["cache"], n_batch=g["nb"], seq=g["seq"], layer=l)
            bout, cout = _mix_call(proj, *mix_w, n_batch=g["nb"], seq=g["seq"])
            res = _outproj_call(attn, bout, cout, gates, g["x"], mod3[l], wa, wb, wc, wo, nw_next,
                                None if last else mod3[l + 1], seq=g["seq"], n_batch_rows=g["rows"], layer=l)
            g["x"] = res[0]
            if not last:
                g["h"] = res[1]

    state_k, state_v = (
        jnp.stack([s[i].reshape(nb_p, seq_p, A_KV_HEADS, HEAD_DIM) for s in states], axis=1) for i in range(2))
    return (groups[0]["x"].reshape(nb_p, seq_p, d), groups[1]["x"].reshape(nb_s, seq_s, d), state_k, state_v)
```

```python
import functools

import jax
import jax.numpy as jnp
import numpy as np
from jax import lax
from jax.experimental import pallas as pl
from jax.experimental.pallas import tpu as pltpu

F32 = jnp.float32
BF16 = jnp.bfloat16

D_MODEL = 2048
DEPTH = 2
GRID_W = 64
EPS = 1e-6
HEAD_DIM = 128
A_HEADS = 8
A_KV_HEADS = 2
A_WIDTH = A_HEADS * HEAD_DIM
KV_WIDTH = A_KV_HEADS * HEAD_DIM
ROPE_THETA = 10000.0
ATTN_SCALE = HEAD_DIM ** -0.5
LOG2_E = 1.4426950408889634
CHUNK = 128
B_GROUPS = 4
B_WIDTH = 512
C_WIDTH = 512
POOL_WINDOWS = (2, 4, 8, 16)
POOL_HALO = 16
N_BRANCH = 3
GATE_COLS = N_BRANCH * D_MODEL
IN_COLS = 2 * A_WIDTH + 2 * KV_WIDTH + 3 * B_WIDTH + 2 * C_WIDTH

COL_TILE = 512
Q_TILE0 = 0
KV_TILE = Q_TILE0 + A_WIDTH // COL_TILE
GA_TILE0 = KV_TILE + 1
U_TILE = GA_TILE0 + A_WIDTH // COL_TILE
VB_TILE = U_TILE + 1
GB_TILE = VB_TILE + 1
Z_TILE = GB_TILE + 1
GC_TILE = Z_TILE + 1
HEADS_PER_TILE = COL_TILE // HEAD_DIM
K_COL = KV_TILE * COL_TILE
V_COL = K_COL + KV_WIDTH

SUB_COLS = 256
MOD_ROWS = 8
ADA_TILE = 1024
IN_TOKEN_TILE = 512
OUT_TOKEN_TILE = 256
ATTN_Q_TILE = 512
MIX_TILE = 512
VMEM_LIMIT = 56 * 1024 * 1024


def _params(*sem):
    return pltpu.CompilerParams(dimension_semantics=sem, vmem_limit_bytes=VMEM_LIMIT)


def _resident(shape):
    return pl.BlockSpec(shape, lambda *_: (0,) * len(shape), pipeline_mode=pl.Buffered(1))


def _layer_resident(shape, layer):
    return pl.BlockSpec((None,) + shape, lambda *_: (layer,) + (0,) * len(shape),
                        pipeline_mode=pl.Buffered(1))


def _silu(x):
    return x * jax.nn.sigmoid(x)


def _rms(x, w):
    ms = jnp.mean(x * x, axis=-1, keepdims=True)
    return x * lax.rsqrt(ms + EPS) * w


class _Stream:
    def __init__(self, n_ctx, n_lat, seq, tm):
        assert n_ctx % tm == 0 and seq % tm == 0
        self.tm = tm
        self.ctx_tiles = n_ctx // tm
        self.tiles = (n_ctx + n_lat) // tm
        self.tiles_per_seq = seq // tm

    def mod_index(self, i):
        lat = 1 + (i - self.ctx_tiles) // self.tiles_per_seq
        return (jnp.where(i < self.ctx_tiles, 0, lat), 0, 0)

    def two_source_specs(self, width):
        ctx = pl.BlockSpec((self.tm, width), lambda i: (jnp.minimum(i, self.ctx_tiles - 1), 0))
        lat = pl.BlockSpec((self.tm, width), lambda i: (jnp.maximum(i - self.ctx_tiles, 0), 0))
        return [ctx, lat]


def _pick(is_ctx, ctx_ref, lat_ref):
    return jnp.where(is_ctx, ctx_ref[...], lat_ref[...])


def _ada_kernel(cv_ref, w_ref, b_ref, o_ref):
    a = _silu(cv_ref[...]).astype(BF16)
    o_ref[...] = jnp.dot(a, w_ref[...].astype(BF16), preferred_element_type=F32) + b_ref[...]


def _ada_call(cv, w_ada, b_ada):
    return pl.pallas_call(
        _ada_kernel,
        out_shape=jax.ShapeDtypeStruct((DEPTH, MOD_ROWS, GATE_COLS), F32),
        grid=(DEPTH, GATE_COLS // ADA_TILE),
        in_specs=[
            pl.BlockSpec((MOD_ROWS, D_MODEL), lambda l, j: (0, 0)),
            pl.BlockSpec((None, D_MODEL, ADA_TILE), lambda l, j: (l, 0, j)),
            pl.BlockSpec((None, 1, ADA_TILE), lambda l, j: (l, 0, j)),
        ],
        out_specs=pl.BlockSpec((None, MOD_ROWS, ADA_TILE), lambda l, j: (l, 0, j)),
        compiler_params=_params("arbitrary", "arbitrary"),
        name="ada_mod",
    )(cv, w_ada, b_ada)


def _normmod_kernel(xc_ref, xl_ref, mod_ref, nw_ref, h_ref, *, ctx_tiles):
    x = _pick(pl.program_id(0) < ctx_tiles, xc_ref, xl_ref)
    shift = mod_ref[:, 0:D_MODEL]
    scale = mod_ref[:, D_MODEL:2 * D_MODEL]
    h_ref[...] = (_rms(x, nw_ref[...]) * (1.0 + scale) + shift).astype(BF16)


def _normmod_call(x_ctx, x_lat, mod3, norm_w, *, seq):
    st = _Stream(x_ctx.shape[0], x_lat.shape[0], seq, IN_TOKEN_TILE)
    return pl.pallas_call(
        functools.partial(_normmod_kernel, ctx_tiles=st.ctx_tiles),
        out_shape=jax.ShapeDtypeStruct((st.tiles * st.tm, D_MODEL), BF16),
        grid=(st.tiles,),
        in_specs=st.two_source_specs(D_MODEL) + [
            pl.BlockSpec((None, 1, GATE_COLS), st.mod_index),
            pl.BlockSpec((1, D_MODEL), lambda i: (0, 0)),
        ],
        out_specs=pl.BlockSpec((st.tm, D_MODEL), lambda i: (i, 0)),
        compiler_params=_params("arbitrary"),
        name="norm_mod",
    )(x_ctx, x_lat, mod3, norm_w)


def _gates_kernel(h_ref, w_ref, b_ref, o_ref):
    for s in range(GATE_COLS // SUB_COLS):
        cs = slice(s * SUB_COLS, (s + 1) * SUB_COLS)
        acc = jnp.dot(h_ref[...], w_ref[:, cs], preferred_element_type=F32)
        o_ref[:, cs] = jax.nn.sigmoid(acc + b_ref[:, cs]).astype(BF16)


def _gates_call(h, w_merge, b_merge, *, layer):
    t = h.shape[0]
    tm = IN_TOKEN_TILE
    return pl.pallas_call(
        _gates_kernel,
        out_shape=jax.ShapeDtypeStruct((t, GATE_COLS), BF16),
        grid=(t // tm,),
        in_specs=[
            pl.BlockSpec((tm, D_MODEL), lambda i: (i, 0)),
            _layer_resident((D_MODEL, GATE_COLS), layer),
            _resident((1, GATE_COLS)),
        ],
        out_specs=pl.BlockSpec((tm, GATE_COLS), lambda i: (i, 0)),
        compiler_params=_params("arbitrary"),
        name="gates",
    )(h, w_merge, b_merge)


def _rope(y, cos, sa, sb):
    return y * cos + pltpu.roll(y, 96, 1) * sa + pltpu.roll(y, 32, 1) * sb


def _proj_kernel(h_ref, w_ref, qnw_ref, knw_ref, cos_ref, sa_ref, sb_ref, proj_ref, ks_ref, vs_ref):
    def head(xh, w):
        return _rope(_rms(xh, w), cos_ref[...], sa_ref[...], sb_ref[...])

    qw = qnw_ref[...] * (ATTN_SCALE * LOG2_E)
    silu_cols = ((GA_TILE0 * COL_TILE, U_TILE * COL_TILE), (GB_TILE * COL_TILE, Z_TILE * COL_TILE),
                 (GC_TILE * COL_TILE, IN_COLS))
    for s in range(IN_COLS // SUB_COLS):
        acc = jnp.dot(h_ref[...], w_ref[:, s * SUB_COLS:(s + 1) * SUB_COLS], preferred_element_type=F32)
        for hb in range(SUB_COLS // HEAD_DIM):
            c0 = s * SUB_COLS + hb * HEAD_DIM
            cs = slice(c0, c0 + HEAD_DIM)
            a = acc[:, hb * HEAD_DIM:(hb + 1) * HEAD_DIM]
            if c0 < K_COL:
                proj_ref[:, cs] = head(a, qw).astype(BF16)
            elif c0 < V_COL:
                ks_ref[:, c0 - K_COL:c0 - K_COL + HEAD_DIM] = _rms(a, knw_ref[...])
                proj_ref[:, cs] = head(a, knw_ref[...]).astype(BF16)
            elif c0 < V_COL + KV_WIDTH:
                vs_ref[:, c0 - V_COL:c0 - V_COL + HEAD_DIM] = a
                proj_ref[:, cs] = a.astype(BF16)
            elif any(lo <= c0 < hi for lo, hi in silu_cols):
                proj_ref[:, cs] = _silu(a).astype(BF16)
            else:
                proj_ref[:, cs] = a.astype(BF16)


def _proj_call(h, w_in, qnw, knw, rope_tabs, *, st, layer):
    t = h.shape[0]
    tm = st.tm

    def tab_index(i):
        return (jnp.where(i < st.ctx_tiles, 0, 1 + (i - st.ctx_tiles) % st.tiles_per_seq), 0)

    in_specs = [
        pl.BlockSpec((tm, D_MODEL), lambda i: (i, 0)),
        _layer_resident((D_MODEL, IN_COLS), layer),
        pl.BlockSpec((1, HEAD_DIM), lambda i: (0, 0)),
        pl.BlockSpec((1, HEAD_DIM), lambda i: (0, 0)),
    ] + [pl.BlockSpec((tm, HEAD_DIM), tab_index) for _ in rope_tabs]
    out_shape = [jax.ShapeDtypeStruct((t, IN_COLS), BF16),
                 jax.ShapeDtypeStruct((t, KV_WIDTH), F32), jax.ShapeDtypeStruct((t, KV_WIDTH), F32)]
    out_specs = [pl.BlockSpec((tm, IN_COLS), lambda i: (i, 0)),
                 pl.BlockSpec((tm, KV_WIDTH), lambda i: (i, 0)), pl.BlockSpec((tm, KV_WIDTH), lambda i: (i, 0))]
    return pl.pallas_call(
        _proj_kernel,
        out_shape=out_shape,
        grid=(t // tm,),
        in_specs=in_specs,
        out_specs=out_specs,
        compiler_params=_params("arbitrary"),
        name="proj",
    )(h, w_in, qnw, knw, *rope_tabs)


def _attn_kernel(*refs, ctx):
    q_ref, k_ref, v_ref, ga0_ref, ga1_ref = refs[:5]
    ga_refs = (ga0_ref, ga1_ref)
    refs = refs[5:]
    if ctx:
        ck_ref, cv_ref = refs[:2]
        refs = refs[2:]
    o_ref, vx_scr = refs[:2]
    if ctx:
        ckx_scr, cvx_scr = refs[2:]

    @pl.when(pl.program_id(1) == 0)
    def _():
        for kh in range(A_KV_HEADS):
            hs = slice(kh * HEAD_DIM, (kh + 1) * HEAD_DIM)
            vx_scr[kh, :, :HEAD_DIM] = v_ref[:, hs]
            vx_scr[kh, :, HEAD_DIM:] = jnp.ones((vx_scr.shape[1], HEAD_DIM), BF16)
            if ctx:
                ckx_scr[kh] = ck_ref[:, kh, :].astype(BF16)
                cvx_scr[kh, :, :HEAD_DIM] = cv_ref[:, kh, :].astype(BF16)
                cvx_scr[kh, :, HEAD_DIM:] = jnp.ones((cvx_scr.shape[1], HEAD_DIM), BF16)

    nt = (((1,), (1,)), ((), ()))
    for head in range(A_HEADS):
        kh, hh = divmod(head, HEADS_PER_TILE)
        cs = slice(head * HEAD_DIM, (head + 1) * HEAD_DIM)
        q = q_ref[:, cs]
        s1 = lax.dot_general(q, k_ref[:, kh * HEAD_DIM:(kh + 1) * HEAD_DIM], nt, preferred_element_type=F32)
        m = jnp.max(s1, axis=-1, keepdims=True)
        if ctx:
            s2 = lax.dot_general(q, ckx_scr[kh], nt, preferred_element_type=F32)
            m = jnp.maximum(m, jnp.max(s2, axis=-1, keepdims=True))
        ox = jnp.dot(jnp.exp2(s1 - m).astype(BF16), vx_scr[kh], preferred_element_type=F32)
        if ctx:
            ox = ox + jnp.dot(jnp.exp2(s2 - m).astype(BF16), cvx_scr[kh], preferred_element_type=F32)
        o = ox[:, :HEAD_DIM] / ox[:, HEAD_DIM:]
        ga = ga_refs[kh][:, hh * HEAD_DIM:(hh + 1) * HEAD_DIM]
        o_ref[:, cs] = (o * ga.astype(F32)).astype(BF16)


def _attn_call(proj, cache, *, row0, n_batch, seq, layer):
    tq = min(ATTN_Q_TILE, seq)
    nq = seq // tq
    q0, s0 = row0 // tq, row0 // seq
    ctx = cache is not None
    in_specs = [
        pl.BlockSpec((tq, A_WIDTH), lambda b, qi: (q0 + b * nq + qi, Q_TILE0 * COL_TILE // A_WIDTH)),
        pl.BlockSpec((seq, KV_WIDTH), lambda b, qi: (s0 + b, K_COL // KV_WIDTH)),
        pl.BlockSpec((seq, KV_WIDTH), lambda b, qi: (s0 + b, V_COL // KV_WIDTH)),
        pl.BlockSpec((tq, COL_TILE), lambda b, qi: (q0 + b * nq + qi, GA_TILE0)),
        pl.BlockSpec((tq, COL_TILE), lambda b, qi: (q0 + b * nq + qi, GA_TILE0 + 1)),
    ]
    args = [proj, proj, proj, proj, proj]
    scratch = [pltpu.VMEM((A_KV_HEADS, seq, 2 * HEAD_DIM), BF16)]
    if ctx:
        past = cache[0].shape[2]
        for c in cache:
            in_specs.append(pl.BlockSpec((None, None, past, A_KV_HEADS, HEAD_DIM),
                                         lambda b, qi: (b, layer, 0, 0, 0)))
            args.append(c)
        scratch += [pltpu.VMEM((A_KV_HEADS, past, HEAD_DIM), BF16),
                    pltpu.VMEM((A_KV_HEADS, past, 2 * HEAD_DIM), BF16)]
    return pl.pallas_call(
        functools.partial(_attn_kernel, ctx=ctx),
        out_shape=jax.ShapeDtypeStruct((n_batch * seq, A_WIDTH), BF16),
        grid=(n_batch, nq),
        in_specs=in_specs,
        out_specs=pl.BlockSpec((tq, A_WIDTH), lambda b, qi: (b * nq + qi, 0)),
        scratch_shapes=scratch,
        compiler_params=_params("arbitrary", "arbitrary"),
        name="attention_lat" if ctx else "attention_ctx",
    )(*args)


def _mix_kernel(u_ref, vb_ref, gb_ref, z_ref, gc_ref, snw_ref, ws_ref, bs_ref, wp_ref, ps_ref,
                bo_ref, co_ref, zp_scr, *, seq):
    r = u_ref.shape[0]
    c = pl.program_id(1)
    nc = pl.num_programs(1)
    base = pl.multiple_of(c * r, r)

    for cc in range(r // CHUNK):
        rs = slice(cc * CHUNK, (cc + 1) * CHUNK)
        vbn = _rms(vb_ref[rs, :].astype(F32), snw_ref[...]).astype(BF16)
        for g in range(B_GROUPS):
            cs = slice(g * HEAD_DIM, (g + 1) * HEAD_DIM)
            mixed = jnp.dot(ws_ref[g], vbn[:, cs], preferred_element_type=F32) + bs_ref[g]
            bo_ref[rs, cs] = (u_ref[rs, cs].astype(F32) * mixed * gb_ref[rs, cs].astype(F32)).astype(BF16)

    zp_scr[POOL_HALO:POOL_HALO + r, :] = z_ref[pl.ds(base, r), :].astype(F32)

    @pl.when(c == 0)
    def _():
        zp_scr[0:POOL_HALO, :] = jnp.zeros((POOL_HALO, C_WIDTH), F32)

    @pl.when(c > 0)
    def _():
        zp_scr[0:POOL_HALO, :] = z_ref[pl.ds(pl.multiple_of(base - POOL_HALO, POOL_HALO), POOL_HALO), :].astype(F32)

    @pl.when(c == nc - 1)
    def _():
        zp_scr[POOL_HALO + r:, :] = jnp.zeros((POOL_HALO, C_WIDTH), F32)

    @pl.when(c < nc - 1)
    def _():
        zp_scr[POOL_HALO + r:, :] = z_ref[pl.ds(pl.multiple_of(base + r, POOL_HALO), POOL_HALO), :].astype(F32)

    t = (base + lax.broadcasted_iota(jnp.int32, (r, HEAD_DIM), 0))
    for g, w in enumerate(POOL_WINDOWS):
        cs = slice(g * HEAD_DIM, (g + 1) * HEAD_DIM)
        half = w // 2
        acc = zp_scr[POOL_HALO - half:POOL_HALO - half + r, cs]
        for o in range(-half + 1, half):
            acc = acc + zp_scr[POOL_HALO + o:POOL_HALO + o + r, cs]
        cnt = (jnp.minimum(t + half, seq) - jnp.maximum(t - half, 0)).astype(F32)
        d = acc / cnt - zp_scr[POOL_HALO:POOL_HALO + r, cs]
        dm = jnp.dot(d.astype(BF16), wp_ref[g], preferred_element_type=F32)
        co_ref[:, cs] = (dm * ps_ref[:, cs] * gc_ref[:, cs].astype(F32)).astype(BF16)


def _mix_call(proj, snw, ws, bs, wp, ps, *, row0, n_batch, seq):
    r = min(MIX_TILE, seq)
    nc = seq // r
    r0, s0 = row0 // r, row0 // seq
    tile = lambda col: pl.BlockSpec((r, COL_TILE), lambda b, c: (r0 + b * nc + c, col))
    const3 = lambda shape: pl.BlockSpec(shape, lambda b, c: (0, 0, 0))
    out_spec = pl.BlockSpec((r, COL_TILE), lambda b, c: (b * nc + c, 0))
    return pl.pallas_call(
        functools.partial(_mix_kernel, seq=seq),
        out_shape=[jax.ShapeDtypeStruct((n_batch * seq, B_WIDTH), BF16),
                   jax.ShapeDtypeStruct((n_batch * seq, C_WIDTH), BF16)],
        grid=(n_batch, nc),
        in_specs=[
            tile(U_TILE), tile(VB_TILE), tile(GB_TILE),
            pl.BlockSpec((seq, COL_TILE), lambda b, c: (s0 + b, Z_TILE)),
            tile(GC_TILE),
            pl.BlockSpec((1, B_WIDTH), lambda b, c: (0, 0)),
            const3((B_GROUPS, CHUNK, CHUNK)),
            const3((B_GROUPS, CHUNK, 1)),
            const3((B_GROUPS, HEAD_DIM, HEAD_DIM)),
            pl.BlockSpec((1, C_WIDTH), lambda b, c: (0, 0)),
        ],
        out_specs=[out_spec, out_spec],
        scratch_shapes=[pltpu.VMEM((r + 2 * POOL_HALO, C_WIDTH), F32)],
        compiler_params=_params("arbitrary", "arbitrary"),
        name="sgu_pool",
    )(proj, proj, proj, proj, proj, snw, ws, bs, wp, ps)


def _outproj_kernel(*refs, first, final, ctx_tiles):
    (ac_ref, al_ref, bc_ref, bl_ref, cc_ref, cl_ref, g0_ref, g1_ref, g2_ref) = refs[:9]
    refs = refs[9:]
    if first:
        xc_ref, xl_ref = refs[:2]
        refs = refs[2:]
    else:
        x_ref = refs[0]
        refs = refs[1:]
    mod_ref, wa_ref, wb_ref, wc_ref, wo_ref, nw_ref = refs[:6]
    refs = refs[6:]
    if final:
        yc_ref, yl_ref, a_scr, b_scr, c_scr, m_scr, y_scr = refs
    else:
        modn_ref, y_ref, hn_ref, a_scr, b_scr, c_scr, m_scr = refs
        y_scr = y_ref
    is_ctx = pl.program_id(0) < ctx_tiles
    a_scr[...] = _pick(is_ctx, ac_ref, al_ref)
    b_scr[...] = _pick(is_ctx, bc_ref, bl_ref)
    c_scr[...] = _pick(is_ctx, cc_ref, cl_ref)
    for n in range(D_MODEL // COL_TILE):
        cs = slice(n * COL_TILE, (n + 1) * COL_TILE)
        a = jnp.dot(a_scr[...], wa_ref[:, cs], preferred_element_type=F32)
        b = jnp.dot(b_scr[...], wb_ref[:, cs], preferred_element_type=F32)
        c = jnp.dot(c_scr[...], wc_ref[:, cs], preferred_element_type=F32)
        m = (g0_ref[:, cs].astype(F32) * a + g1_ref[:, cs].astype(F32) * b
             + g2_ref[:, cs].astype(F32) * c)
        m_scr[:, cs] = m.astype(BF16)
    ssq = jnp.zeros((m_scr.shape[0], 1), F32)
    for n in range(D_MODEL // COL_TILE):
        cs = slice(n * COL_TILE, (n + 1) * COL_TILE)
        out = jnp.dot(m_scr[...], wo_ref[:, cs], preferred_element_type=F32)
        x = jnp.where(is_ctx, xc_ref[:, cs], xl_ref[:, cs]) if first else x_ref[:, cs]
        y = x + mod_ref[:, 2 * D_MODEL + n * COL_TILE:2 * D_MODEL + (n + 1) * COL_TILE] * out
        y_scr[:, cs] = y
        ssq = ssq + jnp.sum(y * y, axis=-1, keepdims=True)
    yn = y_scr[...] * lax.rsqrt(ssq * (1.0 / D_MODEL) + EPS) * nw_ref[...]
    if final:
        @pl.when(is_ctx)
        def _():
            yc_ref[...] = yn

        @pl.when(jnp.logical_not(is_ctx))
        def _():
            yl_ref[...] = yn
    else:
        hn_ref[...] = (yn * (1.0 + modn_ref[:, D_MODEL:2 * D_MODEL]) + modn_ref[:, 0:D_MODEL]).astype(BF16)


def _outproj_call(attn, bout, cout, gates, x, mod3, wa, wb, wc, wo, nw, modn3, *, st, layer):
    tm = st.tm
    first = isinstance(x, tuple)
    final = modn3 is None
    gate = lambda g: pl.BlockSpec((tm, D_MODEL), lambda i: (i, g))
    row = pl.BlockSpec((tm, D_MODEL), lambda i: (i, 0))
    in_specs = (st.two_source_specs(A_WIDTH) + st.two_source_specs(B_WIDTH) + st.two_source_specs(C_WIDTH)
                + [gate(0), gate(1), gate(2)]
                + (st.two_source_specs(D_MODEL) if first else [row])
                + [pl.BlockSpec((None, 1, GATE_COLS), st.mod_index),
                   _layer_resident((A_WIDTH, D_MODEL), layer), _layer_resident((B_WIDTH, D_MODEL), layer),
                   _layer_resident((C_WIDTH, D_MODEL), layer), _layer_resident((D_MODEL, D_MODEL), layer),
                   pl.BlockSpec((1, D_MODEL), lambda i: (0, 0))])
    args = [*attn, *bout, *cout, gates, gates, gates, *(x if first else (x,)), mod3, wa, wb, wc, wo, nw]
    scratch = [pltpu.VMEM((tm, A_WIDTH), BF16), pltpu.VMEM((tm, B_WIDTH), BF16),
               pltpu.VMEM((tm, C_WIDTH), BF16), pltpu.VMEM((tm, D_MODEL), BF16)]
    if final:
        out_shape = [jax.ShapeDtypeStruct((st.ctx_tiles * tm, D_MODEL), F32),
                     jax.ShapeDtypeStruct(((st.tiles - st.ctx_tiles) * tm, D_MODEL), F32)]
        out_specs = st.two_source_specs(D_MODEL)
        scratch.append(pltpu.VMEM((tm, D_MODEL), F32))
    else:
        in_specs.append(pl.BlockSpec((None, 1, GATE_COLS), st.mod_index))
        args.append(modn3)
        out_shape = [jax.ShapeDtypeStruct((st.tiles * tm, D_MODEL), F32),
                     jax.ShapeDtypeStruct((st.tiles * tm, D_MODEL), BF16)]
        out_specs = [row, row]
    return pl.pallas_call(
        functools.partial(_outproj_kernel, first=first, final=final, ctx_tiles=st.ctx_tiles),
        out_shape=out_shape,
        grid=(st.tiles,),
        in_specs=in_specs,
        out_specs=out_specs,
        scratch_shapes=scratch,
        compiler_params=_params("arbitrary"),
        name="out_proj_final" if final else "out_proj",
    )(*args)


def _rope_tables(n_identity, n_tokens):
    rows = n_tokens // GRID_W
    row = np.repeat(np.arange(rows), GRID_W).astype(np.float64)
    col = np.tile(np.arange(GRID_W), rows).astype(np.float64)
    n_freq = HEAD_DIM // 4
    inv = ROPE_THETA ** (-np.arange(n_freq, dtype=np.float64) / n_freq)
    ar = row[:, None] * inv[None, :]
    ac = col[:, None] * inv[None, :]
    zero = np.zeros_like(ar)
    cos = np.concatenate([np.cos(ar), np.cos(ar), np.cos(ac), np.cos(ac)], axis=-1)
    sa = np.concatenate([-np.sin(ar), zero, -np.sin(ac), zero], axis=-1)
    sb = np.concatenate([zero, np.sin(ar), zero, np.sin(ac)], axis=-1)
    ident = (np.ones((n_identity, HEAD_DIM)), np.zeros((n_identity, HEAD_DIM)), np.zeros((n_identity, HEAD_DIM)))
    return tuple(jnp.asarray(np.concatenate([i, t], axis=0), dtype=F32) for i, t in zip(ident, (cos, sa, sb)))


def kernel(x_prompt, x_sample, cache_k, cache_v, c, c_ctx, norm_w, w_ada, b_ada, w_in, q_norm_w,
           k_norm_w, sgu_norm_w, w_sgu, b_sgu, w_pool, pool_scale, w_br_a, w_br_b, w_br_c, w_merge,
           b_merge, w_out, final_norm_w):
    nb_p, seq_p, d = x_prompt.shape
    nb_s, seq_s, _ = x_sample.shape
    n_ctx, n_lat = nb_p * seq_p, nb_s * seq_s
    assert d == D_MODEL and nb_s + 1 <= MOD_ROWS

    cv = jnp.concatenate([c_ctx[None, :], c, jnp.zeros((MOD_ROWS - 1 - nb_s, d), F32)], axis=0)
    mod = _ada_call(cv, w_ada, b_ada.reshape(DEPTH, 1, GATE_COLS))
    mod3 = [mod[l].reshape(MOD_ROWS, 1, GATE_COLS) for l in range(DEPTH)]
    st_in = _Stream(n_ctx, n_lat, seq_s, IN_TOKEN_TILE)
    st_out = _Stream(n_ctx, n_lat, seq_s, OUT_TOKEN_TILE)
    rope_tabs = _rope_tables(IN_TOKEN_TILE, seq_s)
    cache = (cache_k, cache_v)

    x = (x_prompt.reshape(n_ctx, d), x_sample.reshape(n_lat, d))
    h = _normmod_call(*x, mod3[0], norm_w[0].reshape(1, d), seq=seq_s)
    states = []
    w_mg, w_i, wa, wb, wc, wo = (w.astype(BF16) for w in (w_merge, w_in, w_br_a, w_br_b, w_br_c, w_out))
    for l in range(DEPTH):
        last = l == DEPTH - 1
        bm = b_merge[l].reshape(1, GATE_COLS)
        qnw = q_norm_w[l].reshape(1, HEAD_DIM)
        knw = k_norm_w[l].reshape(1, HEAD_DIM)
        mix_w = (sgu_norm_w[l].reshape(1, B_WIDTH), w_sgu[l].astype(BF16),
                 b_sgu[l].reshape(B_GROUPS, CHUNK, 1), w_pool[l].astype(BF16),
                 pool_scale[l].reshape(1, C_WIDTH))
        nw_next = (final_norm_w if last else norm_w[l + 1]).reshape(1, d)

        gates = _gates_call(h, w_mg, bm, layer=l)
        proj, ks, vs = _proj_call(h, w_i, qnw, knw, rope_tabs, st=st_in, layer=l)
        states.append((ks, vs))
        attn = (_attn_call(proj, None, row0=0, n_batch=nb_p, seq=seq_p, layer=l),
                _attn_call(proj, cache, row0=n_ctx, n_batch=nb_s, seq=seq_s, layer=l))
        mix_c = _mix_call(proj, *mix_w, row0=0, n_batch=nb_p, seq=seq_p)
        mix_l = _mix_call(proj, *mix_w, row0=n_ctx, n_batch=nb_s, seq=seq_s)
        res = _outproj_call(attn, (mix_c[0], mix_l[0]), (mix_c[1], mix_l[1]), gates, x, mod3[l],
                            wa, wb, wc, wo, nw_next, None if last else mod3[l + 1], st=st_out, layer=l)
        if last:
            y_ctx, y_lat = res
        else:
            x, h = res

    state_k, state_v = (
        jnp.stack([s[i][:n_ctx].reshape(nb_p, seq_p, A_KV_HEADS, HEAD_DIM) for s in states], axis=1)
        for i in range(2))
    return (y_ctx.reshape(nb_p, seq_p, d), y_lat.reshape(nb_s, seq_s, d), state_k, state_v)
```

```python
import functools

import jax
import jax.numpy as jnp
import numpy as np
from jax import lax
from jax.experimental import pallas as pl
from jax.experimental.pallas import tpu as pltpu

F32 = jnp.float32
BF16 = jnp.bfloat16

D_MODEL = 2048
DEPTH = 2
GRID_W = 64
EPS = 1e-6
HEAD_DIM = 128
A_HEADS = 8
A_KV_HEADS = 2
A_WIDTH = A_HEADS * HEAD_DIM
KV_WIDTH = A_KV_HEADS * HEAD_DIM
ROPE_THETA = 10000.0
ATTN_SCALE = HEAD_DIM ** -0.5
LOG2_E = 1.4426950408889634
CHUNK = 128
B_GROUPS = 4
B_WIDTH = 512
C_WIDTH = 512
POOL_WINDOWS = (2, 4, 8, 16)
POOL_HALO = 16
N_BRANCH = 3
GATE_COLS = N_BRANCH * D_MODEL
IN_COLS = 2 * A_WIDTH + 2 * KV_WIDTH + 3 * B_WIDTH + 2 * C_WIDTH

COL_TILE = 512
Q_TILE0 = 0
KV_TILE = Q_TILE0 + A_WIDTH // COL_TILE
GA_TILE0 = KV_TILE + 1
U_TILE = GA_TILE0 + A_WIDTH // COL_TILE
VB_TILE = U_TILE + 1
GB_TILE = VB_TILE + 1
Z_TILE = GB_TILE + 1
GC_TILE = Z_TILE + 1
HEADS_PER_TILE = COL_TILE // HEAD_DIM
K_COL = KV_TILE * COL_TILE
V_COL = K_COL + KV_WIDTH

SUB_COLS = 256
MOD_ROWS = 8
ADA_TILE = 1024
IN_TOKEN_TILE = 512
OUT_TOKEN_TILE = 256
ATTN_Q_TILE = 512
MIX_TILE = 512
CAST_ROWS = 128
OUT_CAST_ROWS = 256
VMEM_LIMIT = 56 * 1024 * 1024


def _params(*sem):
    return pltpu.CompilerParams(dimension_semantics=sem, vmem_limit_bytes=VMEM_LIMIT)


def _resident(shape):
    return pl.BlockSpec(shape, lambda *_: (0,) * len(shape), pipeline_mode=pl.Buffered(1))


def _silu(x):
    return x * jax.nn.sigmoid(x)


def _rms(x, w):
    ms = jnp.mean(x * x, axis=-1, keepdims=True)
    return x * lax.rsqrt(ms + EPS) * w


class _Stream:
    def __init__(self, n_ctx, n_lat, seq, tm):
        assert n_ctx % tm == 0 and seq % tm == 0
        self.tm = tm
        self.ctx_tiles = n_ctx // tm
        self.tiles = (n_ctx + n_lat) // tm
        self.tiles_per_seq = seq // tm

    def mod_index(self, i):
        lat = 1 + (i - self.ctx_tiles) // self.tiles_per_seq
        return (jnp.where(i < self.ctx_tiles, 0, lat), 0, 0)

    def two_source_specs(self, width):
        ctx = pl.BlockSpec((self.tm, width), lambda i: (jnp.minimum(i, self.ctx_tiles - 1), 0))
        lat = pl.BlockSpec((self.tm, width), lambda i: (jnp.maximum(i - self.ctx_tiles, 0), 0))
        return [ctx, lat]


def _pick(is_ctx, ctx_ref, lat_ref):
    return jnp.where(is_ctx, ctx_ref[...], lat_ref[...])


def _load_weight_bf16(w_hbm, layer, w_scr, stage, sem):
    chunk = stage.shape[1]
    n = w_scr.shape[0] // chunk

    def copy(c):
        return pltpu.make_async_copy(w_hbm.at[layer, pl.ds(c * chunk, chunk), :], stage.at[c % 2], sem.at[c % 2])

    copy(0).start()
    for c in range(n):
        if c + 1 < n:
            copy(c + 1).start()
        copy(c).wait()
        w_scr[c * chunk:(c + 1) * chunk, :] = stage[c % 2].astype(BF16)


def _weight_scratch(rows, cols, chunk):
    return [pltpu.VMEM((rows, cols), BF16), pltpu.VMEM((2, chunk, cols), F32), pltpu.SemaphoreType.DMA((2,))]


HBM = pl.BlockSpec(memory_space=pl.ANY)


def _ada_kernel(cv_ref, w_ref, b_ref, o_ref):
    a = _silu(cv_ref[...]).astype(BF16)
    o_ref[...] = jnp.dot(a, w_ref[...].astype(BF16), preferred_element_type=F32) + b_ref[...]


def _ada_call(cv, w_ada, b_ada):
    return pl.pallas_call(
        _ada_kernel,
        out_shape=jax.ShapeDtypeStruct((DEPTH, MOD_ROWS, GATE_COLS), F32),
        grid=(DEPTH, GATE_COLS // ADA_TILE),
        in_specs=[
            pl.BlockSpec((MOD_ROWS, D_MODEL), lambda l, j: (0, 0)),
            pl.BlockSpec((None, D_MODEL, ADA_TILE), lambda l, j: (l, 0, j)),
            pl.BlockSpec((None, 1, ADA_TILE), lambda l, j: (l, 0, j)),
        ],
        out_specs=pl.BlockSpec((None, MOD_ROWS, ADA_TILE), lambda l, j: (l, 0, j)),
        compiler_params=_params("arbitrary", "arbitrary"),
        name="ada_mod",
    )(cv, w_ada, b_ada)


def _normmod_kernel(xc_ref, xl_ref, mod_ref, nw_ref, h_ref, *, ctx_tiles):
    x = _pick(pl.program_id(0) < ctx_tiles, xc_ref, xl_ref)
    shift = mod_ref[:, 0:D_MODEL]
    scale = mod_ref[:, D_MODEL:2 * D_MODEL]
    h_ref[...] = (_rms(x, nw_ref[...]) * (1.0 + scale) + shift).astype(BF16)


def _normmod_call(x_ctx, x_lat, mod3, norm_w, *, seq):
    st = _Stream(x_ctx.shape[0], x_lat.shape[0], seq, IN_TOKEN_TILE)
    return pl.pallas_call(
        functools.partial(_normmod_kernel, ctx_tiles=st.ctx_tiles),
        out_shape=jax.ShapeDtypeStruct((st.tiles * st.tm, D_MODEL), BF16),
        grid=(st.tiles,),
        in_specs=st.two_source_specs(D_MODEL) + [
            pl.BlockSpec((None, 1, GATE_COLS), st.mod_index),
            pl.BlockSpec((1, D_MODEL), lambda i: (0, 0)),
        ],
        out_specs=pl.BlockSpec((st.tm, D_MODEL), lambda i: (i, 0)),
        compiler_params=_params("arbitrary"),
        name="norm_mod",
    )(x_ctx, x_lat, mod3, norm_w)


def _gates_kernel(h_ref, w_hbm, b_ref, o_ref, w_ref, stage, sem, *, layer):
    @pl.when(pl.program_id(0) == 0)
    def _():
        _load_weight_bf16(w_hbm, layer, w_ref, stage, sem)

    for s in range(GATE_COLS // SUB_COLS):
        cs = slice(s * SUB_COLS, (s + 1) * SUB_COLS)
        acc = jnp.dot(h_ref[...], w_ref[:, cs], preferred_element_type=F32)
        o_ref[:, cs] = jax.nn.sigmoid(acc + b_ref[:, cs]).astype(BF16)


def _gates_call(h, w_merge, b_merge, *, layer):
    t = h.shape[0]
    tm = IN_TOKEN_TILE
    return pl.pallas_call(
        functools.partial(_gates_kernel, layer=layer),
        out_shape=jax.ShapeDtypeStruct((t, GATE_COLS), BF16),
        grid=(t // tm,),
        in_specs=[
            pl.BlockSpec((tm, D_MODEL), lambda i: (i, 0)),
            HBM,
            _resident((1, GATE_COLS)),
        ],
        out_specs=pl.BlockSpec((tm, GATE_COLS), lambda i: (i, 0)),
        scratch_shapes=_weight_scratch(D_MODEL, GATE_COLS, CAST_ROWS),
        compiler_params=_params("arbitrary"),
        name="gates",
    )(h, w_merge, b_merge)


def _rope(y, cos, sa, sb):
    return y * cos + pltpu.roll(y, 96, 1) * sa + pltpu.roll(y, 32, 1) * sb


def _proj_kernel(h_ref, w_hbm, qnw_ref, knw_ref, cos_ref, sa_ref, sb_ref, proj_ref, ks_ref, vs_ref,
                 w_ref, stage, sem, *, layer):
    @pl.when(pl.program_id(0) == 0)
    def _():
        _load_weight_bf16(w_hbm, layer, w_ref, stage, sem)

    def head(xh, w):
        return _rope(_rms(xh, w), cos_ref[...], sa_ref[...], sb_ref[...])

    qw = qnw_ref[...] * (ATTN_SCALE * LOG2_E)
    silu_cols = ((GA_TILE0 * COL_TILE, U_TILE * COL_TILE), (GB_TILE * COL_TILE, Z_TILE * COL_TILE),
                 (GC_TILE * COL_TILE, IN_COLS))
    for s in range(IN_COLS // SUB_COLS):
        acc = jnp.dot(h_ref[...], w_ref[:, s * SUB_COLS:(s + 1) * SUB_COLS], preferred_element_type=F32)
        for hb in range(SUB_COLS // HEAD_DIM):
            c0 = s * SUB_COLS + hb * HEAD_DIM
            cs = slice(c0, c0 + HEAD_DIM)
            a = acc[:, hb * HEAD_DIM:(hb + 1) * HEAD_DIM]
            if c0 < K_COL:
                proj_ref[:, cs] = head(a, qw).astype(BF16)
            elif c0 < V_COL:
                ks_ref[:, c0 - K_COL:c0 - K_COL + HEAD_DIM] = _rms(a, knw_ref[...])
                proj_ref[:, cs] = head(a, knw_ref[...]).astype(BF16)
            elif c0 < V_COL + KV_WIDTH:
                vs_ref[:, c0 - V_COL:c0 - V_COL + HEAD_DIM] = a
                proj_ref[:, cs] = a.astype(BF16)
            elif any(lo <= c0 < hi for lo, hi in silu_cols):
                proj_ref[:, cs] = _silu(a).astype(BF16)
            else:
                proj_ref[:, cs] = a.astype(BF16)


def _proj_call(h, w_in, qnw, knw, rope_tabs, *, st, layer):
    t = h.shape[0]
    tm = st.tm

    def tab_index(i):
        return (jnp.where(i < st.ctx_tiles, 0, 1 + (i - st.ctx_tiles) % st.tiles_per_seq), 0)

    in_specs = [
        pl.BlockSpec((tm, D_MODEL), lambda i: (i, 0)),
        HBM,
        pl.BlockSpec((1, HEAD_DIM), lambda i: (0, 0)),
        pl.BlockSpec((1, HEAD_DIM), lambda i: (0, 0)),
    ] + [pl.BlockSpec((tm, HEAD_DIM), tab_index) for _ in rope_tabs]
    out_shape = [jax.ShapeDtypeStruct((t, IN_COLS), BF16),
                 jax.ShapeDtypeStruct((t, KV_WIDTH), F32), jax.ShapeDtypeStruct((t, KV_WIDTH), F32)]
    out_specs = [pl.BlockSpec((tm, IN_COLS), lambda i: (i, 0)),
                 pl.BlockSpec((tm, KV_WIDTH), lambda i: (i, 0)), pl.BlockSpec((tm, KV_WIDTH), lambda i: (i, 0))]
    return pl.pallas_call(
        functools.partial(_proj_kernel, layer=layer),
        out_shape=out_shape,
        grid=(t // tm,),
        in_specs=in_specs,
        out_specs=out_specs,
        scratch_shapes=_weight_scratch(D_MODEL, IN_COLS, CAST_ROWS),
        compiler_params=_params("arbitrary"),
        name="proj",
    )(h, w_in, qnw, knw, *rope_tabs)


def _attn_kernel(*refs, ctx):
    q_ref, k_ref, v_ref, ga0_ref, ga1_ref = refs[:5]
    ga_refs = (ga0_ref, ga1_ref)
    refs = refs[5:]
    if ctx:
        ck_ref, cv_ref = refs[:2]
        refs = refs[2:]
    o_ref, vx_scr = refs[:2]
    if ctx:
        ckx_scr, cvx_scr = refs[2:]

    @pl.when(pl.program_id(1) == 0)
    def _():
        for kh in range(A_KV_HEADS):
            hs = slice(kh * HEAD_DIM, (kh + 1) * HEAD_DIM)
            vx_scr[kh, :, :HEAD_DIM] = v_ref[:, hs]
            vx_scr[kh, :, HEAD_DIM:] = jnp.ones((vx_scr.shape[1], HEAD_DIM), BF16)
            if ctx:
                ckx_scr[kh] = ck_ref[:, kh, :].astype(BF16)
                cvx_scr[kh, :, :HEAD_DIM] = cv_ref[:, kh, :].astype(BF16)
                cvx_scr[kh, :, HEAD_DIM:] = jnp.ones((cvx_scr.shape[1], HEAD_DIM), BF16)

    nt = (((1,), (1,)), ((), ()))
    for head in range(A_HEADS):
        kh, hh = divmod(head, HEADS_PER_TILE)
        cs = slice(head * HEAD_DIM, (head + 1) * HEAD_DIM)
        q = q_ref[:, cs]
        s1 = lax.dot_general(q, k_ref[:, kh * HEAD_DIM:(kh + 1) * HEAD_DIM], nt, preferred_element_type=F32)
        m = jnp.max(s1, axis=-1, keepdims=True)
        if ctx:
            s2 = lax.dot_general(q, ckx_scr[kh], nt, preferred_element_type=F32)
            m = jnp.maximum(m, jnp.max(s2, axis=-1, keepdims=True))
        ox = jnp.dot(jnp.exp2(s1 - m).astype(BF16), vx_scr[kh], preferred_element_type=F32)
        if ctx:
            ox = ox + jnp.dot(jnp.exp2(s2 - m).astype(BF16), cvx_scr[kh], preferred_element_type=F32)
        o = ox[:, :HEAD_DIM] / ox[:, HEAD_DIM:]
        ga = ga_refs[kh][:, hh * HEAD_DIM:(hh + 1) * HEAD_DIM]
        o_ref[:, cs] = (o * ga.astype(F32)).astype(BF16)


def _attn_call(proj, cache, *, row0, n_batch, seq, layer):
    tq = min(ATTN_Q_TILE, seq)
    nq = seq // tq
    q0, s0 = row0 // tq, row0 // seq
    ctx = cache is not None
    in_specs = [
        pl.BlockSpec((tq, A_WIDTH), lambda b, qi: (q0 + b * nq + qi, Q_TILE0 * COL_TILE // A_WIDTH)),
        pl.BlockSpec((seq, KV_WIDTH), lambda b, qi: (s0 + b, K_COL // KV_WIDTH)),
        pl.BlockSpec((seq, KV_WIDTH), lambda b, qi: (s0 + b, V_COL // KV_WIDTH)),
        pl.BlockSpec((tq, COL_TILE), lambda b, qi: (q0 + b * nq + qi, GA_TILE0)),
        pl.BlockSpec((tq, COL_TILE), lambda b, qi: (q0 + b * nq + qi, GA_TILE0 + 1)),
    ]
    args = [proj, proj, proj, proj, proj]
    scratch = [pltpu.VMEM((A_KV_HEADS, seq, 2 * HEAD_DIM), BF16)]
    if ctx:
        past = cache[0].shape[2]
        for c in cache:
            in_specs.append(pl.BlockSpec((None, None, past, A_KV_HEADS, HEAD_DIM),
                                         lambda b, qi: (b, layer, 0, 0, 0)))
            args.append(c)
        scratch += [pltpu.VMEM((A_KV_HEADS, past, HEAD_DIM), BF16),
                    pltpu.VMEM((A_KV_HEADS, past, 2 * HEAD_DIM), BF16)]
    return pl.pallas_call(
        functools.partial(_attn_kernel, ctx=ctx),
        out_shape=jax.ShapeDtypeStruct((n_batch * seq, A_WIDTH), BF16),
        grid=(n_batch, nq),
        in_specs=in_specs,
        out_specs=pl.BlockSpec((tq, A_WIDTH), lambda b, qi: (b * nq + qi, 0)),
        scratch_shapes=scratch,
        compiler_params=_params("arbitrary", "arbitrary"),
        name="attention_lat" if ctx else "attention_ctx",
    )(*args)


def _mix_kernel(u_ref, vb_ref, gb_ref, z_ref, gc_ref, snw_ref, ws_ref, bs_ref, wp_ref, ps_ref,
                bo_ref, co_ref, zp_scr, *, seq):
    r = u_ref.shape[0]
    c = pl.program_id(1)
    nc = pl.num_programs(1)
    base = pl.multiple_of(c * r, r)

    for cc in range(r // CHUNK):
        rs = slice(cc * CHUNK, (cc + 1) * CHUNK)
        vbn = _rms(vb_ref[rs, :].astype(F32), snw_ref[...]).astype(BF16)
        for g in range(B_GROUPS):
            cs = slice(g * HEAD_DIM, (g + 1) * HEAD_DIM)
            mixed = jnp.dot(ws_ref[g], vbn[:, cs], preferred_element_type=F32) + bs_ref[g]
            bo_ref[rs, cs] = (u_ref[rs, cs].astype(F32) * mixed * gb_ref[rs, cs].astype(F32)).astype(BF16)

    zp_scr[POOL_HALO:POOL_HALO + r, :] = z_ref[pl.ds(base, r), :].astype(F32)

    @pl.when(c == 0)
    def _():
        zp_scr[0:POOL_HALO, :] = jnp.zeros((POOL_HALO, C_WIDTH), F32)

    @pl.when(c > 0)
    def _():
        zp_scr[0:POOL_HALO, :] = z_ref[pl.ds(pl.multiple_of(base - POOL_HALO, POOL_HALO), POOL_HALO), :].astype(F32)

    @pl.when(c == nc - 1)
    def _():
        zp_scr[POOL_HALO + r:, :] = jnp.zeros((POOL_HALO, C_WIDTH), F32)

    @pl.when(c < nc - 1)
    def _():
        zp_scr[POOL_HALO + r:, :] = z_ref[pl.ds(pl.multiple_of(base + r, POOL_HALO), POOL_HALO), :].astype(F32)

    t = (base + lax.broadcasted_iota(jnp.int32, (r, HEAD_DIM), 0))
    for g, w in enumerate(POOL_WINDOWS):
        cs = slice(g * HEAD_DIM, (g + 1) * HEAD_DIM)
        half = w // 2
        acc = zp_scr[POOL_HALO - half:POOL_HALO - half + r, cs]
        for o in range(-half + 1, half):
            acc = acc + zp_scr[POOL_HALO + o:POOL_HALO + o + r, cs]
        cnt = (jnp.minimum(t + half, seq) - jnp.maximum(t - half, 0)).astype(F32)
        d = acc / cnt - zp_scr[POOL_HALO:POOL_HALO + r, cs]
        dm = jnp.dot(d.astype(BF16), wp_ref[g], preferred_element_type=F32)
        co_ref[:, cs] = (dm * ps_ref[:, cs] * gc_ref[:, cs].astype(F32)).astype(BF16)


def _mix_call(proj, snw, ws, bs, wp, ps, *, row0, n_batch, seq):
    r = min(MIX_TILE, seq)
    nc = seq // r
    r0, s0 = row0 // r, row0 // seq
    tile = lambda col: pl.BlockSpec((r, COL_TILE), lambda b, c: (r0 + b * nc + c, col))
    const3 = lambda shape: pl.BlockSpec(shape, lambda b, c: (0, 0, 0))
    out_spec = pl.BlockSpec((r, COL_TILE), lambda b, c: (b * nc + c, 0))
    return pl.pallas_call(
        functools.partial(_mix_kernel, seq=seq),
        out_shape=[jax.ShapeDtypeStruct((n_batch * seq, B_WIDTH), BF16),
                   jax.ShapeDtypeStruct((n_batch * seq, C_WIDTH), BF16)],
        grid=(n_batch, nc),
        in_specs=[
            tile(U_TILE), tile(VB_TILE), tile(GB_TILE),
            pl.BlockSpec((seq, COL_TILE), lambda b, c: (s0 + b, Z_TILE)),
            tile(GC_TILE),
            pl.BlockSpec((1, B_WIDTH), lambda b, c: (0, 0)),
            const3((B_GROUPS, CHUNK, CHUNK)),
            const3((B_GROUPS, CHUNK, 1)),
            const3((B_GROUPS, HEAD_DIM, HEAD_DIM)),
            pl.BlockSpec((1, C_WIDTH), lambda b, c: (0, 0)),
        ],
        out_specs=[out_spec, out_spec],
        scratch_shapes=[pltpu.VMEM((r + 2 * POOL_HALO, C_WIDTH), F32)],
        compiler_params=_params("arbitrary", "arbitrary"),
        name="sgu_pool",
    )(proj, proj, proj, proj, proj, snw, ws, bs, wp, ps)


def _outproj_kernel(*refs, first, final, ctx_tiles, layer):
    (ac_ref, al_ref, bc_ref, bl_ref, cc_ref, cl_ref, g0_ref, g1_ref, g2_ref) = refs[:9]
    refs = refs[9:]
    if first:
        xc_ref, xl_ref = refs[:2]
        refs = refs[2:]
    else:
        x_ref = refs[0]
        refs = refs[1:]
    mod_ref, wa_hbm, wb_hbm, wc_hbm, wo_hbm, nw_ref = refs[:6]
    refs = refs[6:]
    if final:
        yc_ref, yl_ref = refs[:2]
        refs = refs[2:]
    else:
        modn_ref, y_ref, hn_ref = refs[:3]
        refs = refs[3:]
    a_scr, b_scr, c_scr, m_scr, wa_ref, wb_ref, wc_ref, wo_ref, stage, sem = refs[:10]
    y_scr = refs[10] if final else y_ref

    @pl.when(pl.program_id(0) == 0)
    def _():
        for w_hbm, w_ref in ((wa_hbm, wa_ref), (wb_hbm, wb_ref), (wc_hbm, wc_ref), (wo_hbm, wo_ref)):
            _load_weight_bf16(w_hbm, layer, w_ref, stage, sem)

    is_ctx = pl.program_id(0) < ctx_tiles
    a_scr[...] = _pick(is_ctx, ac_ref, al_ref)
    b_scr[...] = _pick(is_ctx, bc_ref, bl_ref)
    c_scr[...] = _pick(is_ctx, cc_ref, cl_ref)
    for n in range(D_MODEL // COL_TILE):
        cs = slice(n * COL_TILE, (n + 1) * COL_TILE)
        a = jnp.dot(a_scr[...], wa_ref[:, cs], preferred_element_type=F32)
        b = jnp.dot(b_scr[...], wb_ref[:, cs], preferred_element_type=F32)
        c = jnp.dot(c_scr[...], wc_ref[:, cs], preferred_element_type=F32)
        m = (g0_ref[:, cs].astype(F32) * a + g1_ref[:, cs].astype(F32) * b
             + g2_ref[:, cs].astype(F32) * c)
        m_scr[:, cs] = m.astype(BF16)
    ssq = jnp.zeros((m_scr.shape[0], 1), F32)
    for n in range(D_MODEL // COL_TILE):
        cs = slice(n * COL_TILE, (n + 1) * COL_TILE)
        out = jnp.dot(m_scr[...], wo_ref[:, cs], preferred_element_type=F32)
        x = jnp.where(is_ctx, xc_ref[:, cs], xl_ref[:, cs]) if first else x_ref[:, cs]
        y = x + mod_ref[:, 2 * D_MODEL + n * COL_TILE:2 * D_MODEL + (n + 1) * COL_TILE] * out
        y_scr[:, cs] = y
        ssq = ssq + jnp.sum(y * y, axis=-1, keepdims=True)
    yn = y_scr[...] * lax.rsqrt(ssq * (1.0 / D_MODEL) + EPS) * nw_ref[...]
    if final:
        @pl.when(is_ctx)
        def _():
            yc_ref[...] = yn

        @pl.when(jnp.logical_not(is_ctx))
        def _():
            yl_ref[...] = yn
    else:
        hn_ref[...] = (yn * (1.0 + modn_ref[:, D_MODEL:2 * D_MODEL]) + modn_ref[:, 0:D_MODEL]).astype(BF16)


def _outproj_call(attn, bout, cout, gates, x, mod3, wa, wb, wc, wo, nw, modn3, *, st, layer):
    tm = st.tm
    first = isinstance(x, tuple)
    final = modn3 is None
    gate = lambda g: pl.BlockSpec((tm, D_MODEL), lambda i: (i, g))
    row = pl.BlockSpec((tm, D_MODEL), lambda i: (i, 0))
    in_specs = (st.two_source_specs(A_WIDTH) + st.two_source_specs(B_WIDTH) + st.two_source_specs(C_WIDTH)
                + [gate(0), gate(1), gate(2)]
                + (st.two_source_specs(D_MODEL) if first else [row])
                + [pl.BlockSpec((None, 1, GATE_COLS), st.mod_index),
                   HBM, HBM, HBM, HBM,
                   pl.BlockSpec((1, D_MODEL), lambda i: (0, 0))])
    args = [*attn, *bout, *cout, gates, gates, gates, *(x if first else (x,)), mod3, wa, wb, wc, wo, nw]
    scratch = [pltpu.VMEM((tm, A_WIDTH), BF16), pltpu.VMEM((tm, B_WIDTH), BF16),
               pltpu.VMEM((tm, C_WIDTH), BF16), pltpu.VMEM((tm, D_MODEL), BF16),
               pltpu.VMEM((A_WIDTH, D_MODEL), BF16), pltpu.VMEM((B_WIDTH, D_MODEL), BF16),
               pltpu.VMEM((C_WIDTH, D_MODEL), BF16), pltpu.VMEM((D_MODEL, D_MODEL), BF16),
               pltpu.VMEM((2, OUT_CAST_ROWS, D_MODEL), F32), pltpu.SemaphoreType.DMA((2,))]
    if final:
        out_shape = [jax.ShapeDtypeStruct((st.ctx_tiles * tm, D_MODEL), F32),
                     jax.ShapeDtypeStruct(((st.tiles - st.ctx_tiles) * tm, D_MODEL), F32)]
        out_specs = st.two_source_specs(D_MODEL)
        scratch.append(pltpu.VMEM((tm, D_MODEL), F32))
    else:
        in_specs.append(pl.BlockSpec((None, 1, GATE_COLS), st.mod_index))
        args.append(modn3)
        out_shape = [jax.ShapeDtypeStruct((st.tiles * tm, D_MODEL), F32),
                     jax.ShapeDtypeStruct((st.tiles * tm, D_MODEL), BF16)]
        out_specs = [row, row]
    return pl.pallas_call(
        functools.partial(_outproj_kernel, first=first, final=final, ctx_tiles=st.ctx_tiles, layer=layer),
        out_shape=out_shape,
        grid=(st.tiles,),
        in_specs=in_specs,
        out_specs=out_specs,
        scratch_shapes=scratch,
        compiler_params=_params("arbitrary"),
        name="out_proj_final" if final else "out_proj",
    )(*args)


def _rope_tables(n_identity, n_tokens):
    rows = n_tokens // GRID_W
    row = np.repeat(np.arange(rows), GRID_W).astype(np.float64)
    col = np.tile(np.arange(GRID_W), rows).astype(np.float64)
    n_freq = HEAD_DIM // 4
    inv = ROPE_THETA ** (-np.arange(n_freq, dtype=np.float64) / n_freq)
    ar = row[:, None] * inv[None, :]
    ac = col[:, None] * inv[None, :]
    zero = np.zeros_like(ar)
    cos = np.concatenate([np.cos(ar), np.cos(ar), np.cos(ac), np.cos(ac)], axis=-1)
    sa = np.concatenate([-np.sin(ar), zero, -np.sin(ac), zero], axis=-1)
    sb = np.concatenate([zero, np.sin(ar), zero, np.sin(ac)], axis=-1)
    ident = (np.ones((n_identity, HEAD_DIM)), np.zeros((n_identity, HEAD_DIM)), np.zeros((n_identity, HEAD_DIM)))
    return tuple(jnp.asarray(np.concatenate([i, t], axis=0), dtype=F32) for i, t in zip(ident, (cos, sa, sb)))


def kernel(x_prompt, x_sample, cache_k, cache_v, c, c_ctx, norm_w, w_ada, b_ada, w_in, q_norm_w,
           k_norm_w, sgu_norm_w, w_sgu, b_sgu, w_pool, pool_scale, w_br_a, w_br_b, w_br_c, w_merge,
           b_merge, w_out, final_norm_w):
    nb_p, seq_p, d = x_prompt.shape
    nb_s, seq_s, _ = x_sample.shape
    n_ctx, n_lat = nb_p * seq_p, nb_s * seq_s
    assert d == D_MODEL and nb_s + 1 <= MOD_ROWS

    cv = jnp.concatenate([c_ctx[None, :], c, jnp.zeros((MOD_ROWS - 1 - nb_s, d), F32)], axis=0)
    mod = _ada_call(cv, w_ada, b_ada.reshape(DEPTH, 1, GATE_COLS))
    mod3 = [mod[l].reshape(MOD_ROWS, 1, GATE_COLS) for l in range(DEPTH)]
    st_in = _Stream(n_ctx, n_lat, seq_s, IN_TOKEN_TILE)
    st_out = _Stream(n_ctx, n_lat, seq_s, OUT_TOKEN_TILE)
    rope_tabs = _rope_tables(IN_TOKEN_TILE, seq_s)
    cache = (cache_k, cache_v)

    x = (x_prompt.reshape(n_ctx, d), x_sample.reshape(n_lat, d))
    h = _normmod_call(*x, mod3[0], norm_w[0].reshape(1, d), seq=seq_s)
    states = []
    w_mg, w_i, wa, wb, wc, wo = w_merge, w_in, w_br_a, w_br_b, w_br_c, w_out
    for l in range(DEPTH):
        last = l == DEPTH - 1
        bm = b_merge[l].reshape(1, GATE_COLS)
        qnw = q_norm_w[l].reshape(1, HEAD_DIM)
        knw = k_norm_w[l].reshape(1, HEAD_DIM)
        mix_w = (sgu_norm_w[l].reshape(1, B_WIDTH), w_sgu[l].astype(BF16),
                 b_sgu[l].reshape(B_GROUPS, CHUNK, 1), w_pool[l].astype(BF16),
                 pool_scale[l].reshape(1, C_WIDTH))
        nw_next = (final_norm_w if last else norm_w[l + 1]).reshape(1, d)

        gates = _gates_call(h, w_mg, bm, layer=l)
        proj, ks, vs = _proj_call(h, w_i, qnw, knw, rope_tabs, st=st_in, layer=l)
        states.append((ks, vs))
        attn = (_attn_call(proj, None, row0=0, n_batch=nb_p, seq=seq_p, layer=l),
                _attn_call(proj, cache, row0=n_ctx, n_batch=nb_s, seq=seq_s, layer=l))
        mix_c = _mix_call(proj, *mix_w, row0=0, n_batch=nb_p, seq=seq_p)
        mix_l = _mix_call(proj, *mix_w, row0=n_ctx, n_batch=nb_s, seq=seq_s)
        res = _outproj_call(attn, (mix_c[0], mix_l[0]), (mix_c[1], mix_l[1]), gates, x, mod3[l],
                            wa, wb, wc, wo, nw_next, None if last else mod3[l + 1], st=st_out, layer=l)
        if last:
            y_ctx, y_lat = res
        else:
            x, h = res

    state_k, state_v = (
        jnp.stack([s[i][:n_ctx].reshape(nb_p, seq_p, A_KV_HEADS, HEAD_DIM) for s in states], axis=1)
        for i in range(2))
    return (y_ctx.reshape(nb_p, seq_p, d), y_lat.reshape(nb_s, seq_s, d), state_k, state_v)
```

```python
import functools

import jax
import jax.numpy as jnp
import numpy as np
from jax import lax
from jax.experimental import pallas as pl
from jax.experimental.pallas import tpu as pltpu

F32 = jnp.float32
BF16 = jnp.bfloat16

D_MODEL = 2048
DEPTH = 2
GRID_W = 64
EPS = 1e-6
HEAD_DIM = 128
A_HEADS = 8
A_KV_HEADS = 2
A_WIDTH = A_HEADS * HEAD_DIM
KV_WIDTH = A_KV_HEADS * HEAD_DIM
ROPE_THETA = 10000.0
ATTN_SCALE = HEAD_DIM ** -0.5
LOG2_E = 1.4426950408889634
CHUNK = 128
B_GROUPS = 4
B_WIDTH = 512
C_WIDTH = 512
POOL_WINDOWS = (2, 4, 8, 16)
POOL_HALO = 64
N_BRANCH = 3
GATE_COLS = N_BRANCH * D_MODEL
IN_COLS = 2 * A_WIDTH + 2 * KV_WIDTH + 3 * B_WIDTH + 2 * C_WIDTH

COL_TILE = 512
Q_TILE0 = 0
KV_TILE = Q_TILE0 + A_WIDTH // COL_TILE
GA_TILE0 = KV_TILE + 1
U_TILE = GA_TILE0 + A_WIDTH // COL_TILE
VB_TILE = U_TILE + 1
GB_TILE = VB_TILE + 1
Z_TILE = GB_TILE + 1
GC_TILE = Z_TILE + 1
HEADS_PER_TILE = COL_TILE // HEAD_DIM
K_COL = KV_TILE * COL_TILE
V_COL = K_COL + KV_WIDTH

SUB_COLS = 256
MOD_ROWS = 8
ADA_TILE = 1024
IN_TOKEN_TILE = 512
OUT_TOKEN_TILE = 256
ATTN_Q_TILE = 512
MIX_TILE = 512
CAST_ROWS = 128
OUT_CAST_ROWS = 256
VMEM_LIMIT = 56 * 1024 * 1024


def _params(*sem):
    return pltpu.CompilerParams(dimension_semantics=sem, vmem_limit_bytes=VMEM_LIMIT)


def _resident(shape):
    return pl.BlockSpec(shape, lambda *_: (0,) * len(shape), pipeline_mode=pl.Buffered(1))


def _silu(x):
    return x * jax.nn.sigmoid(x)


def _rms(x, w):
    ms = jnp.mean(x * x, axis=-1, keepdims=True)
    return x * lax.rsqrt(ms + EPS) * w


class _Stream:
    def __init__(self, n_ctx, n_lat, seq, tm):
        assert n_ctx % tm == 0 and seq % tm == 0
        self.tm = tm
        self.ctx_tiles = n_ctx // tm
        self.tiles = (n_ctx + n_lat) // tm
        self.tiles_per_seq = seq // tm

    def mod_index(self, i):
        lat = 1 + (i - self.ctx_tiles) // self.tiles_per_seq
        return (jnp.where(i < self.ctx_tiles, 0, lat), 0, 0)

    def two_source_specs(self, width, of=lambda i: i):
        ctx = pl.BlockSpec((self.tm, width), lambda i: (jnp.minimum(of(i), self.ctx_tiles - 1), 0))
        lat = pl.BlockSpec((self.tm, width), lambda i: (jnp.maximum(of(i) - self.ctx_tiles, 0), 0))
        return [ctx, lat]


def _pick(is_ctx, ctx_ref, lat_ref):
    return jnp.where(is_ctx, ctx_ref[...], lat_ref[...])


def _load_weight_bf16(w_hbm, layer, w_scr, stage, sem):
    chunk = stage.shape[1]
    n = w_scr.shape[0] // chunk

    def copy(c):
        return pltpu.make_async_copy(w_hbm.at[layer, pl.ds(c * chunk, chunk), :], stage.at[c % 2], sem.at[c % 2])

    copy(0).start()
    for c in range(n):
        if c + 1 < n:
            copy(c + 1).start()
        copy(c).wait()
        w_scr[c * chunk:(c + 1) * chunk, :] = stage[c % 2].astype(BF16)


def _weight_scratch(rows, cols, chunk):
    return [pltpu.VMEM((rows, cols), BF16), pltpu.VMEM((2, chunk, cols), F32), pltpu.SemaphoreType.DMA((2,))]


HBM = pl.BlockSpec(memory_space=pl.ANY)


def _ada_kernel(cv_ref, w_ref, b_ref, o_ref):
    a = _silu(cv_ref[...]).astype(BF16)
    o_ref[...] = jnp.dot(a, w_ref[...].astype(BF16), preferred_element_type=F32) + b_ref[...]


def _ada_call(cv, w_ada, b_ada):
    return pl.pallas_call(
        _ada_kernel,
        out_shape=jax.ShapeDtypeStruct((DEPTH, MOD_ROWS, GATE_COLS), F32),
        grid=(DEPTH, GATE_COLS // ADA_TILE),
        in_specs=[
            pl.BlockSpec((MOD_ROWS, D_MODEL), lambda l, j: (0, 0)),
            pl.BlockSpec((None, D_MODEL, ADA_TILE), lambda l, j: (l, 0, j)),
            pl.BlockSpec((None, 1, ADA_TILE), lambda l, j: (l, 0, j)),
        ],
        out_specs=pl.BlockSpec((None, MOD_ROWS, ADA_TILE), lambda l, j: (l, 0, j)),
        compiler_params=_params("arbitrary", "arbitrary"),
        name="ada_mod",
    )(cv, w_ada, b_ada)


def _normmod_kernel(xc_ref, xl_ref, mod_ref, nw_ref, h_ref, *, ctx_tiles):
    x = _pick(pl.program_id(0) < ctx_tiles, xc_ref, xl_ref)
    shift = mod_ref[:, 0:D_MODEL]
    scale = mod_ref[:, D_MODEL:2 * D_MODEL]
    h_ref[...] = (_rms(x, nw_ref[...]) * (1.0 + scale) + shift).astype(BF16)


def _normmod_call(x_ctx, x_lat, mod3, norm_w, *, seq):
    st = _Stream(x_ctx.shape[0], x_lat.shape[0], seq, IN_TOKEN_TILE)
    return pl.pallas_call(
        functools.partial(_normmod_kernel, ctx_tiles=st.ctx_tiles),
        out_shape=jax.ShapeDtypeStruct((st.tiles * st.tm, D_MODEL), BF16),
        grid=(st.tiles,),
        in_specs=st.two_source_specs(D_MODEL) + [
            pl.BlockSpec((None, 1, GATE_COLS), st.mod_index),
            pl.BlockSpec((1, D_MODEL), lambda i: (0, 0)),
        ],
        out_specs=pl.BlockSpec((st.tm, D_MODEL), lambda i: (i, 0)),
        compiler_params=_params("arbitrary"),
        name="norm_mod",
    )(x_ctx, x_lat, mod3, norm_w)


def _gates_kernel(h_ref, w_hbm, b_ref, o_ref, w_ref, stage, sem, *, layer):
    @pl.when(pl.program_id(0) == 0)
    def _():
        _load_weight_bf16(w_hbm, layer, w_ref, stage, sem)

    for s in range(GATE_COLS // SUB_COLS):
        cs = slice(s * SUB_COLS, (s + 1) * SUB_COLS)
        acc = jnp.dot(h_ref[...], w_ref[:, cs], preferred_element_type=F32)
        o_ref[:, cs] = jax.nn.sigmoid(acc + b_ref[:, cs]).astype(BF16)


def _gates_call(h, w_merge, b_merge, *, layer):
    t = h.shape[0]
    tm = IN_TOKEN_TILE
    return pl.pallas_call(
        functools.partial(_gates_kernel, layer=layer),
        out_shape=jax.ShapeDtypeStruct((t, GATE_COLS), BF16),
        grid=(t // tm,),
        in_specs=[
            pl.BlockSpec((tm, D_MODEL), lambda i: (i, 0)),
            HBM,
            _resident((1, GATE_COLS)),
        ],
        out_specs=pl.BlockSpec((tm, GATE_COLS), lambda i: (i, 0)),
        scratch_shapes=_weight_scratch(D_MODEL, GATE_COLS, CAST_ROWS),
        compiler_params=_params("arbitrary"),
        name="gates",
    )(h, w_merge, b_merge)


def _rope(y, cos, sa, sb):
    return y * cos + pltpu.roll(y, 96, 1) * sa + pltpu.roll(y, 32, 1) * sb


def _proj_kernel(h_ref, w_hbm, qnw_ref, knw_ref, cos_ref, sa_ref, sb_ref, proj_ref, ks_ref, vs_ref,
                 w_ref, stage, sem, *, layer):
    @pl.when(pl.program_id(0) == 0)
    def _():
        _load_weight_bf16(w_hbm, layer, w_ref, stage, sem)

    def head(xh, w):
        return _rope(_rms(xh, w), cos_ref[...], sa_ref[...], sb_ref[...])

    qw = qnw_ref[...] * (ATTN_SCALE * LOG2_E)
    silu_cols = ((GA_TILE0 * COL_TILE, U_TILE * COL_TILE), (GB_TILE * COL_TILE, Z_TILE * COL_TILE),
                 (GC_TILE * COL_TILE, IN_COLS))
    for s in range(IN_COLS // SUB_COLS):
        acc = jnp.dot(h_ref[...], w_ref[:, s * SUB_COLS:(s + 1) * SUB_COLS], preferred_element_type=F32)
        for hb in range(SUB_COLS // HEAD_DIM):
            c0 = s * SUB_COLS + hb * HEAD_DIM
            cs = slice(c0, c0 + HEAD_DIM)
            a = acc[:, hb * HEAD_DIM:(hb + 1) * HEAD_DIM]
            if c0 < K_COL:
                proj_ref[:, cs] = head(a, qw).astype(BF16)
            elif c0 < V_COL:
                ks_ref[:, c0 - K_COL:c0 - K_COL + HEAD_DIM] = _rms(a, knw_ref[...])
                proj_ref[:, cs] = head(a, knw_ref[...]).astype(BF16)
            elif c0 < V_COL + KV_WIDTH:
                vs_ref[:, c0 - V_COL:c0 - V_COL + HEAD_DIM] = a
                proj_ref[:, cs] = a.astype(BF16)
            elif any(lo <= c0 < hi for lo, hi in silu_cols):
                proj_ref[:, cs] = _silu(a).astype(BF16)
            else:
                proj_ref[:, cs] = a.astype(BF16)


def _proj_call(h, w_in, qnw, knw, rope_tabs, *, st, layer):
    t = h.shape[0]
    tm = st.tm

    def tab_index(i):
        return (jnp.where(i < st.ctx_tiles, 0, 1 + (i - st.ctx_tiles) % st.tiles_per_seq), 0)

    in_specs = [
        pl.BlockSpec((tm, D_MODEL), lambda i: (i, 0)),
        HBM,
        pl.BlockSpec((1, HEAD_DIM), lambda i: (0, 0)),
        pl.BlockSpec((1, HEAD_DIM), lambda i: (0, 0)),
    ] + [pl.BlockSpec((tm, HEAD_DIM), tab_index) for _ in rope_tabs]
    out_shape = [jax.ShapeDtypeStruct((t, IN_COLS), BF16),
                 jax.ShapeDtypeStruct((t, KV_WIDTH), F32), jax.ShapeDtypeStruct((t, KV_WIDTH), F32)]
    out_specs = [pl.BlockSpec((tm, IN_COLS), lambda i: (i, 0)),
                 pl.BlockSpec((tm, KV_WIDTH), lambda i: (i, 0)), pl.BlockSpec((tm, KV_WIDTH), lambda i: (i, 0))]
    return pl.pallas_call(
        functools.partial(_proj_kernel, layer=layer),
        out_shape=out_shape,
        grid=(t // tm,),
        in_specs=in_specs,
        out_specs=out_specs,
        scratch_shapes=_weight_scratch(D_MODEL, IN_COLS, CAST_ROWS),
        compiler_params=_params("arbitrary"),
        name="proj",
    )(h, w_in, qnw, knw, *rope_tabs)


def _attn_kernel(*refs, ctx):
    q_ref, k_ref, v_ref, ga0_ref, ga1_ref = refs[:5]
    ga_refs = (ga0_ref, ga1_ref)
    refs = refs[5:]
    if ctx:
        ck_ref, cv_ref = refs[:2]
        refs = refs[2:]
    o_ref, vx_scr = refs[:2]
    if ctx:
        ckx_scr, cvx_scr = refs[2:]

    @pl.when(pl.program_id(1) == 0)
    def _():
        for kh in range(A_KV_HEADS):
            hs = slice(kh * HEAD_DIM, (kh + 1) * HEAD_DIM)
            vx_scr[kh, :, :HEAD_DIM] = v_ref[:, hs]
            vx_scr[kh, :, HEAD_DIM:] = jnp.ones((vx_scr.shape[1], HEAD_DIM), BF16)
            if ctx:
                ckx_scr[kh] = ck_ref[:, kh, :].astype(BF16)
                cvx_scr[kh, :, :HEAD_DIM] = cv_ref[:, kh, :].astype(BF16)
                cvx_scr[kh, :, HEAD_DIM:] = jnp.ones((cvx_scr.shape[1], HEAD_DIM), BF16)

    nt = (((1,), (1,)), ((), ()))
    for head in range(A_HEADS):
        kh, hh = divmod(head, HEADS_PER_TILE)
        cs = slice(head * HEAD_DIM, (head + 1) * HEAD_DIM)
        q = q_ref[:, cs]
        s1 = lax.dot_general(q, k_ref[:, kh * HEAD_DIM:(kh + 1) * HEAD_DIM], nt, preferred_element_type=F32)
        m = jnp.max(s1, axis=-1, keepdims=True)
        if ctx:
            s2 = lax.dot_general(q, ckx_scr[kh], nt, preferred_element_type=F32)
            m = jnp.maximum(m, jnp.max(s2, axis=-1, keepdims=True))
        ox = jnp.dot(jnp.exp2(s1 - m).astype(BF16), vx_scr[kh], preferred_element_type=F32)
        if ctx:
            ox = ox + jnp.dot(jnp.exp2(s2 - m).astype(BF16), cvx_scr[kh], preferred_element_type=F32)
        o = ox[:, :HEAD_DIM] / ox[:, HEAD_DIM:]
        ga = ga_refs[kh][:, hh * HEAD_DIM:(hh + 1) * HEAD_DIM]
        o_ref[:, cs] = (o * ga.astype(F32)).astype(BF16)


def _attn_call(proj, cache, *, row0, n_batch, seq, layer):
    tq = min(ATTN_Q_TILE, seq)
    nq = seq // tq
    q0, s0 = row0 // tq, row0 // seq
    ctx = cache is not None
    in_specs = [
        pl.BlockSpec((tq, A_WIDTH), lambda b, qi: (q0 + b * nq + qi, Q_TILE0 * COL_TILE // A_WIDTH)),
        pl.BlockSpec((seq, KV_WIDTH), lambda b, qi: (s0 + b, K_COL // KV_WIDTH)),
        pl.BlockSpec((seq, KV_WIDTH), lambda b, qi: (s0 + b, V_COL // KV_WIDTH)),
        pl.BlockSpec((tq, COL_TILE), lambda b, qi: (q0 + b * nq + qi, GA_TILE0)),
        pl.BlockSpec((tq, COL_TILE), lambda b, qi: (q0 + b * nq + qi, GA_TILE0 + 1)),
    ]
    args = [proj, proj, proj, proj, proj]
    scratch = [pltpu.VMEM((A_KV_HEADS, seq, 2 * HEAD_DIM), BF16)]
    if ctx:
        past = cache[0].shape[2]
        for c in cache:
            in_specs.append(pl.BlockSpec((None, None, past, A_KV_HEADS, HEAD_DIM),
                                         lambda b, qi: (b, layer, 0, 0, 0)))
            args.append(c)
        scratch += [pltpu.VMEM((A_KV_HEADS, past, HEAD_DIM), BF16),
                    pltpu.VMEM((A_KV_HEADS, past, 2 * HEAD_DIM), BF16)]
    return pl.pallas_call(
        functools.partial(_attn_kernel, ctx=ctx),
        out_shape=jax.ShapeDtypeStruct((n_batch * seq, A_WIDTH), BF16),
        grid=(n_batch, nq),
        in_specs=in_specs,
        out_specs=pl.BlockSpec((tq, A_WIDTH), lambda b, qi: (b * nq + qi, 0)),
        scratch_shapes=scratch,
        compiler_params=_params("arbitrary", "arbitrary"),
        name="attention_lat" if ctx else "attention_ctx",
    )(*args)


def _mix_kernel(u_ref, vb_ref, gb_ref, z_ref, gc_ref, snw_ref, ws_ref, bs_ref, wp_ref, ps_ref, band_ref, inv_ref,
                bo_ref, co_ref, zb_scr):
    r = u_ref.shape[0]
    c = pl.program_id(1)
    nc = pl.num_programs(1)
    base = pl.multiple_of(c * r, r)

    n_chunks = r // CHUNK
    vbn = _rms(vb_ref[...].astype(F32), snw_ref[...]).astype(BF16)
    for g in range(B_GROUPS):
        cs = slice(g * HEAD_DIM, (g + 1) * HEAD_DIM)
        wide = jnp.concatenate([vbn[cc * CHUNK:(cc + 1) * CHUNK, cs] for cc in range(n_chunks)], axis=1)
        mixed = jnp.dot(ws_ref[g], wide, preferred_element_type=F32) + bs_ref[g]
        for cc in range(n_chunks):
            rs = slice(cc * CHUNK, (cc + 1) * CHUNK)
            mx = mixed[:, cc * HEAD_DIM:(cc + 1) * HEAD_DIM]
            bo_ref[rs, cs] = (u_ref[rs, cs].astype(F32) * mx * gb_ref[rs, cs].astype(F32)).astype(BF16)

    zb_scr[POOL_HALO:POOL_HALO + r, :] = z_ref[pl.ds(base, r), :]

    @pl.when(c == 0)
    def _():
        zb_scr[0:POOL_HALO, :] = jnp.zeros((POOL_HALO, C_WIDTH), BF16)

    @pl.when(c > 0)
    def _():
        zb_scr[0:POOL_HALO, :] = z_ref[pl.ds(pl.multiple_of(base - POOL_HALO, POOL_HALO), POOL_HALO), :]

    @pl.when(c == nc - 1)
    def _():
        zb_scr[POOL_HALO + r:, :] = jnp.zeros((POOL_HALO, C_WIDTH), BF16)

    @pl.when(c < nc - 1)
    def _():
        zb_scr[POOL_HALO + r:, :] = z_ref[pl.ds(pl.multiple_of(base + r, POOL_HALO), POOL_HALO), :]

    for g in range(len(POOL_WINDOWS)):
        cs = slice(g * HEAD_DIM, (g + 1) * HEAD_DIM)
        ds = []
        for blk in range(n_chunks):
            lo = blk * CHUNK
            wsum = jnp.dot(band_ref[g], zb_scr[lo:lo + CHUNK + 2 * POOL_HALO, cs], preferred_element_type=F32)
            zc = zb_scr[POOL_HALO + lo:POOL_HALO + lo + CHUNK, cs].astype(F32)
            ds.append((wsum * inv_ref[lo:lo + CHUNK, cs] - zc).astype(BF16))
        dm = jnp.dot(jnp.concatenate(ds, axis=0), wp_ref[g], preferred_element_type=F32)
        co_ref[:, cs] = (dm * ps_ref[:, cs] * gc_ref[:, cs].astype(F32)).astype(BF16)


def _pool_band():
    t = np.arange(CHUNK)[:, None] + POOL_HALO
    j = np.arange(CHUNK + 2 * POOL_HALO)[None, :]
    return jnp.asarray(np.stack([(j >= t - w // 2) & (j < t + w - w // 2) for w in POOL_WINDOWS]), dtype=BF16)


def _pool_inv_count(seq):
    t = np.arange(seq)
    cols = [1.0 / (np.clip(t + w - w // 2, 0, seq) - np.clip(t - w // 2, 0, seq)) for w in POOL_WINDOWS]
    return jnp.asarray(np.repeat(np.stack(cols, axis=1), HEAD_DIM, axis=1), dtype=F32)


def _mix_call(proj, snw, ws, bs, wp, ps, *, row0, n_batch, seq):
    r = min(MIX_TILE, seq)
    nc = seq // r
    r0, s0 = row0 // r, row0 // seq
    tile = lambda col: pl.BlockSpec((r, COL_TILE), lambda b, c: (r0 + b * nc + c, col))
    const3 = lambda shape: pl.BlockSpec(shape, lambda b, c: (0, 0, 0))
    out_spec = pl.BlockSpec((r, COL_TILE), lambda b, c: (b * nc + c, 0))
    return pl.pallas_call(
        _mix_kernel,
        out_shape=[jax.ShapeDtypeStruct((n_batch * seq, B_WIDTH), BF16),
                   jax.ShapeDtypeStruct((n_batch * seq, C_WIDTH), BF16)],
        grid=(n_batch, nc),
        in_specs=[
            tile(U_TILE), tile(VB_TILE), tile(GB_TILE),
            pl.BlockSpec((seq, COL_TILE), lambda b, c: (s0 + b, Z_TILE)),
            tile(GC_TILE),
            pl.BlockSpec((1, B_WIDTH), lambda b, c: (0, 0)),
            const3((B_GROUPS, CHUNK, CHUNK)),
            const3((B_GROUPS, CHUNK, 1)),
            const3((B_GROUPS, HEAD_DIM, HEAD_DIM)),
            pl.BlockSpec((1, C_WIDTH), lambda b, c: (0, 0)),
            const3((len(POOL_WINDOWS), CHUNK, CHUNK + 2 * POOL_HALO)),
            pl.BlockSpec((r, C_WIDTH), lambda b, c: (c, 0)),
        ],
        out_specs=[out_spec, out_spec],
        scratch_shapes=[pltpu.VMEM((r + 2 * POOL_HALO, C_WIDTH), BF16)],
        compiler_params=_params("arbitrary", "arbitrary"),
        name="sgu_pool",
    )(proj, proj, proj, proj, proj, snw, ws, bs, wp, ps, _pool_band(), _pool_inv_count(seq))


def _outproj_kernel(*refs, first, final, ctx_tiles, n_tiles, layer):
    (ac_ref, al_ref, bc_ref, bl_ref, cc_ref, cl_ref, g0_ref, g1_ref, g2_ref) = refs[:9]
    refs = refs[9:]
    if first:
        xc_ref, xl_ref = refs[:2]
        refs = refs[2:]
    else:
        x_ref = refs[0]
        refs = refs[1:]
    mod_ref, wa_hbm, wb_hbm, wc_hbm, wo_hbm, nw_ref = refs[:6]
    refs = refs[6:]
    if final:
        yc_ref, yl_ref = refs[:2]
        refs = refs[2:]
    else:
        modn_ref, y_ref, hn_ref = refs[:3]
        refs = refs[3:]
    a_scr, b_scr, c_scr, m_scr, y_scr, ssq_scr, wa_ref, wb_ref, wc_ref, wo_ref, stage, sem = refs
    i = pl.program_id(0)

    @pl.when(i == 0)
    def _():
        for w_hbm, w_ref in ((wa_hbm, wa_ref), (wb_hbm, wb_ref), (wc_hbm, wc_ref), (wo_hbm, wo_ref)):
            _load_weight_bf16(w_hbm, layer, w_ref, stage, sem)
        y_scr[...] = jnp.zeros(y_scr.shape, F32)
        ssq_scr[...] = jnp.zeros(ssq_scr.shape, F32)
        if final:
            yc_ref[...] = jnp.zeros(yc_ref.shape, F32)
            yl_ref[...] = jnp.zeros(yl_ref.shape, F32)

    yn = y_scr[...] * lax.rsqrt(ssq_scr[...] * (1.0 / D_MODEL) + EPS) * nw_ref[...]
    if final:
        prev_is_ctx = i - 1 < ctx_tiles
        yc_ref[...] = jnp.where(prev_is_ctx, yn, yc_ref[...])
        yl_ref[...] = jnp.where(prev_is_ctx, yl_ref[...], yn)
    else:
        hn_ref[...] = (yn * (1.0 + modn_ref[:, D_MODEL:2 * D_MODEL]) + modn_ref[:, 0:D_MODEL]).astype(BF16)

    is_ctx = jnp.minimum(i, n_tiles - 1) < ctx_tiles
    a_scr[...] = _pick(is_ctx, ac_ref, al_ref)
    b_scr[...] = _pick(is_ctx, bc_ref, bl_ref)
    c_scr[...] = _pick(is_ctx, cc_ref, cl_ref)
    for n in range(D_MODEL // COL_TILE):
        cs = slice(n * COL_TILE, (n + 1) * COL_TILE)
        a = jnp.dot(a_scr[...], wa_ref[:, cs], preferred_element_type=F32)
        b = jnp.dot(b_scr[...], wb_ref[:, cs], preferred_element_type=F32)
        c = jnp.dot(c_scr[...], wc_ref[:, cs], preferred_element_type=F32)
        m = (g0_ref[:, cs].astype(F32) * a + g1_ref[:, cs].astype(F32) * b
             + g2_ref[:, cs].astype(F32) * c)
        m_scr[:, cs] = m.astype(BF16)
    ssq = jnp.zeros(ssq_scr.shape, F32)
    for n in range(D_MODEL // COL_TILE):
        cs = slice(n * COL_TILE, (n + 1) * COL_TILE)
        out = jnp.dot(m_scr[...], wo_ref[:, cs], preferred_element_type=F32)
        x = jnp.where(is_ctx, xc_ref[:, cs], xl_ref[:, cs]) if first else x_ref[:, cs]
        y = x + mod_ref[:, 2 * D_MODEL + n * COL_TILE:2 * D_MODEL + (n + 1) * COL_TILE] * out
        if not final:
            y_ref[:, cs] = y
        y_scr[:, cs] = y
        ssq = ssq + jnp.sum(y * y, axis=-1, keepdims=True)
    ssq_scr[...] = ssq


def _outproj_call(attn, bout, cout, gates, x, mod3, wa, wb, wc, wo, nw, modn3, *, st, layer):
    tm = st.tm
    first = isinstance(x, tuple)
    final = modn3 is None
    cur = lambda i: jnp.minimum(i, st.tiles - 1)
    prev = lambda i: jnp.maximum(i - 1, 0)
    gate = lambda g: pl.BlockSpec((tm, D_MODEL), lambda i: (cur(i), g))
    row = lambda of: pl.BlockSpec((tm, D_MODEL), lambda i: (of(i), 0))
    in_specs = (st.two_source_specs(A_WIDTH, cur) + st.two_source_specs(B_WIDTH, cur)
                + st.two_source_specs(C_WIDTH, cur)
                + [gate(0), gate(1), gate(2)]
                + (st.two_source_specs(D_MODEL, cur) if first else [row(cur)])
                + [pl.BlockSpec((None, 1, GATE_COLS), lambda i: st.mod_index(cur(i))),
                   HBM, HBM, HBM, HBM,
                   pl.BlockSpec((1, D_MODEL), lambda i: (0, 0))])
    args = [*attn, *bout, *cout, gates, gates, gates, *(x if first else (x,)), mod3, wa, wb, wc, wo, nw]
    scratch = [pltpu.VMEM((tm, A_WIDTH), BF16), pltpu.VMEM((tm, B_WIDTH), BF16),
               pltpu.VMEM((tm, C_WIDTH), BF16), pltpu.VMEM((tm, D_MODEL), BF16),
               pltpu.VMEM((tm, D_MODEL), F32), pltpu.VMEM((tm, 1), F32),
               pltpu.VMEM((A_WIDTH, D_MODEL), BF16), pltpu.VMEM((B_WIDTH, D_MODEL), BF16),
               pltpu.VMEM((C_WIDTH, D_MODEL), BF16), pltpu.VMEM((D_MODEL, D_MODEL), BF16),
               pltpu.VMEM((2, OUT_CAST_ROWS, D_MODEL), F32), pltpu.SemaphoreType.DMA((2,))]
    if final:
        out_shape = [jax.ShapeDtypeStruct((st.ctx_tiles * tm, D_MODEL), F32),
                     jax.ShapeDtypeStruct(((st.tiles - st.ctx_tiles) * tm, D_MODEL), F32)]
        out_specs = st.two_source_specs(D_MODEL, prev)
    else:
        in_specs.append(pl.BlockSpec((None, 1, GATE_COLS), lambda i: st.mod_index(prev(i))))
        args.append(modn3)
        out_shape = [jax.ShapeDtypeStruct((st.tiles * tm, D_MODEL), F32),
                     jax.ShapeDtypeStruct((st.tiles * tm, D_MODEL), BF16)]
        out_specs = [row(cur), row(prev)]
    return pl.pallas_call(
        functools.partial(_outproj_kernel, first=first, final=final, ctx_tiles=st.ctx_tiles,
                          n_tiles=st.tiles, layer=layer),
        out_shape=out_shape,
        grid=(st.tiles + 1,),
        in_specs=in_specs,
        out_specs=out_specs,
        scratch_shapes=scratch,
        compiler_params=_params("arbitrary"),
        name="out_proj_final" if final else "out_proj",
    )(*args)


def _rope_tables(n_identity, n_tokens):
    rows = n_tokens // GRID_W
    row = np.repeat(np.arange(rows), GRID_W).astype(np.float64)
    col = np.tile(np.arange(GRID_W), rows).astype(np.float64)
    n_freq = HEAD_DIM // 4
    inv = ROPE_THETA ** (-np.arange(n_freq, dtype=np.float64) / n_freq)
    ar = row[:, None] * inv[None, :]
    ac = col[:, None] * inv[None, :]
    zero = np.zeros_like(ar)
    cos = np.concatenate([np.cos(ar), np.cos(ar), np.cos(ac), np.cos(ac)], axis=-1)
    sa = np.concatenate([-np.sin(ar), zero, -np.sin(ac), zero], axis=-1)
    sb = np.concatenate([zero, np.sin(ar), zero, np.sin(ac)], axis=-1)
    ident = (np.ones((n_identity, HEAD_DIM)), np.zeros((n_identity, HEAD_DIM)), np.zeros((n_identity, HEAD_DIM)))
    return tuple(jnp.asarray(np.concatenate([i, t], axis=0), dtype=F32) for i, t in zip(ident, (cos, sa, sb)))


def kernel(x_prompt, x_sample, cache_k, cache_v, c, c_ctx, norm_w, w_ada, b_ada, w_in, q_norm_w,
           k_norm_w, sgu_norm_w, w_sgu, b_sgu, w_pool, pool_scale, w_br_a, w_br_b, w_br_c, w_merge,
           b_merge, w_out, final_norm_w):
    nb_p, seq_p, d = x_prompt.shape
    nb_s, seq_s, _ = x_sample.shape
    n_ctx, n_lat = nb_p * seq_p, nb_s * seq_s
    assert d == D_MODEL and nb_s + 1 <= MOD_ROWS

    cv = jnp.concatenate([c_ctx[None, :], c, jnp.zeros((MOD_ROWS - 1 - nb_s, d), F32)], axis=0)
    mod = _ada_call(cv, w_ada, b_ada.reshape(DEPTH, 1, GATE_COLS))
    mod3 = [mod[l].reshape(MOD_ROWS, 1, GATE_COLS) for l in range(DEPTH)]
    st_in = _Stream(n_ctx, n_lat, seq_s, IN_TOKEN_TILE)
    st_out = _Stream(n_ctx, n_lat, seq_s, OUT_TOKEN_TILE)
    rope_tabs = _rope_tables(IN_TOKEN_TILE, seq_s)
    cache = (cache_k, cache_v)

    x = (x_prompt.reshape(n_ctx, d), x_sample.reshape(n_lat, d))
    h = _normmod_call(*x, mod3[0], norm_w[0].reshape(1, d), seq=seq_s)
    states = []
    w_mg, w_i, wa, wb, wc, wo = w_merge, w_in, w_br_a, w_br_b, w_br_c, w_out
    for l in range(DEPTH):
        last = l == DEPTH - 1
        bm = b_merge[l].reshape(1, GATE_COLS)
        qnw = q_norm_w[l].reshape(1, HEAD_DIM)
        knw = k_norm_w[l].reshape(1, HEAD_DIM)
        mix_w = (sgu_norm_w[l].reshape(1, B_WIDTH), w_sgu[l].astype(BF16),
                 b_sgu[l].reshape(B_GROUPS, CHUNK, 1), w_pool[l].astype(BF16),
                 pool_scale[l].reshape(1, C_WIDTH))
        nw_next = (final_norm_w if last else norm_w[l + 1]).reshape(1, d)

        gates = _gates_call(h, w_mg, bm, layer=l)
        proj, ks, vs = _proj_call(h, w_i, qnw, knw, rope_tabs, st=st_in, layer=l)
        states.append((ks, vs))
        attn = (_attn_call(proj, None, row0=0, n_batch=nb_p, seq=seq_p, layer=l),
                _attn_call(proj, cache, row0=n_ctx, n_batch=nb_s, seq=seq_s, layer=l))
        mix_c = _mix_call(proj, *mix_w, row0=0, n_batch=nb_p, seq=seq_p)
        mix_l = _mix_call(proj, *mix_w, row0=n_ctx, n_batch=nb_s, seq=seq_s)
        res = _outproj_call(attn, (mix_c[0], mix_l[0]), (mix_c[1], mix_l[1]), gates, x, mod3[l],
                            wa, wb, wc, wo, nw_next, None if last else mod3[l + 1], st=st_out, layer=l)
        if last:
            y_ctx, y_lat = res
        else:
            x, h = res

    state_k, state_v = (
        jnp.stack([s[i][:n_ctx].reshape(nb_p, seq_p, A_KV_HEADS, HEAD_DIM) for s in states], axis=1)
        for i in range(2))
    return (y_ctx.reshape(nb_p, seq_p, d), y_lat.reshape(nb_s, seq_s, d), state_k, state_v)
```

```python
import functools

import jax
import jax.numpy as jnp
import numpy as np
from jax import lax
from jax.experimental import pallas as pl
from jax.experimental.pallas import tpu as pltpu

F32 = jnp.float32
BF16 = jnp.bfloat16

D_MODEL = 2048
DEPTH = 2
GRID_W = 64
EPS = 1e-6
HEAD_DIM = 128
A_HEADS = 8
A_KV_HEADS = 2
A_WIDTH = A_HEADS * HEAD_DIM
KV_WIDTH = A_KV_HEADS * HEAD_DIM
ROPE_THETA = 10000.0
ATTN_SCALE = HEAD_DIM ** -0.5
LOG2_E = 1.4426950408889634
CHUNK = 128
B_GROUPS = 4
B_WIDTH = 512
C_WIDTH = 512
POOL_WINDOWS = (2, 4, 8, 16)
POOL_HALO = 64
N_BRANCH = 3
GATE_COLS = N_BRANCH * D_MODEL
IN_COLS = 2 * A_WIDTH + 2 * KV_WIDTH + 3 * B_WIDTH + 2 * C_WIDTH

COL_TILE = 512
Q_TILE0 = 0
KV_TILE = Q_TILE0 + A_WIDTH // COL_TILE
GA_TILE0 = KV_TILE + 1
U_TILE = GA_TILE0 + A_WIDTH // COL_TILE
VB_TILE = U_TILE + 1
GB_TILE = VB_TILE + 1
Z_TILE = GB_TILE + 1
GC_TILE = Z_TILE + 1
HEADS_PER_TILE = COL_TILE // HEAD_DIM
K_COL = KV_TILE * COL_TILE
V_COL = K_COL + KV_WIDTH

SUB_COLS = 256
MOD_ROWS = 8
ADA_TILE = 1024
IN_TOKEN_TILE = 512
NORM_TOKEN_TILE = 1024
NORM_ROWS = 16
OUT_TOKEN_TILE = 256
ATTN_Q_TILE = 512
MIX_TILE = 512
CAST_ROWS = 128
OUT_CAST_ROWS = 256
VMEM_LIMIT = 56 * 1024 * 1024


def _params(*sem):
    return pltpu.CompilerParams(dimension_semantics=sem, vmem_limit_bytes=VMEM_LIMIT)


def _resident(shape):
    return pl.BlockSpec(shape, lambda *_: (0,) * len(shape), pipeline_mode=pl.Buffered(1))


def _silu(x):
    return x * jax.nn.sigmoid(x)


def _rms(x, w):
    ms = jnp.mean(x * x, axis=-1, keepdims=True)
    return x * lax.rsqrt(ms + EPS) * w


class _Stream:
    def __init__(self, n_ctx, n_lat, seq, tm):
        assert n_ctx % tm == 0 and seq % tm == 0
        self.tm = tm
        self.ctx_tiles = n_ctx // tm
        self.tiles = (n_ctx + n_lat) // tm
        self.tiles_per_seq = seq // tm

    def mod_index(self, i):
        lat = 1 + (i - self.ctx_tiles) // self.tiles_per_seq
        return (jnp.where(i < self.ctx_tiles, 0, lat), 0, 0)

    def two_source_specs(self, width):
        ctx = pl.BlockSpec((self.tm, width), lambda i: (jnp.minimum(i, self.ctx_tiles - 1), 0))
        lat = pl.BlockSpec((self.tm, width), lambda i: (jnp.maximum(i - self.ctx_tiles, 0), 0))
        return [ctx, lat]


def _pick(is_ctx, ctx_ref, lat_ref):
    return jnp.where(is_ctx, ctx_ref[...], lat_ref[...])


def _load_weight_bf16(w_hbm, layer, w_scr, stage, sem):
    chunk = stage.shape[1]
    n = w_scr.shape[0] // chunk

    def copy(c):
        return pltpu.make_async_copy(w_hbm.at[layer, pl.ds(c * chunk, chunk), :], stage.at[c % 2], sem.at[c % 2])

    copy(0).start()
    for c in range(n):
        if c + 1 < n:
            copy(c + 1).start()
        copy(c).wait()
        w_scr[c * chunk:(c + 1) * chunk, :] = stage[c % 2].astype(BF16)


def _weight_scratch(rows, cols, chunk):
    return [pltpu.VMEM((rows, cols), BF16), pltpu.VMEM((2, chunk, cols), F32), pltpu.SemaphoreType.DMA((2,))]


HBM = pl.BlockSpec(memory_space=pl.ANY)


def _ada_kernel(cv_ref, w_ref, b_ref, o_ref):
    a = _silu(cv_ref[...]).astype(BF16)
    o_ref[...] = jnp.dot(a, w_ref[...].astype(BF16), preferred_element_type=F32) + b_ref[...]


def _ada_call(cv, w_ada, b_ada):
    return pl.pallas_call(
        _ada_kernel,
        out_shape=jax.ShapeDtypeStruct((DEPTH, MOD_ROWS, GATE_COLS), F32),
        grid=(DEPTH, GATE_COLS // ADA_TILE),
        in_specs=[
            pl.BlockSpec((MOD_ROWS, D_MODEL), lambda l, j: (0, 0)),
            pl.BlockSpec((None, D_MODEL, ADA_TILE), lambda l, j: (l, 0, j)),
            pl.BlockSpec((None, 1, ADA_TILE), lambda l, j: (l, 0, j)),
        ],
        out_specs=pl.BlockSpec((None, MOD_ROWS, ADA_TILE), lambda l, j: (l, 0, j)),
        compiler_params=_params("arbitrary", "arbitrary"),
        name="ada_mod",
    )(cv, w_ada, b_ada)


def _normmod_kernel(xc_ref, xl_ref, mod_ref, nw_ref, h_ref, *, ctx_tiles):
    def emit(x_ref):
        def chunk(r, carry):
            rows = pl.ds(pl.multiple_of(r * NORM_ROWS, NORM_ROWS), NORM_ROWS)
            y = _rms(x_ref[rows, :], nw_ref[...])
            h_ref[rows, :] = (y * (1.0 + mod_ref[:, D_MODEL:2 * D_MODEL]) + mod_ref[:, 0:D_MODEL]).astype(BF16)
            return carry

        lax.fori_loop(0, h_ref.shape[0] // NORM_ROWS, chunk, 0, unroll=8)

    @pl.when(pl.program_id(0) < ctx_tiles)
    def _():
        emit(xc_ref)

    @pl.when(pl.program_id(0) >= ctx_tiles)
    def _():
        emit(xl_ref)


def _normmod_call(x_ctx, x_lat, mod3, norm_w, *, seq):
    st = _Stream(x_ctx.shape[0], x_lat.shape[0], seq, NORM_TOKEN_TILE)
    return pl.pallas_call(
        functools.partial(_normmod_kernel, ctx_tiles=st.ctx_tiles),
        out_shape=jax.ShapeDtypeStruct((st.tiles * st.tm, D_MODEL), BF16),
        grid=(st.tiles,),
        in_specs=st.two_source_specs(D_MODEL) + [
            pl.BlockSpec((None, 1, GATE_COLS), st.mod_index),
            pl.BlockSpec((1, D_MODEL), lambda i: (0, 0)),
        ],
        out_specs=pl.BlockSpec((st.tm, D_MODEL), lambda i: (i, 0)),
        compiler_params=_params("arbitrary"),
        name="norm_mod",
    )(x_ctx, x_lat, mod3, norm_w)


def _gates_kernel(h_ref, w_hbm, b_ref, o_ref, w_ref, stage, sem, *, layer):
    @pl.when(pl.program_id(0) == 0)
    def _():
        _load_weight_bf16(w_hbm, layer, w_ref, stage, sem)

    for s in range(GATE_COLS // SUB_COLS):
        cs = slice(s * SUB_COLS, (s + 1) * SUB_COLS)
        acc = jnp.dot(h_ref[...], w_ref[:, cs], preferred_element_type=F32)
        o_ref[:, cs] = jax.nn.sigmoid(acc + b_ref[:, cs]).astype(BF16)


def _gates_call(h, w_merge, b_merge, *, layer):
    t = h.shape[0]
    tm = IN_TOKEN_TILE
    return pl.pallas_call(
        functools.partial(_gates_kernel, layer=layer),
        out_shape=jax.ShapeDtypeStruct((t, GATE_COLS), BF16),
        grid=(t // tm,),
        in_specs=[
            pl.BlockSpec((tm, D_MODEL), lambda i: (i, 0)),
            HBM,
            _resident((1, GATE_COLS)),
        ],
        out_specs=pl.BlockSpec((tm, GATE_COLS), lambda i: (i, 0)),
        scratch_shapes=_weight_scratch(D_MODEL, GATE_COLS, CAST_ROWS),
        compiler_params=_params("arbitrary"),
        name="gates",
    )(h, w_merge, b_merge)


def _rope(y, cos, sa, sb):
    return y * cos + pltpu.roll(y, 96, 1) * sa + pltpu.roll(y, 32, 1) * sb


def _proj_kernel(h_ref, w_hbm, qnw_ref, knw_ref, cos_ref, sa_ref, sb_ref, proj_ref, ks_ref, vs_ref,
                 w_ref, stage, sem, *, layer):
    @pl.when(pl.program_id(0) == 0)
    def _():
        _load_weight_bf16(w_hbm, layer, w_ref, stage, sem)

    def head(xh, w):
        return _rope(_rms(xh, w), cos_ref[...], sa_ref[...], sb_ref[...])

    qw = qnw_ref[...] * (ATTN_SCALE * LOG2_E)
    silu_cols = ((GA_TILE0 * COL_TILE, U_TILE * COL_TILE), (GB_TILE * COL_TILE, Z_TILE * COL_TILE),
                 (GC_TILE * COL_TILE, IN_COLS))
    for s in range(IN_COLS // SUB_COLS):
        acc = jnp.dot(h_ref[...], w_ref[:, s * SUB_COLS:(s + 1) * SUB_COLS], preferred_element_type=F32)
        for hb in range(SUB_COLS // HEAD_DIM):
            c0 = s * SUB_COLS + hb * HEAD_DIM
            cs = slice(c0, c0 + HEAD_DIM)
            a = acc[:, hb * HEAD_DIM:(hb + 1) * HEAD_DIM]
            if c0 < K_COL:
                proj_ref[:, cs] = head(a, qw).astype(BF16)
            elif c0 < V_COL:
                ks_ref[:, c0 - K_COL:c0 - K_COL + HEAD_DIM] = _rms(a, knw_ref[...])
                proj_ref[:, cs] = head(a, knw_ref[...]).astype(BF16)
            elif c0 < V_COL + KV_WIDTH:
                vs_ref[:, c0 - V_COL:c0 - V_COL + HEAD_DIM] = a
                proj_ref[:, cs] = a.astype(BF16)
            elif any(lo <= c0 < hi for lo, hi in silu_cols):
                proj_ref[:, cs] = _silu(a).astype(BF16)
            else:
                proj_ref[:, cs] = a.astype(BF16)


def _proj_call(h, w_in, qnw, knw, rope_tabs, *, st, layer):
    t = h.shape[0]
    tm = st.tm

    def tab_index(i):
        return (jnp.where(i < st.ctx_tiles, 0, 1 + (i - st.ctx_tiles) % st.tiles_per_seq), 0)

    in_specs = [
        pl.BlockSpec((tm, D_MODEL), lambda i: (i, 0)),
        HBM,
        pl.BlockSpec((1, HEAD_DIM), lambda i: (0, 0)),
        pl.BlockSpec((1, HEAD_DIM), lambda i: (0, 0)),
    ] + [pl.BlockSpec((tm, HEAD_DIM), tab_index) for _ in rope_tabs]
    out_shape = [jax.ShapeDtypeStruct((t, IN_COLS), BF16),
                 jax.ShapeDtypeStruct((t, KV_WIDTH), F32), jax.ShapeDtypeStruct((t, KV_WIDTH), F32)]
    out_specs = [pl.BlockSpec((tm, IN_COLS), lambda i: (i, 0)),
                 pl.BlockSpec((tm, KV_WIDTH), lambda i: (i, 0)), pl.BlockSpec((tm, KV_WIDTH), lambda i: (i, 0))]
    return pl.pallas_call(
        functools.partial(_proj_kernel, layer=layer),
        out_shape=out_shape,
        grid=(t // tm,),
        in_specs=in_specs,
        out_specs=out_specs,
        scratch_shapes=_weight_scratch(D_MODEL, IN_COLS, CAST_ROWS),
        compiler_params=_params("arbitrary"),
        name="proj",
    )(h, w_in, qnw, knw, *rope_tabs)


def _attn_kernel(*refs, ctx, n_seq):
    q_ref, k_ref, v_ref, ga0_ref, ga1_ref = refs[:5]
    ga_refs = (ga0_ref, ga1_ref)
    refs = refs[5:]
    if ctx:
        ck_ref, cv_ref = refs[:2]
        refs = refs[2:]
    o_ref, vx_scr = refs[:2]
    if ctx:
        ckx_scr, cvx_scr = refs[2:]
    seq = k_ref.shape[0] // n_seq
    tq = q_ref.shape[0] // n_seq

    @pl.when(pl.program_id(1) == 0)
    def _():
        for j in range(n_seq):
            for kh in range(A_KV_HEADS):
                hs = slice(kh * HEAD_DIM, (kh + 1) * HEAD_DIM)
                vx_scr[j, kh, :, :HEAD_DIM] = v_ref[j * seq:(j + 1) * seq, hs]
                vx_scr[j, kh, :, HEAD_DIM:] = jnp.ones((seq, HEAD_DIM), BF16)
        if ctx:
            for kh in range(A_KV_HEADS):
                ckx_scr[kh] = ck_ref[:, kh, :].astype(BF16)
                cvx_scr[kh, :, :HEAD_DIM] = cv_ref[:, kh, :].astype(BF16)
                cvx_scr[kh, :, HEAD_DIM:] = jnp.ones((cvx_scr.shape[1], HEAD_DIM), BF16)

    nt = (((1,), (1,)), ((), ()))
    for j in range(n_seq):
        qrows = slice(j * tq, (j + 1) * tq)
        for head in range(A_HEADS):
            kh, hh = divmod(head, HEADS_PER_TILE)
            cs = slice(head * HEAD_DIM, (head + 1) * HEAD_DIM)
            q = q_ref[qrows, cs]
            k = k_ref[j * seq:(j + 1) * seq, kh * HEAD_DIM:(kh + 1) * HEAD_DIM]
            s1 = lax.dot_general(q, k, nt, preferred_element_type=F32)
            m = jnp.max(s1, axis=-1, keepdims=True)
            if ctx:
                s2 = lax.dot_general(q, ckx_scr[kh], nt, preferred_element_type=F32)
                m = jnp.maximum(m, jnp.max(s2, axis=-1, keepdims=True))
            ox = jnp.dot(jnp.exp2(s1 - m).astype(BF16), vx_scr[j, kh], preferred_element_type=F32)
            if ctx:
                ox = ox + jnp.dot(jnp.exp2(s2 - m).astype(BF16), cvx_scr[kh], preferred_element_type=F32)
            o = ox[:, :HEAD_DIM] / ox[:, HEAD_DIM:]
            ga = ga_refs[kh][qrows, hh * HEAD_DIM:(hh + 1) * HEAD_DIM]
            o_ref[qrows, cs] = (o * ga.astype(F32)).astype(BF16)


def _attn_call(proj, cache, *, row0, n_batch, seq, layer):
    tq = min(ATTN_Q_TILE, seq)
    nq = seq // tq
    ctx = cache is not None
    n_seq = ATTN_Q_TILE // seq if (nq == 1 and not ctx) else 1
    assert n_batch % n_seq == 0
    qb, sb = tq * n_seq, seq * n_seq
    q0, s0 = row0 // qb, row0 // sb
    in_specs = [
        pl.BlockSpec((qb, A_WIDTH), lambda b, qi: (q0 + b * nq + qi, Q_TILE0 * COL_TILE // A_WIDTH)),
        pl.BlockSpec((sb, KV_WIDTH), lambda b, qi: (s0 + b, K_COL // KV_WIDTH)),
        pl.BlockSpec((sb, KV_WIDTH), lambda b, qi: (s0 + b, V_COL // KV_WIDTH)),
        pl.BlockSpec((qb, COL_TILE), lambda b, qi: (q0 + b * nq + qi, GA_TILE0)),
        pl.BlockSpec((qb, COL_TILE), lambda b, qi: (q0 + b * nq + qi, GA_TILE0 + 1)),
    ]
    args = [proj, proj, proj, proj, proj]
    scratch = [pltpu.VMEM((n_seq, A_KV_HEADS, seq, 2 * HEAD_DIM), BF16)]
    if ctx:
        past = cache[0].shape[2]
        for c in cache:
            in_specs.append(pl.BlockSpec((None, None, past, A_KV_HEADS, HEAD_DIM),
                                         lambda b, qi: (b, layer, 0, 0, 0)))
            args.append(c)
        scratch += [pltpu.VMEM((A_KV_HEADS, past, HEAD_DIM), BF16),
                    pltpu.VMEM((A_KV_HEADS, past, 2 * HEAD_DIM), BF16)]
    return pl.pallas_call(
        functools.partial(_attn_kernel, ctx=ctx, n_seq=n_seq),
        out_shape=jax.ShapeDtypeStruct((n_batch * seq, A_WIDTH), BF16),
        grid=(n_batch // n_seq, nq),
        in_specs=in_specs,
        out_specs=pl.BlockSpec((qb, A_WIDTH), lambda b, qi: (b * nq + qi, 0)),
        scratch_shapes=scratch,
        compiler_params=_params("arbitrary", "arbitrary"),
        name="attention_lat" if ctx else "attention_ctx",
    )(*args)


def _mix_kernel(u_ref, vb_ref, gb_ref, z_ref, gc_ref, snw_ref, ws_ref, bs_ref, wp_ref, ps_ref, band_ref, inv_ref,
                bo_ref, co_ref, zb_scr):
    r = u_ref.shape[0]
    c = pl.program_id(1)
    nc = pl.num_programs(1)
    base = pl.multiple_of(c * r, r)

    n_chunks = r // CHUNK
    vbn = _rms(vb_ref[...].astype(F32), snw_ref[...]).astype(BF16)
    for g in range(B_GROUPS):
        cs = slice(g * HEAD_DIM, (g + 1) * HEAD_DIM)
        wide = jnp.concatenate([vbn[cc * CHUNK:(cc + 1) * CHUNK, cs] for cc in range(n_chunks)], axis=1)
        mixed = jnp.dot(ws_ref[g], wide, preferred_element_type=F32) + bs_ref[g]
        for cc in range(n_chunks):
            rs = slice(cc * CHUNK, (cc + 1) * CHUNK)
            mx = mixed[:, cc * HEAD_DIM:(cc + 1) * HEAD_DIM]
            bo_ref[rs, cs] = (u_ref[rs, cs].astype(F32) * mx * gb_ref[rs, cs].astype(F32)).astype(BF16)

    zb_scr[POOL_HALO:POOL_HALO + r, :] = z_ref[pl.ds(base, r), :]

    @pl.when(c == 0)
    def _():
        zb_scr[0:POOL_HALO, :] = jnp.zeros((POOL_HALO, C_WIDTH), BF16)

    @pl.when(c > 0)
    def _():
        zb_scr[0:POOL_HALO, :] = z_ref[pl.ds(pl.multiple_of(base - POOL_HALO, POOL_HALO), POOL_HALO), :]

    @pl.when(c == nc - 1)
    def _():
        zb_scr[POOL_HALO + r:, :] = jnp.zeros((POOL_HALO, C_WIDTH), BF16)

    @pl.when(c < nc - 1)
    def _():
        zb_scr[POOL_HALO + r:, :] = z_ref[pl.ds(pl.multiple_of(base + r, POOL_HALO), POOL_HALO), :]

    for g in range(len(POOL_WINDOWS)):
        cs = slice(g * HEAD_DIM, (g + 1) * HEAD_DIM)
        ds = []
        for blk in range(n_chunks):
            lo = blk * CHUNK
            wsum = jnp.dot(band_ref[g], zb_scr[lo:lo + CHUNK + 2 * POOL_HALO, cs], preferred_element_type=F32)
            zc = zb_scr[POOL_HALO + lo:POOL_HALO + lo + CHUNK, cs].astype(F32)
            ds.append((wsum * inv_ref[lo:lo + CHUNK, cs] - zc).astype(BF16))
        dm = jnp.dot(jnp.concatenate(ds, axis=0), wp_ref[g], preferred_element_type=F32)
        co_ref[:, cs] = (dm * ps_ref[:, cs] * gc_ref[:, cs].astype(F32)).astype(BF16)


def _pool_band():
    t = np.arange(CHUNK)[:, None] + POOL_HALO
    j = np.arange(CHUNK + 2 * POOL_HALO)[None, :]
    return jnp.asarray(np.stack([(j >= t - w // 2) & (j < t + w - w // 2) for w in POOL_WINDOWS]), dtype=BF16)


def _pool_inv_count(seq):
    t = np.arange(seq)
    cols = [1.0 / (np.clip(t + w - w // 2, 0, seq) - np.clip(t - w // 2, 0, seq)) for w in POOL_WINDOWS]
    return jnp.asarray(np.repeat(np.stack(cols, axis=1), HEAD_DIM, axis=1), dtype=F32)


def _mix_call(proj, snw, ws, bs, wp, ps, *, row0, n_batch, seq):
    r = min(MIX_TILE, seq)
    nc = seq // r
    r0, s0 = row0 // r, row0 // seq
    tile = lambda col: pl.BlockSpec((r, COL_TILE), lambda b, c: (r0 + b * nc + c, col))
    const3 = lambda shape: pl.BlockSpec(shape, lambda b, c: (0, 0, 0))
    out_spec = pl.BlockSpec((r, COL_TILE), lambda b, c: (b * nc + c, 0))
    return pl.pallas_call(
        _mix_kernel,
        out_shape=[jax.ShapeDtypeStruct((n_batch * seq, B_WIDTH), BF16),
                   jax.ShapeDtypeStruct((n_batch * seq, C_WIDTH), BF16)],
        grid=(n_batch, nc),
        in_specs=[
            tile(U_TILE), tile(VB_TILE), tile(GB_TILE),
            pl.BlockSpec((seq, COL_TILE), lambda b, c: (s0 + b, Z_TILE)),
            tile(GC_TILE),
            pl.BlockSpec((1, B_WIDTH), lambda b, c: (0, 0)),
            const3((B_GROUPS, CHUNK, CHUNK)),
            const3((B_GROUPS, CHUNK, 1)),
            const3((B_GROUPS, HEAD_DIM, HEAD_DIM)),
            pl.BlockSpec((1, C_WIDTH), lambda b, c: (0, 0)),
            const3((len(POOL_WINDOWS), CHUNK, CHUNK + 2 * POOL_HALO)),
            pl.BlockSpec((r, C_WIDTH), lambda b, c: (c, 0)),
        ],
        out_specs=[out_spec, out_spec],
        scratch_shapes=[pltpu.VMEM((r + 2 * POOL_HALO, C_WIDTH), BF16)],
        compiler_params=_params("arbitrary", "arbitrary"),
        name="sgu_pool",
    )(proj, proj, proj, proj, proj, snw, ws, bs, wp, ps, _pool_band(), _pool_inv_count(seq))


def _outproj_kernel(*refs, first, final, ctx_tiles, layer):
    (ac_ref, al_ref, bc_ref, bl_ref, cc_ref, cl_ref, g0_ref, g1_ref, g2_ref) = refs[:9]
    refs = refs[9:]
    if first:
        xc_ref, xl_ref = refs[:2]
        refs = refs[2:]
    else:
        x_ref = refs[0]
        refs = refs[1:]
    mod_ref, wa_hbm, wb_hbm, wc_hbm, wo_hbm, nw_ref = refs[:6]
    refs = refs[6:]
    if final:
        yc_ref, yl_ref = refs[:2]
        refs = refs[2:]
    else:
        modn_ref, y_ref, hn_ref = refs[:3]
        refs = refs[3:]
    a_scr, b_scr, c_scr, m_scr, wa_ref, wb_ref, wc_ref, wo_ref, stage, sem = refs[:10]
    y_scr = refs[10] if final else y_ref

    @pl.when(pl.program_id(0) == 0)
    def _():
        for w_hbm, w_ref in ((wa_hbm, wa_ref), (wb_hbm, wb_ref), (wc_hbm, wc_ref), (wo_hbm, wo_ref)):
            _load_weight_bf16(w_hbm, layer, w_ref, stage, sem)

    is_ctx = pl.program_id(0) < ctx_tiles
    a_scr[...] = _pick(is_ctx, ac_ref, al_ref)
    b_scr[...] = _pick(is_ctx, bc_ref, bl_ref)
    c_scr[...] = _pick(is_ctx, cc_ref, cl_ref)
    for n in range(D_MODEL // COL_TILE):
        cs = slice(n * COL_TILE, (n + 1) * COL_TILE)
        a = jnp.dot(a_scr[...], wa_ref[:, cs], preferred_element_type=F32)
        b = jnp.dot(b_scr[...], wb_ref[:, cs], preferred_element_type=F32)
        c = jnp.dot(c_scr[...], wc_ref[:, cs], preferred_element_type=F32)
        m = (g0_ref[:, cs].astype(F32) * a + g1_ref[:, cs].astype(F32) * b
             + g2_ref[:, cs].astype(F32) * c)
        m_scr[:, cs] = m.astype(BF16)
    ssq = jnp.zeros((m_scr.shape[0], 1), F32)
    for n in range(D_MODEL // COL_TILE):
        cs = slice(n * COL_TILE, (n + 1) * COL_TILE)
        out = jnp.dot(m_scr[...], wo_ref[:, cs], preferred_element_type=F32)
        x = jnp.where(is_ctx, xc_ref[:, cs], xl_ref[:, cs]) if first else x_ref[:, cs]
        y = x + mod_ref[:, 2 * D_MODEL + n * COL_TILE:2 * D_MODEL + (n + 1) * COL_TILE] * out
        y_scr[:, cs] = y
        ssq = ssq + jnp.sum(y * y, axis=-1, keepdims=True)
    yn = y_scr[...] * lax.rsqrt(ssq * (1.0 / D_MODEL) + EPS) * nw_ref[...]
    if final:
        @pl.when(is_ctx)
        def _():
            yc_ref[...] = yn

        @pl.when(jnp.logical_not(is_ctx))
        def _():
            yl_ref[...] = yn
    else:
        hn_ref[...] = (yn * (1.0 + modn_ref[:, D_MODEL:2 * D_MODEL]) + modn_ref[:, 0:D_MODEL]).astype(BF16)


def _outproj_call(attn, bout, cout, gates, x, mod3, wa, wb, wc, wo, nw, modn3, *, st, layer):
    tm = st.tm
    first = isinstance(x, tuple)
    final = modn3 is None
    gate = lambda g: pl.BlockSpec((tm, D_MODEL), lambda i: (i, g))
    row = pl.BlockSpec((tm, D_MODEL), lambda i: (i, 0))
    in_specs = (st.two_source_specs(A_WIDTH) + st.two_source_specs(B_WIDTH) + st.two_source_specs(C_WIDTH)
                + [gate(0), gate(1), gate(2)]
                + (st.two_source_specs(D_MODEL) if first else [row])
                + [pl.BlockSpec((None, 1, GATE_COLS), st.mod_index),
                   HBM, HBM, HBM, HBM,
                   pl.BlockSpec((1, D_MODEL), lambda i: (0, 0))])
    args = [*attn, *bout, *cout, gates, gates, gates, *(x if first else (x,)), mod3, wa, wb, wc, wo, nw]
    scratch = [pltpu.VMEM((tm, A_WIDTH), BF16), pltpu.VMEM((tm, B_WIDTH), BF16),
               pltpu.VMEM((tm, C_WIDTH), BF16), pltpu.VMEM((tm, D_MODEL), BF16),
               pltpu.VMEM((A_WIDTH, D_MODEL), BF16), pltpu.VMEM((B_WIDTH, D_MODEL), BF16),
               pltpu.VMEM((C_WIDTH, D_MODEL), BF16), pltpu.VMEM((D_MODEL, D_MODEL), BF16),
               pltpu.VMEM((2, OUT_CAST_ROWS, D_MODEL), F32), pltpu.SemaphoreType.DMA((2,))]
    if final:
        out_shape = [jax.ShapeDtypeStruct((st.ctx_tiles * tm, D_MODEL), F32),
                     jax.ShapeDtypeStruct(((st.tiles - st.ctx_tiles) * tm, D_MODEL), F32)]
        out_specs = st.two_source_specs(D_MODEL)
        scratch.append(pltpu.VMEM((tm, D_MODEL), F32))
    else:
        in_specs.append(pl.BlockSpec((None, 1, GATE_COLS), st.mod_index))
        args.append(modn3)
        out_shape = [jax.ShapeDtypeStruct((st.tiles * tm, D_MODEL), F32),
                     jax.ShapeDtypeStruct((st.tiles * tm, D_MODEL), BF16)]
        out_specs = [row, row]
    return pl.pallas_call(
        functools.partial(_outproj_kernel, first=first, final=final, ctx_tiles=st.ctx_tiles, layer=layer),
        out_shape=out_shape,
        grid=(st.tiles,),
        in_specs=in_specs,
        out_specs=out_specs,
        scratch_shapes=scratch,
        compiler_params=_params("arbitrary"),
        name="out_proj_final" if final else "out_proj",
    )(*args)


def _rope_tables(n_identity, n_tokens):
    rows = n_tokens // GRID_W
    row = np.repeat(np.arange(rows), GRID_W).astype(np.float64)
    col = np.tile(np.arange(GRID_W), rows).astype(np.float64)
    n_freq = HEAD_DIM // 4
    inv = ROPE_THETA ** (-np.arange(n_freq, dtype=np.float64) / n_freq)
    ar = row[:, None] * inv[None, :]
    ac = col[:, None] * inv[None, :]
    zero = np.zeros_like(ar)
    cos = np.concatenate([np.cos(ar), np.cos(ar), np.cos(ac), np.cos(ac)], axis=-1)
    sa = np.concatenate([-np.sin(ar), zero, -np.sin(ac), zero], axis=-1)
    sb = np.concatenate([zero, np.sin(ar), zero, np.sin(ac)], axis=-1)
    ident = (np.ones((n_identity, HEAD_DIM)), np.zeros((n_identity, HEAD_DIM)), np.zeros((n_identity, HEAD_DIM)))
    return tuple(jnp.asarray(np.concatenate([i, t], axis=0), dtype=F32) for i, t in zip(ident, (cos, sa, sb)))


def kernel(x_prompt, x_sample, cache_k, cache_v, c, c_ctx, norm_w, w_ada, b_ada, w_in, q_norm_w,
           k_norm_w, sgu_norm_w, w_sgu, b_sgu, w_pool, pool_scale, w_br_a, w_br_b, w_br_c, w_merge,
           b_merge, w_out, final_norm_w):
    nb_p, seq_p, d = x_prompt.shape
    nb_s, seq_s, _ = x_sample.shape
    n_ctx, n_lat = nb_p * seq_p, nb_s * seq_s
    assert d == D_MODEL and nb_s + 1 <= MOD_ROWS

    cv = jnp.concatenate([c_ctx[None, :], c, jnp.zeros((MOD_ROWS - 1 - nb_s, d), F32)], axis=0)
    mod = _ada_call(cv, w_ada, b_ada.reshape(DEPTH, 1, GATE_COLS))
    mod3 = [mod[l].reshape(MOD_ROWS, 1, GATE_COLS) for l in range(DEPTH)]
    st_in = _Stream(n_ctx, n_lat, seq_s, IN_TOKEN_TILE)
    st_out = _Stream(n_ctx, n_lat, seq_s, OUT_TOKEN_TILE)
    rope_tabs = _rope_tables(IN_TOKEN_TILE, seq_s)
    cache = (cache_k, cache_v)

    x = (x_prompt.reshape(n_ctx, d), x_sample.reshape(n_lat, d))
    h = _normmod_call(*x, mod3[0], norm_w[0].reshape(1, d), seq=seq_s)
    states = []
    w_mg, w_i, wa, wb, wc, wo = w_merge, w_in, w_br_a, w_br_b, w_br_c, w_out
    for l in range(DEPTH):
        last = l == DEPTH - 1
        bm = b_merge[l].reshape(1, GATE_COLS)
        qnw = q_norm_w[l].reshape(1, HEAD_DIM)
        knw = k_norm_w[l].reshape(1, HEAD_DIM)
        mix_w = (sgu_norm_w[l].reshape(1, B_WIDTH), w_sgu[l].astype(BF16),
                 b_sgu[l].reshape(B_GROUPS, CHUNK, 1), w_pool[l].astype(BF16),
                 pool_scale[l].reshape(1, C_WIDTH))
        nw_next = (final_norm_w if last else norm_w[l + 1]).reshape(1, d)

        gates = _gates_call(h, w_mg, bm, layer=l)
        proj, ks, vs = _proj_call(h, w_i, qnw, knw, rope_tabs, st=st_in, layer=l)
        states.append((ks, vs))
        attn = (_attn_call(proj, None, row0=0, n_batch=nb_p, seq=seq_p, layer=l),
                _attn_call(proj, cache, row0=n_ctx, n_batch=nb_s, seq=seq_s, layer=l))
        mix_c = _mix_call(proj, *mix_w, row0=0, n_batch=nb_p, seq=seq_p)
        mix_l = _mix_call(proj, *mix_w, row0=n_ctx, n_batch=nb_s, seq=seq_s)
        res = _outproj_call(attn, (mix_c[0], mix_l[0]), (mix_c[1], mix_l[1]), gates, x, mod3[l],
                            wa, wb, wc, wo, nw_next, None if last else mod3[l + 1], st=st_out, layer=l)
        if last:
            y_ctx, y_lat = res
        else:
            x, h = res

    state_k, state_v = (
        jnp.stack([s[i][:n_ctx].reshape(nb_p, seq_p, A_KV_HEADS, HEAD_DIM) for s in states], axis=1)
        for i in range(2))
    return (y_ctx.reshape(nb_p, seq_p, d), y_lat.reshape(nb_s, seq_s, d), state_k, state_v)
```

```python
import functools

import jax
import jax.numpy as jnp
import numpy as np
from jax import lax
from jax.experimental import pallas as pl
from jax.experimental.pallas import tpu as pltpu

F32 = jnp.float32
BF16 = jnp.bfloat16

D_MODEL = 2048
DEPTH = 2
GRID_W = 64
EPS = 1e-6
HEAD_DIM = 128
A_HEADS = 8
A_KV_HEADS = 2
A_WIDTH = A_HEADS * HEAD_DIM
KV_WIDTH = A_KV_HEADS * HEAD_DIM
ROPE_THETA = 10000.0
ATTN_SCALE = HEAD_DIM ** -0.5
LOG2_E = 1.4426950408889634
CHUNK = 128
B_GROUPS = 4
B_WIDTH = 512
C_WIDTH = 512
POOL_WINDOWS = (2, 4, 8, 16)
POOL_HALO = 64
N_BRANCH = 3
GATE_COLS = N_BRANCH * D_MODEL
IN_COLS = 2 * A_WIDTH + 2 * KV_WIDTH + 3 * B_WIDTH + 2 * C_WIDTH

COL_TILE = 512
Q_TILE0 = 0
KV_TILE = Q_TILE0 + A_WIDTH // COL_TILE
GA_TILE0 = KV_TILE + 1
U_TILE = GA_TILE0 + A_WIDTH // COL_TILE
VB_TILE = U_TILE + 1
GB_TILE = VB_TILE + 1
Z_TILE = GB_TILE + 1
GC_TILE = Z_TILE + 1
HEADS_PER_TILE = COL_TILE // HEAD_DIM
K_COL = KV_TILE * COL_TILE
V_COL = K_COL + KV_WIDTH

SUB_COLS = 256
MOD_ROWS = 8
ADA_TILE = 1024
IN_TOKEN_TILE = 512
NORM_TOKEN_TILE = 1024
NORM_ROWS = 16
OUT_TOKEN_TILE = 256
ATTN_Q_TILE = 512
MIX_TILE = 512
SHORT_SEQ_ROWS = 1024
CAST_ROWS = 128
OUT_CAST_ROWS = 256
VMEM_LIMIT = 56 * 1024 * 1024


def _params(*sem):
    return pltpu.CompilerParams(dimension_semantics=sem, vmem_limit_bytes=VMEM_LIMIT)


def _resident(shape):
    return pl.BlockSpec(shape, lambda *_: (0,) * len(shape), pipeline_mode=pl.Buffered(1))


def _silu(x):
    return x * jax.nn.sigmoid(x)


def _rms(x, w):
    ms = jnp.mean(x * x, axis=-1, keepdims=True)
    return x * lax.rsqrt(ms + EPS) * w


class _Stream:
    def __init__(self, n_ctx, n_lat, seq, tm):
        assert n_ctx % tm == 0 and seq % tm == 0
        self.tm = tm
        self.ctx_tiles = n_ctx // tm
        self.tiles = (n_ctx + n_lat) // tm
        self.tiles_per_seq = seq // tm

    def mod_index(self, i):
        lat = 1 + (i - self.ctx_tiles) // self.tiles_per_seq
        return (jnp.where(i < self.ctx_tiles, 0, lat), 0, 0)

    def two_source_specs(self, width):
        ctx = pl.BlockSpec((self.tm, width), lambda i: (jnp.minimum(i, self.ctx_tiles - 1), 0))
        lat = pl.BlockSpec((self.tm, width), lambda i: (jnp.maximum(i - self.ctx_tiles, 0), 0))
        return [ctx, lat]


def _pick(is_ctx, ctx_ref, lat_ref):
    return jnp.where(is_ctx, ctx_ref[...], lat_ref[...])


def _load_weight_bf16(w_hbm, layer, w_scr, stage, sem):
    chunk = stage.shape[1]
    n = w_scr.shape[0] // chunk

    def copy(c):
        return pltpu.make_async_copy(w_hbm.at[layer, pl.ds(c * chunk, chunk), :], stage.at[c % 2], sem.at[c % 2])

    copy(0).start()
    for c in range(n):
        if c + 1 < n:
            copy(c + 1).start()
        copy(c).wait()
        w_scr[c * chunk:(c + 1) * chunk, :] = stage[c % 2].astype(BF16)


def _weight_scratch(rows, cols, chunk):
    return [pltpu.VMEM((rows, cols), BF16), pltpu.VMEM((2, chunk, cols), F32), pltpu.SemaphoreType.DMA((2,))]


HBM = pl.BlockSpec(memory_space=pl.ANY)


def _ada_kernel(cv_ref, w_ref, b_ref, o_ref):
    a = _silu(cv_ref[...]).astype(BF16)
    o_ref[...] = jnp.dot(a, w_ref[...].astype(BF16), preferred_element_type=F32) + b_ref[...]


def _ada_call(cv, w_ada, b_ada):
    return pl.pallas_call(
        _ada_kernel,
        out_shape=jax.ShapeDtypeStruct((DEPTH, MOD_ROWS, GATE_COLS), F32),
        grid=(DEPTH, GATE_COLS // ADA_TILE),
        in_specs=[
            pl.BlockSpec((MOD_ROWS, D_MODEL), lambda l, j: (0, 0)),
            pl.BlockSpec((None, D_MODEL, ADA_TILE), lambda l, j: (l, 0, j)),
            pl.BlockSpec((None, 1, ADA_TILE), lambda l, j: (l, 0, j)),
        ],
        out_specs=pl.BlockSpec((None, MOD_ROWS, ADA_TILE), lambda l, j: (l, 0, j)),
        compiler_params=_params("arbitrary", "arbitrary"),
        name="ada_mod",
    )(cv, w_ada, b_ada)


def _normmod_kernel(xc_ref, xl_ref, mod_ref, nw_ref, h_ref, *, ctx_tiles):
    def emit(x_ref):
        def chunk(r, carry):
            rows = pl.ds(pl.multiple_of(r * NORM_ROWS, NORM_ROWS), NORM_ROWS)
            y = _rms(x_ref[rows, :], nw_ref[...])
            h_ref[rows, :] = (y * (1.0 + mod_ref[:, D_MODEL:2 * D_MODEL]) + mod_ref[:, 0:D_MODEL]).astype(BF16)
            return carry

        lax.fori_loop(0, h_ref.shape[0] // NORM_ROWS, chunk, 0, unroll=8)

    @pl.when(pl.program_id(0) < ctx_tiles)
    def _():
        emit(xc_ref)

    @pl.when(pl.program_id(0) >= ctx_tiles)
    def _():
        emit(xl_ref)


def _normmod_call(x_ctx, x_lat, mod3, norm_w, *, seq):
    st = _Stream(x_ctx.shape[0], x_lat.shape[0], seq, NORM_TOKEN_TILE)
    return pl.pallas_call(
        functools.partial(_normmod_kernel, ctx_tiles=st.ctx_tiles),
        out_shape=jax.ShapeDtypeStruct((st.tiles * st.tm, D_MODEL), BF16),
        grid=(st.tiles,),
        in_specs=st.two_source_specs(D_MODEL) + [
            pl.BlockSpec((None, 1, GATE_COLS), st.mod_index),
            pl.BlockSpec((1, D_MODEL), lambda i: (0, 0)),
        ],
        out_specs=pl.BlockSpec((st.tm, D_MODEL), lambda i: (i, 0)),
        compiler_params=_params("arbitrary"),
        name="norm_mod",
    )(x_ctx, x_lat, mod3, norm_w)


def _gates_kernel(h_ref, w_hbm, b_ref, o_ref, w_ref, stage, sem, *, layer):
    @pl.when(pl.program_id(0) == 0)
    def _():
        _load_weight_bf16(w_hbm, layer, w_ref, stage, sem)

    for s in range(GATE_COLS // SUB_COLS):
        cs = slice(s * SUB_COLS, (s + 1) * SUB_COLS)
        acc = jnp.dot(h_ref[...], w_ref[:, cs], preferred_element_type=F32)
        o_ref[:, cs] = jax.nn.sigmoid(acc + b_ref[:, cs]).astype(BF16)


def _gates_call(h, w_merge, b_merge, *, layer):
    t = h.shape[0]
    tm = IN_TOKEN_TILE
    return pl.pallas_call(
        functools.partial(_gates_kernel, layer=layer),
        out_shape=jax.ShapeDtypeStruct((t, GATE_COLS), BF16),
        grid=(t // tm,),
        in_specs=[
            pl.BlockSpec((tm, D_MODEL), lambda i: (i, 0)),
            HBM,
            _resident((1, GATE_COLS)),
        ],
        out_specs=pl.BlockSpec((tm, GATE_COLS), lambda i: (i, 0)),
        scratch_shapes=_weight_scratch(D_MODEL, GATE_COLS, CAST_ROWS),
        compiler_params=_params("arbitrary"),
        name="gates",
    )(h, w_merge, b_merge)


def _rope(y, cos, sa, sb):
    return y * cos + pltpu.roll(y, 96, 1) * sa + pltpu.roll(y, 32, 1) * sb


def _proj_kernel(h_ref, w_hbm, qnw_ref, knw_ref, cos_ref, sa_ref, sb_ref, proj_ref, ks_ref, vs_ref,
                 w_ref, stage, sem, *, layer):
    @pl.when(pl.program_id(0) == 0)
    def _():
        _load_weight_bf16(w_hbm, layer, w_ref, stage, sem)

    def head(xh, w):
        return _rope(_rms(xh, w), cos_ref[...], sa_ref[...], sb_ref[...])

    qw = qnw_ref[...] * (ATTN_SCALE * LOG2_E)
    silu_cols = ((GA_TILE0 * COL_TILE, U_TILE * COL_TILE), (GB_TILE * COL_TILE, Z_TILE * COL_TILE),
                 (GC_TILE * COL_TILE, IN_COLS))
    for s in range(IN_COLS // SUB_COLS):
        acc = jnp.dot(h_ref[...], w_ref[:, s * SUB_COLS:(s + 1) * SUB_COLS], preferred_element_type=F32)
        for hb in range(SUB_COLS // HEAD_DIM):
            c0 = s * SUB_COLS + hb * HEAD_DIM
            cs = slice(c0, c0 + HEAD_DIM)
            a = acc[:, hb * HEAD_DIM:(hb + 1) * HEAD_DIM]
            if c0 < K_COL:
                proj_ref[:, cs] = head(a, qw).astype(BF16)
            elif c0 < V_COL:
                ks_ref[:, c0 - K_COL:c0 - K_COL + HEAD_DIM] = _rms(a, knw_ref[...])
                proj_ref[:, cs] = head(a, knw_ref[...]).astype(BF16)
            elif c0 < V_COL + KV_WIDTH:
                vs_ref[:, c0 - V_COL:c0 - V_COL + HEAD_DIM] = a
                proj_ref[:, cs] = a.astype(BF16)
            elif any(lo <= c0 < hi for lo, hi in silu_cols):
                proj_ref[:, cs] = _silu(a).astype(BF16)
            else:
                proj_ref[:, cs] = a.astype(BF16)


def _proj_call(h, w_in, qnw, knw, rope_tabs, *, st, layer):
    t = h.shape[0]
    tm = st.tm

    def tab_index(i):
        return (jnp.where(i < st.ctx_tiles, 0, 1 + (i - st.ctx_tiles) % st.tiles_per_seq), 0)

    in_specs = [
        pl.BlockSpec((tm, D_MODEL), lambda i: (i, 0)),
        HBM,
        pl.BlockSpec((1, HEAD_DIM), lambda i: (0, 0)),
        pl.BlockSpec((1, HEAD_DIM), lambda i: (0, 0)),
    ] + [pl.BlockSpec((tm, HEAD_DIM), tab_index) for _ in rope_tabs]
    out_shape = [jax.ShapeDtypeStruct((t, IN_COLS), BF16),
                 jax.ShapeDtypeStruct((t, KV_WIDTH), F32), jax.ShapeDtypeStruct((t, KV_WIDTH), F32)]
    out_specs = [pl.BlockSpec((tm, IN_COLS), lambda i: (i, 0)),
                 pl.BlockSpec((tm, KV_WIDTH), lambda i: (i, 0)), pl.BlockSpec((tm, KV_WIDTH), lambda i: (i, 0))]
    return pl.pallas_call(
        functools.partial(_proj_kernel, layer=layer),
        out_shape=out_shape,
        grid=(t // tm,),
        in_specs=in_specs,
        out_specs=out_specs,
        scratch_shapes=_weight_scratch(D_MODEL, IN_COLS, CAST_ROWS),
        compiler_params=_params("arbitrary"),
        name="proj",
    )(h, w_in, qnw, knw, *rope_tabs)


def _attn_kernel(*refs, ctx, n_seq):
    q_ref, k_ref, v_ref, ga0_ref, ga1_ref = refs[:5]
    ga_refs = (ga0_ref, ga1_ref)
    refs = refs[5:]
    if ctx:
        ck_ref, cv_ref = refs[:2]
        refs = refs[2:]
    o_ref, vx_scr = refs[:2]
    if ctx:
        ckx_scr, cvx_scr = refs[2:]
    seq = k_ref.shape[0] // n_seq
    tq = q_ref.shape[0] // n_seq

    @pl.when(pl.program_id(1) == 0)
    def _():
        for j in range(n_seq):
            for kh in range(A_KV_HEADS):
                hs = slice(kh * HEAD_DIM, (kh + 1) * HEAD_DIM)
                vx_scr[j, kh, :, :HEAD_DIM] = v_ref[j * seq:(j + 1) * seq, hs]
                vx_scr[j, kh, :, HEAD_DIM:] = jnp.ones((seq, HEAD_DIM), BF16)
        if ctx:
            for kh in range(A_KV_HEADS):
                ckx_scr[kh] = ck_ref[:, kh, :].astype(BF16)
                cvx_scr[kh, :, :HEAD_DIM] = cv_ref[:, kh, :].astype(BF16)
                cvx_scr[kh, :, HEAD_DIM:] = jnp.ones((cvx_scr.shape[1], HEAD_DIM), BF16)

    nt = (((1,), (1,)), ((), ()))
    for j in range(n_seq):
        qrows = slice(j * tq, (j + 1) * tq)
        for head in range(A_HEADS):
            kh, hh = divmod(head, HEADS_PER_TILE)
            cs = slice(head * HEAD_DIM, (head + 1) * HEAD_DIM)
            q = q_ref[qrows, cs]
            k = k_ref[j * seq:(j + 1) * seq, kh * HEAD_DIM:(kh + 1) * HEAD_DIM]
            s1 = lax.dot_general(q, k, nt, preferred_element_type=F32)
            m = jnp.max(s1, axis=-1, keepdims=True)
            if ctx:
                s2 = lax.dot_general(q, ckx_scr[kh], nt, preferred_element_type=F32)
                m = jnp.maximum(m, jnp.max(s2, axis=-1, keepdims=True))
            ox = jnp.dot(jnp.exp2(s1 - m).astype(BF16), vx_scr[j, kh], preferred_element_type=F32)
            if ctx:
                ox = ox + jnp.dot(jnp.exp2(s2 - m).astype(BF16), cvx_scr[kh], preferred_element_type=F32)
            o = ox[:, :HEAD_DIM] / ox[:, HEAD_DIM:]
            ga = ga_refs[kh][qrows, hh * HEAD_DIM:(hh + 1) * HEAD_DIM]
            o_ref[qrows, cs] = (o * ga.astype(F32)).astype(BF16)


def _attn_call(proj, cache, *, row0, n_batch, seq, layer):
    tq = min(ATTN_Q_TILE, seq)
    nq = seq // tq
    ctx = cache is not None
    n_seq = max(SHORT_SEQ_ROWS // seq, 1) if (nq == 1 and not ctx) else 1
    assert n_batch % n_seq == 0
    qb, sb = tq * n_seq, seq * n_seq
    q0, s0 = row0 // qb, row0 // sb
    in_specs = [
        pl.BlockSpec((qb, A_WIDTH), lambda b, qi: (q0 + b * nq + qi, Q_TILE0 * COL_TILE // A_WIDTH)),
        pl.BlockSpec((sb, KV_WIDTH), lambda b, qi: (s0 + b, K_COL // KV_WIDTH)),
        pl.BlockSpec((sb, KV_WIDTH), lambda b, qi: (s0 + b, V_COL // KV_WIDTH)),
        pl.BlockSpec((qb, COL_TILE), lambda b, qi: (q0 + b * nq + qi, GA_TILE0)),
        pl.BlockSpec((qb, COL_TILE), lambda b, qi: (q0 + b * nq + qi, GA_TILE0 + 1)),
    ]
    args = [proj, proj, proj, proj, proj]
    scratch = [pltpu.VMEM((n_seq, A_KV_HEADS, seq, 2 * HEAD_DIM), BF16)]
    if ctx:
        past = cache[0].shape[2]
        for c in cache:
            in_specs.append(pl.BlockSpec((None, None, past, A_KV_HEADS, HEAD_DIM),
                                         lambda b, qi: (b, layer, 0, 0, 0)))
            args.append(c)
        scratch += [pltpu.VMEM((A_KV_HEADS, past, HEAD_DIM), BF16),
                    pltpu.VMEM((A_KV_HEADS, past, 2 * HEAD_DIM), BF16)]
    return pl.pallas_call(
        functools.partial(_attn_kernel, ctx=ctx, n_seq=n_seq),
        out_shape=jax.ShapeDtypeStruct((n_batch * seq, A_WIDTH), BF16),
        grid=(n_batch // n_seq, nq),
        in_specs=in_specs,
        out_specs=pl.BlockSpec((qb, A_WIDTH), lambda b, qi: (b * nq + qi, 0)),
        scratch_shapes=scratch,
        compiler_params=_params("arbitrary", "arbitrary"),
        name="attention_lat" if ctx else "attention_ctx",
    )(*args)


def _mix_kernel(u_ref, vb_ref, gb_ref, z_ref, gc_ref, snw_ref, ws_ref, bs_ref, wp_ref, ps_ref, band_ref, inv_ref,
                bo_ref, co_ref, zb_scr, *, n_seq):
    r = u_ref.shape[0] // n_seq
    c = pl.program_id(1)
    nc = pl.num_programs(1)
    base = pl.multiple_of(c * r, r)
    n_chunks = r // CHUNK
    z_rows = z_ref.shape[0]
    for j in range(n_seq):
        row0 = j * r

        vbn = _rms(vb_ref[row0:row0 + r, :].astype(F32), snw_ref[...]).astype(BF16)
        for g in range(B_GROUPS):
            cs = slice(g * HEAD_DIM, (g + 1) * HEAD_DIM)
            wide = jnp.concatenate([vbn[cc * CHUNK:(cc + 1) * CHUNK, cs] for cc in range(n_chunks)], axis=1)
            mixed = jnp.dot(ws_ref[g], wide, preferred_element_type=F32) + bs_ref[g]
            for cc in range(n_chunks):
                rs = slice(row0 + cc * CHUNK, row0 + (cc + 1) * CHUNK)
                mx = mixed[:, cc * HEAD_DIM:(cc + 1) * HEAD_DIM]
                bo_ref[rs, cs] = (u_ref[rs, cs].astype(F32) * mx * gb_ref[rs, cs].astype(F32)).astype(BF16)

        zb = zb_scr.at[j]
        z0 = base + row0
        zb[POOL_HALO:POOL_HALO + r, :] = z_ref[pl.ds(pl.multiple_of(z0, POOL_HALO), r), :]
        above = z_ref[pl.ds(pl.multiple_of(jnp.maximum(z0 - POOL_HALO, 0), POOL_HALO), POOL_HALO), :]
        below = z_ref[pl.ds(pl.multiple_of(jnp.minimum(z0 + r, z_rows - POOL_HALO), POOL_HALO), POOL_HALO), :]
        zeros = jnp.zeros((POOL_HALO, C_WIDTH), BF16)
        zb[0:POOL_HALO, :] = jnp.where(c > 0, above, zeros)
        zb[POOL_HALO + r:, :] = jnp.where(c < nc - 1, below, zeros)

        for g in range(len(POOL_WINDOWS)):
            cs = slice(g * HEAD_DIM, (g + 1) * HEAD_DIM)
            ds = []
            for blk in range(n_chunks):
                lo = blk * CHUNK
                wsum = jnp.dot(band_ref[g], zb[lo:lo + CHUNK + 2 * POOL_HALO, cs], preferred_element_type=F32)
                zc = zb[POOL_HALO + lo:POOL_HALO + lo + CHUNK, cs].astype(F32)
                ds.append((wsum * inv_ref[lo:lo + CHUNK, cs] - zc).astype(BF16))
            dm = jnp.dot(jnp.concatenate(ds, axis=0), wp_ref[g], preferred_element_type=F32)
            rows = slice(row0, row0 + r)
            co_ref[rows, cs] = (dm * ps_ref[:, cs] * gc_ref[rows, cs].astype(F32)).astype(BF16)


def _pool_band():
    t = np.arange(CHUNK)[:, None] + POOL_HALO
    j = np.arange(CHUNK + 2 * POOL_HALO)[None, :]
    return jnp.asarray(np.stack([(j >= t - w // 2) & (j < t + w - w // 2) for w in POOL_WINDOWS]), dtype=BF16)


def _pool_inv_count(seq):
    t = np.arange(seq)
    cols = [1.0 / (np.clip(t + w - w // 2, 0, seq) - np.clip(t - w // 2, 0, seq)) for w in POOL_WINDOWS]
    return jnp.asarray(np.repeat(np.stack(cols, axis=1), HEAD_DIM, axis=1), dtype=F32)


def _mix_call(proj, snw, ws, bs, wp, ps, *, row0, n_batch, seq):
    r = min(MIX_TILE, seq)
    nc = seq // r
    n_seq = max(SHORT_SEQ_ROWS // seq, 1) if nc == 1 else 1
    assert n_batch % n_seq == 0
    rb, sb = r * n_seq, seq * n_seq
    r0, s0 = row0 // rb, row0 // sb
    tile = lambda col: pl.BlockSpec((rb, COL_TILE), lambda b, c: (r0 + b * nc + c, col))
    const3 = lambda shape: pl.BlockSpec(shape, lambda b, c: (0, 0, 0))
    out_spec = pl.BlockSpec((rb, COL_TILE), lambda b, c: (b * nc + c, 0))
    return pl.pallas_call(
        functools.partial(_mix_kernel, n_seq=n_seq),
        out_shape=[jax.ShapeDtypeStruct((n_batch * seq, B_WIDTH), BF16),
                   jax.ShapeDtypeStruct((n_batch * seq, C_WIDTH), BF16)],
        grid=(n_batch // n_seq, nc),
        in_specs=[
            tile(U_TILE), tile(VB_TILE), tile(GB_TILE),
            pl.BlockSpec((sb, COL_TILE), lambda b, c: (s0 + b, Z_TILE)),
            tile(GC_TILE),
            pl.BlockSpec((1, B_WIDTH), lambda b, c: (0, 0)),
            const3((B_GROUPS, CHUNK, CHUNK)),
            const3((B_GROUPS, CHUNK, 1)),
            const3((B_GROUPS, HEAD_DIM, HEAD_DIM)),
            pl.BlockSpec((1, C_WIDTH), lambda b, c: (0, 0)),
            const3((len(POOL_WINDOWS), CHUNK, CHUNK + 2 * POOL_HALO)),
            pl.BlockSpec((r, C_WIDTH), lambda b, c: (c, 0)),
        ],
        out_specs=[out_spec, out_spec],
        scratch_shapes=[pltpu.VMEM((n_seq, r + 2 * POOL_HALO, C_WIDTH), BF16)],
        compiler_params=_params("arbitrary", "arbitrary"),
        name="sgu_pool",
    )(proj, proj, proj, proj, proj, snw, ws, bs, wp, ps, _pool_band(), _pool_inv_count(seq))


def _outproj_kernel(*refs, first, final, ctx_tiles, layer):
    (ac_ref, al_ref, bc_ref, bl_ref, cc_ref, cl_ref, g0_ref, g1_ref, g2_ref) = refs[:9]
    refs = refs[9:]
    if first:
        xc_ref, xl_ref = refs[:2]
        refs = refs[2:]
    else:
        x_ref = refs[0]
        refs = refs[1:]
    mod_ref, wa_hbm, wb_hbm, wc_hbm, wo_hbm, nw_ref = refs[:6]
    refs = refs[6:]
    if final:
        yc_ref, yl_ref = refs[:2]
        refs = refs[2:]
    else:
        modn_ref, y_ref, hn_ref = refs[:3]
        refs = refs[3:]
    a_scr, b_scr, c_scr, m_scr, wa_ref, wb_ref, wc_ref, wo_ref, stage, sem = refs[:10]
    y_scr = refs[10] if final else y_ref

    @pl.when(pl.program_id(0) == 0)
    def _():
        for w_hbm, w_ref in ((wa_hbm, wa_ref), (wb_hbm, wb_ref), (wc_hbm, wc_ref), (wo_hbm, wo_ref)):
            _load_weight_bf16(w_hbm, layer, w_ref, stage, sem)

    is_ctx = pl.program_id(0) < ctx_tiles
    a_scr[...] = _pick(is_ctx, ac_ref, al_ref)
    b_scr[...] = _pick(is_ctx, bc_ref, bl_ref)
    c_scr[...] = _pick(is_ctx, cc_ref, cl_ref)
    for n in range(D_MODEL // COL_TILE):
        cs = slice(n * COL_TILE, (n + 1) * COL_TILE)
        a = jnp.dot(a_scr[...], wa_ref[:, cs], preferred_element_type=F32)
        b = jnp.dot(b_scr[...], wb_ref[:, cs], preferred_element_type=F32)
        c = jnp.dot(c_scr[...], wc_ref[:, cs], preferred_element_type=F32)
        m = (g0_ref[:, cs].astype(F32) * a + g1_ref[:, cs].astype(F32) * b
             + g2_ref[:, cs].astype(F32) * c)
        m_scr[:, cs] = m.astype(BF16)
    ssq = jnp.zeros((m_scr.shape[0], 1), F32)
    for n in range(D_MODEL // COL_TILE):
        cs = slice(n * COL_TILE, (n + 1) * COL_TILE)
        out = jnp.dot(m_scr[...], wo_ref[:, cs], preferred_element_type=F32)
        x = jnp.where(is_ctx, xc_ref[:, cs], xl_ref[:, cs]) if first else x_ref[:, cs]
        y = x + mod_ref[:, 2 * D_MODEL + n * COL_TILE:2 * D_MODEL + (n + 1) * COL_TILE] * out
        y_scr[:, cs] = y
        ssq = ssq + jnp.sum(y * y, axis=-1, keepdims=True)
    yn = y_scr[...] * lax.rsqrt(ssq * (1.0 / D_MODEL) + EPS) * nw_ref[...]
    if final:
        @pl.when(is_ctx)
        def _():
            yc_ref[...] = yn

        @pl.when(jnp.logical_not(is_ctx))
        def _():
            yl_ref[...] = yn
    else:
        hn_ref[...] = (yn * (1.0 + modn_ref[:, D_MODEL:2 * D_MODEL]) + modn_ref[:, 0:D_MODEL]).astype(BF16)


def _outproj_call(attn, bout, cout, gates, x, mod3, wa, wb, wc, wo, nw, modn3, *, st, layer):
    tm = st.tm
    first = isinstance(x, tuple)
    final = modn3 is None
    gate = lambda g: pl.BlockSpec((tm, D_MODEL), lambda i: (i, g))
    row = pl.BlockSpec((tm, D_MODEL), lambda i: (i, 0))
    in_specs = (st.two_source_specs(A_WIDTH) + st.two_source_specs(B_WIDTH) + st.two_source_specs(C_WIDTH)
                + [gate(0), gate(1), gate(2)]
                + (st.two_source_specs(D_MODEL) if first else [row])
                + [pl.BlockSpec((None, 1, GATE_COLS), st.mod_index),
                   HBM, HBM, HBM, HBM,
                   pl.BlockSpec((1, D_MODEL), lambda i: (0, 0))])
    args = [*attn, *bout, *cout, gates, gates, gates, *(x if first else (x,)), mod3, wa, wb, wc, wo, nw]
    scratch = [pltpu.VMEM((tm, A_WIDTH), BF16), pltpu.VMEM((tm, B_WIDTH), BF16),
               pltpu.VMEM((tm, C_WIDTH), BF16), pltpu.VMEM((tm, D_MODEL), BF16),
               pltpu.VMEM((A_WIDTH, D_MODEL), BF16), pltpu.VMEM((B_WIDTH, D_MODEL), BF16),
               pltpu.VMEM((C_WIDTH, D_MODEL), BF16), pltpu.VMEM((D_MODEL, D_MODEL), BF16),
               pltpu.VMEM((2, OUT_CAST_ROWS, D_MODEL), F32), pltpu.SemaphoreType.DMA((2,))]
    if final:
        out_shape = [jax.ShapeDtypeStruct((st.ctx_tiles * tm, D_MODEL), F32),
                     jax.ShapeDtypeStruct(((st.tiles - st.ctx_tiles) * tm, D_MODEL), F32)]
        out_specs = st.two_source_specs(D_MODEL)
        scratch.append(pltpu.VMEM((tm, D_MODEL), F32))
    else:
        in_specs.append(pl.BlockSpec((None, 1, GATE_COLS), st.mod_index))
        args.append(modn3)
        out_shape = [jax.ShapeDtypeStruct((st.tiles * tm, D_MODEL), F32),
                     jax.ShapeDtypeStruct((st.tiles * tm, D_MODEL), BF16)]
        out_specs = [row, row]
    return pl.pallas_call(
        functools.partial(_outproj_kernel, first=first, final=final, ctx_tiles=st.ctx_tiles, layer=layer),
        out_shape=out_shape,
        grid=(st.tiles,),
        in_specs=in_specs,
        out_specs=out_specs,
        scratch_shapes=scratch,
        compiler_params=_params("arbitrary"),
        name="out_proj_final" if final else "out_proj",
    )(*args)


def _rope_tables(n_identity, n_tokens):
    rows = n_tokens // GRID_W
    row = np.repeat(np.arange(rows), GRID_W).astype(np.float64)
    col = np.tile(np.arange(GRID_W), rows).astype(np.float64)
    n_freq = HEAD_DIM // 4
    inv = ROPE_THETA ** (-np.arange(n_freq, dtype=np.float64) / n_freq)
    ar = row[:, None] * inv[None, :]
    ac = col[:, None] * inv[None, :]
    zero = np.zeros_like(ar)
    cos = np.concatenate([np.cos(ar), np.cos(ar), np.cos(ac), np.cos(ac)], axis=-1)
    sa = np.concatenate([-np.sin(ar), zero, -np.sin(ac), zero], axis=-1)
    sb = np.concatenate([zero, np.sin(ar), zero, np.sin(ac)], axis=-1)
    ident = (np.ones((n_identity, HEAD_DIM)), np.zeros((n_identity, HEAD_DIM)), np.zeros((n_identity, HEAD_DIM)))
    return tuple(jnp.asarray(np.concatenate([i, t], axis=0), dtype=F32) for i, t in zip(ident, (cos, sa, sb)))


def kernel(x_prompt, x_sample, cache_k, cache_v, c, c_ctx, norm_w, w_ada, b_ada, w_in, q_norm_w,
           k_norm_w, sgu_norm_w, w_sgu, b_sgu, w_pool, pool_scale, w_br_a, w_br_b, w_br_c, w_merge,
           b_merge, w_out, final_norm_w):
    nb_p, seq_p, d = x_prompt.shape
    nb_s, seq_s, _ = x_sample.shape
    n_ctx, n_lat = nb_p * seq_p, nb_s * seq_s
    assert d == D_MODEL and nb_s + 1 <= MOD_ROWS

    cv = jnp.concatenate([c_ctx[None, :], c, jnp.zeros((MOD_ROWS - 1 - nb_s, d), F32)], axis=0)
    mod = _ada_call(cv, w_ada, b_ada.reshape(DEPTH, 1, GATE_COLS))
    mod3 = [mod[l].reshape(MOD_ROWS, 1, GATE_COLS) for l in range(DEPTH)]
    st_in = _Stream(n_ctx, n_lat, seq_s, IN_TOKEN_TILE)
    st_out = _Stream(n_ctx, n_lat, seq_s, OUT_TOKEN_TILE)
    rope_tabs = _rope_tables(IN_TOKEN_TILE, seq_s)
    cache = (cache_k, cache_v)

    x = (x_prompt.reshape(n_ctx, d), x_sample.reshape(n_lat, d))
    h = _normmod_call(*x, mod3[0], norm_w[0].reshape(1, d), seq=seq_s)
    states = []
    w_mg, w_i, wa, wb, wc, wo = w_merge, w_in, w_br_a, w_br_b, w_br_c, w_out
    for l in range(DEPTH):
        last = l == DEPTH - 1
        bm = b_merge[l].reshape(1, GATE_COLS)
        qnw = q_norm_w[l].reshape(1, HEAD_DIM)
        knw = k_norm_w[l].reshape(1, HEAD_DIM)
        mix_w = (sgu_norm_w[l].reshape(1, B_WIDTH), w_sgu[l].astype(BF16),
                 b_sgu[l].reshape(B_GROUPS, CHUNK, 1), w_pool[l].astype(BF16),
                 pool_scale[l].reshape(1, C_WIDTH))
        nw_next = (final_norm_w if last else norm_w[l + 1]).reshape(1, d)

        gates = _gates_call(h, w_mg, bm, layer=l)
        proj, ks, vs = _proj_call(h, w_i, qnw, knw, rope_tabs, st=st_in, layer=l)
        states.append((ks, vs))
        attn = (_attn_call(proj, None, row0=0, n_batch=nb_p, seq=seq_p, layer=l),
                _attn_call(proj, cache, row0=n_ctx, n_batch=nb_s, seq=seq_s, layer=l))
        mix_c = _mix_call(proj, *mix_w, row0=0, n_batch=nb_p, seq=seq_p)
        mix_l = _mix_call(proj, *mix_w, row0=n_ctx, n_batch=nb_s, seq=seq_s)
        res = _outproj_call(attn, (mix_c[0], mix_l[0]), (mix_c[1], mix_l[1]), gates, x, mod3[l],
                            wa, wb, wc, wo, nw_next, None if last else mod3[l + 1], st=st_out, layer=l)
        if last:
            y_ctx, y_lat = res
        else:
            x, h = res

    state_k, state_v = (
        jnp.stack([s[i][:n_ctx].reshape(nb_p, seq_p, A_KV_HEADS, HEAD_DIM) for s in states], axis=1)
        for i in range(2))
    return (y_ctx.reshape(nb_p, seq_p, d), y_lat.reshape(nb_s, seq_s, d), state_k, state_v)
```

```python
import functools

import jax
import jax.numpy as jnp
import numpy as np
from jax import lax
from jax.experimental import pallas as pl
from jax.experimental.pallas import tpu as pltpu

F32 = jnp.float32
BF16 = jnp.bfloat16

D_MODEL = 2048
DEPTH = 2
GRID_W = 64
EPS = 1e-6
HEAD_DIM = 128
A_HEADS = 8
A_KV_HEADS = 2
A_WIDTH = A_HEADS * HEAD_DIM
KV_WIDTH = A_KV_HEADS * HEAD_DIM
ROPE_THETA = 10000.0
ATTN_SCALE = HEAD_DIM ** -0.5
LOG2_E = 1.4426950408889634
CHUNK = 128
B_GROUPS = 4
B_WIDTH = 512
C_WIDTH = 512
POOL_WINDOWS = (2, 4, 8, 16)
POOL_HALO = 64
N_BRANCH = 3
GATE_COLS = N_BRANCH * D_MODEL
IN_COLS = 2 * A_WIDTH + 2 * KV_WIDTH + 3 * B_WIDTH + 2 * C_WIDTH

COL_TILE = 512
Q_TILE0 = 0
KV_TILE = Q_TILE0 + A_WIDTH // COL_TILE
GA_TILE0 = KV_TILE + 1
U_TILE = GA_TILE0 + A_WIDTH // COL_TILE
VB_TILE = U_TILE + 1
GB_TILE = VB_TILE + 1
Z_TILE = GB_TILE + 1
GC_TILE = Z_TILE + 1
HEADS_PER_TILE = COL_TILE // HEAD_DIM
K_COL = KV_TILE * COL_TILE
V_COL = K_COL + KV_WIDTH

SUB_COLS = 256
MOD_ROWS = 8
ADA_TILE = 1024
IN_TOKEN_TILE = 512
NORM_TOKEN_TILE = 1024
NORM_ROWS = 16
OUT_TOKEN_TILE = 256
ATTN_Q_TILE = 512
MIX_TILE = 512
SHORT_SEQ_ROWS = 1024
CAST_ROWS = 128
OUT_CAST_ROWS = 256
VMEM_LIMIT = 56 * 1024 * 1024


def _params(*sem):
    return pltpu.CompilerParams(dimension_semantics=sem, vmem_limit_bytes=VMEM_LIMIT)


def _resident(shape):
    return pl.BlockSpec(shape, lambda *_: (0,) * len(shape), pipeline_mode=pl.Buffered(1))


def _silu(x):
    return x * jax.nn.sigmoid(x)


def _rms(x, w):
    ms = jnp.mean(x * x, axis=-1, keepdims=True)
    return x * lax.rsqrt(ms + EPS) * w


class _Stream:
    def __init__(self, n_ctx, n_lat, seq, tm):
        assert n_ctx % tm == 0 and seq % tm == 0
        self.tm = tm
        self.ctx_tiles = n_ctx // tm
        self.tiles = (n_ctx + n_lat) // tm
        self.tiles_per_seq = seq // tm

    def mod_index(self, i):
        lat = 1 + (i - self.ctx_tiles) // self.tiles_per_seq
        return (jnp.where(i < self.ctx_tiles, 0, lat), 0, 0)

    def two_source_specs(self, width):
        ctx = pl.BlockSpec((self.tm, width), lambda i: (jnp.minimum(i, self.ctx_tiles - 1), 0))
        lat = pl.BlockSpec((self.tm, width), lambda i: (jnp.maximum(i - self.ctx_tiles, 0), 0))
        return [ctx, lat]


def _pick(is_ctx, ctx_ref, lat_ref):
    return jnp.where(is_ctx, ctx_ref[...], lat_ref[...])


def _load_weight_bf16(w_hbm, layer, w_scr, stage, sem):
    chunk = stage.shape[1]
    n = w_scr.shape[0] // chunk

    def copy(c):
        return pltpu.make_async_copy(w_hbm.at[layer, pl.ds(c * chunk, chunk), :], stage.at[c % 2], sem.at[c % 2])

    copy(0).start()
    for c in range(n):
        if c + 1 < n:
            copy(c + 1).start()
        copy(c).wait()
        w_scr[c * chunk:(c + 1) * chunk, :] = stage[c % 2].astype(BF16)


def _weight_scratch(rows, cols, chunk):
    return [pltpu.VMEM((rows, cols), BF16), pltpu.VMEM((2, chunk, cols), F32), pltpu.SemaphoreType.DMA((2,))]


HBM = pl.BlockSpec(memory_space=pl.ANY)


def _ada_kernel(cv_ref, w_ref, b_ref, o_ref):
    a = _silu(cv_ref[...]).astype(BF16)
    o_ref[...] = jnp.dot(a, w_ref[...].astype(BF16), preferred_element_type=F32) + b_ref[...]


def _ada_call(cv, w_ada, b_ada):
    return pl.pallas_call(
        _ada_kernel,
        out_shape=jax.ShapeDtypeStruct((DEPTH, MOD_ROWS, GATE_COLS), F32),
        grid=(DEPTH, GATE_COLS // ADA_TILE),
        in_specs=[
            pl.BlockSpec((MOD_ROWS, D_MODEL), lambda l, j: (0, 0)),
            pl.BlockSpec((None, D_MODEL, ADA_TILE), lambda l, j: (l, 0, j)),
            pl.BlockSpec((None, 1, ADA_TILE), lambda l, j: (l, 0, j)),
        ],
        out_specs=pl.BlockSpec((None, MOD_ROWS, ADA_TILE), lambda l, j: (l, 0, j)),
        compiler_params=_params("arbitrary", "arbitrary"),
        name="ada_mod",
    )(cv, w_ada, b_ada)


def _normmod_kernel(xc_ref, xl_ref, mod_ref, nw_ref, h_ref, *, ctx_tiles):
    def emit(x_ref):
        def chunk(r, carry):
            rows = pl.ds(pl.multiple_of(r * NORM_ROWS, NORM_ROWS), NORM_ROWS)
            y = _rms(x_ref[rows, :], nw_ref[...])
            h_ref[rows, :] = (y * (1.0 + mod_ref[:, D_MODEL:2 * D_MODEL]) + mod_ref[:, 0:D_MODEL]).astype(BF16)
            return carry

        lax.fori_loop(0, h_ref.shape[0] // NORM_ROWS, chunk, 0, unroll=8)

    @pl.when(pl.program_id(0) < ctx_tiles)
    def _():
        emit(xc_ref)

    @pl.when(pl.program_id(0) >= ctx_tiles)
    def _():
        emit(xl_ref)


def _normmod_call(x_ctx, x_lat, mod3, norm_w, *, seq):
    st = _Stream(x_ctx.shape[0], x_lat.shape[0], seq, NORM_TOKEN_TILE)
    return pl.pallas_call(
        functools.partial(_normmod_kernel, ctx_tiles=st.ctx_tiles),
        out_shape=jax.ShapeDtypeStruct((st.tiles * st.tm, D_MODEL), BF16),
        grid=(st.tiles,),
        in_specs=st.two_source_specs(D_MODEL) + [
            pl.BlockSpec((None, 1, GATE_COLS), st.mod_index),
            pl.BlockSpec((1, D_MODEL), lambda i: (0, 0)),
        ],
        out_specs=pl.BlockSpec((st.tm, D_MODEL), lambda i: (i, 0)),
        compiler_params=_params("arbitrary"),
        name="norm_mod",
    )(x_ctx, x_lat, mod3, norm_w)


def _gates_kernel(h_ref, w_hbm, b_ref, o_ref, w_ref, stage, sem, *, layer):
    @pl.when(pl.program_id(0) == 0)
    def _():
        _load_weight_bf16(w_hbm, layer, w_ref, stage, sem)

    for s in range(GATE_COLS // SUB_COLS):
        cs = slice(s * SUB_COLS, (s + 1) * SUB_COLS)
        acc = jnp.dot(h_ref[...], w_ref[:, cs], preferred_element_type=F32)
        o_ref[:, cs] = jax.nn.sigmoid(acc + b_ref[:, cs]).astype(BF16)


def _gates_call(h, w_merge, b_merge, *, layer):
    t = h.shape[0]
    tm = IN_TOKEN_TILE
    return pl.pallas_call(
        functools.partial(_gates_kernel, layer=layer),
        out_shape=jax.ShapeDtypeStruct((t, GATE_COLS), BF16),
        grid=(t // tm,),
        in_specs=[
            pl.BlockSpec((tm, D_MODEL), lambda i: (i, 0)),
            HBM,
            _resident((1, GATE_COLS)),
        ],
        out_specs=pl.BlockSpec((tm, GATE_COLS), lambda i: (i, 0)),
        scratch_shapes=_weight_scratch(D_MODEL, GATE_COLS, CAST_ROWS),
        compiler_params=_params("arbitrary"),
        name="gates",
    )(h, w_merge, b_merge)


def _rope(y, cos, sa, sb):
    return y * cos + pltpu.roll(y, 96, 1) * sa + pltpu.roll(y, 32, 1) * sb


def _proj_kernel(h_ref, w_hbm, qnw_ref, knw_ref, cos_ref, sa_ref, sb_ref, proj_ref, ks_ref, vs_ref,
                 w_ref, stage, sem, *, layer):
    @pl.when(pl.program_id(0) == 0)
    def _():
        _load_weight_bf16(w_hbm, layer, w_ref, stage, sem)

    def head(xh, w):
        return _rope(_rms(xh, w), cos_ref[...], sa_ref[...], sb_ref[...])

    qw = qnw_ref[...] * (ATTN_SCALE * LOG2_E)
    silu_cols = ((GA_TILE0 * COL_TILE, U_TILE * COL_TILE), (GB_TILE * COL_TILE, Z_TILE * COL_TILE),
                 (GC_TILE * COL_TILE, IN_COLS))
    for s in range(IN_COLS // SUB_COLS):
        acc = jnp.dot(h_ref[...], w_ref[:, s * SUB_COLS:(s + 1) * SUB_COLS], preferred_element_type=F32)
        for hb in range(SUB_COLS // HEAD_DIM):
            c0 = s * SUB_COLS + hb * HEAD_DIM
            cs = slice(c0, c0 + HEAD_DIM)
            a = acc[:, hb * HEAD_DIM:(hb + 1) * HEAD_DIM]
            if c0 < K_COL:
                proj_ref[:, cs] = head(a, qw).astype(BF16)
            elif c0 < V_COL:
                ks_ref[:, c0 - K_COL:c0 - K_COL + HEAD_DIM] = _rms(a, knw_ref[...])
                proj_ref[:, cs] = head(a, knw_ref[...]).astype(BF16)
            elif c0 < V_COL + KV_WIDTH:
                vs_ref[:, c0 - V_COL:c0 - V_COL + HEAD_DIM] = a
                proj_ref[:, cs] = a.astype(BF16)
            elif any(lo <= c0 < hi for lo, hi in silu_cols):
                proj_ref[:, cs] = _silu(a).astype(BF16)
            else:
                proj_ref[:, cs] = a.astype(BF16)


def _proj_call(h, w_in, qnw, knw, rope_tabs, *, st, layer):
    t = h.shape[0]
    tm = st.tm

    def tab_index(i):
        return (jnp.where(i < st.ctx_tiles, 0, 1 + (i - st.ctx_tiles) % st.tiles_per_seq), 0)

    in_specs = [
        pl.BlockSpec((tm, D_MODEL), lambda i: (i, 0)),
        HBM,
        pl.BlockSpec((1, HEAD_DIM), lambda i: (0, 0)),
        pl.BlockSpec((1, HEAD_DIM), lambda i: (0, 0)),
    ] + [pl.BlockSpec((tm, HEAD_DIM), tab_index) for _ in rope_tabs]
    out_shape = [jax.ShapeDtypeStruct((t, IN_COLS), BF16),
                 jax.ShapeDtypeStruct((t, KV_WIDTH), F32), jax.ShapeDtypeStruct((t, KV_WIDTH), F32)]
    out_specs = [pl.BlockSpec((tm, IN_COLS), lambda i: (i, 0)),
                 pl.BlockSpec((tm, KV_WIDTH), lambda i: (i, 0)), pl.BlockSpec((tm, KV_WIDTH), lambda i: (i, 0))]
    return pl.pallas_call(
        functools.partial(_proj_kernel, layer=layer),
        out_shape=out_shape,
        grid=(t // tm,),
        in_specs=in_specs,
        out_specs=out_specs,
        scratch_shapes=_weight_scratch(D_MODEL, IN_COLS, CAST_ROWS),
        compiler_params=_params("arbitrary"),
        name="proj",
    )(h, w_in, qnw, knw, *rope_tabs)


def _attn_kernel(*refs, ctx, n_seq):
    q_ref, k_ref, v_ref, ga0_ref, ga1_ref = refs[:5]
    ga_refs = (ga0_ref, ga1_ref)
    refs = refs[5:]
    if ctx:
        ck_ref, cv_ref = refs[:2]
        refs = refs[2:]
    o_ref, vx_scr = refs[:2]
    if ctx:
        ckx_scr, cvx_scr = refs[2:]
    seq = k_ref.shape[0] // n_seq
    tq = q_ref.shape[0] // n_seq

    @pl.when(pl.program_id(1) == 0)
    def _():
        for j in range(n_seq):
            for kh in range(A_KV_HEADS):
                hs = slice(kh * HEAD_DIM, (kh + 1) * HEAD_DIM)
                vx_scr[j, kh, :, :HEAD_DIM] = v_ref[j * seq:(j + 1) * seq, hs]
                vx_scr[j, kh, :, HEAD_DIM:] = jnp.ones((seq, HEAD_DIM), BF16)
        if ctx:
            for kh in range(A_KV_HEADS):
                ckx_scr[kh] = ck_ref[:, kh, :].astype(BF16)
                cvx_scr[kh, :, :HEAD_DIM] = cv_ref[:, kh, :].astype(BF16)
                cvx_scr[kh, :, HEAD_DIM:] = jnp.ones((cvx_scr.shape[1], HEAD_DIM), BF16)

    nt = (((1,), (1,)), ((), ()))
    for j in range(n_seq):
        qrows = slice(j * tq, (j + 1) * tq)
        for head in range(A_HEADS):
            kh, hh = divmod(head, HEADS_PER_TILE)
            cs = slice(head * HEAD_DIM, (head + 1) * HEAD_DIM)
            q = q_ref[qrows, cs]
            k = k_ref[j * seq:(j + 1) * seq, kh * HEAD_DIM:(kh + 1) * HEAD_DIM]
            s1 = lax.dot_general(q, k, nt, preferred_element_type=F32)
            m = jnp.max(s1, axis=-1, keepdims=True)
            if ctx:
                s2 = lax.dot_general(q, ckx_scr[kh], nt, preferred_element_type=F32)
                m = jnp.maximum(m, jnp.max(s2, axis=-1, keepdims=True))
            ox = jnp.dot(jnp.exp2(s1 - m).astype(BF16), vx_scr[j, kh], preferred_element_type=F32)
            if ctx:
                ox = ox + jnp.dot(jnp.exp2(s2 - m).astype(BF16), cvx_scr[kh], preferred_element_type=F32)
            o = ox[:, :HEAD_DIM] / ox[:, HEAD_DIM:]
            ga = ga_refs[kh][qrows, hh * HEAD_DIM:(hh + 1) * HEAD_DIM]
            o_ref[qrows, cs] = (o * ga.astype(F32)).astype(BF16)


def _attn_call(proj, cache, *, row0, n_batch, seq, layer):
    tq = min(ATTN_Q_TILE, seq)
    nq = seq // tq
    ctx = cache is not None
    n_seq = max(SHORT_SEQ_ROWS // seq, 1) if (nq == 1 and not ctx) else 1
    assert n_batch % n_seq == 0
    qb, sb = tq * n_seq, seq * n_seq
    q0, s0 = row0 // qb, row0 // sb
    in_specs = [
        pl.BlockSpec((qb, A_WIDTH), lambda b, qi: (q0 + b * nq + qi, Q_TILE0 * COL_TILE // A_WIDTH)),
        pl.BlockSpec((sb, KV_WIDTH), lambda b, qi: (s0 + b, K_COL // KV_WIDTH)),
        pl.BlockSpec((sb, KV_WIDTH), lambda b, qi: (s0 + b, V_COL // KV_WIDTH)),
        pl.BlockSpec((qb, COL_TILE), lambda b, qi: (q0 + b * nq + qi, GA_TILE0)),
        pl.BlockSpec((qb, COL_TILE), lambda b, qi: (q0 + b * nq + qi, GA_TILE0 + 1)),
    ]
    args = [proj, proj, proj, proj, proj]
    scratch = [pltpu.VMEM((n_seq, A_KV_HEADS, seq, 2 * HEAD_DIM), BF16)]
    if ctx:
        past = cache[0].shape[2]
        for c in cache:
            in_specs.append(pl.BlockSpec((None, None, past, A_KV_HEADS, HEAD_DIM),
                                         lambda b, qi: (b, layer, 0, 0, 0)))
            args.append(c)
        scratch += [pltpu.VMEM((A_KV_HEADS, past, HEAD_DIM), BF16),
                    pltpu.VMEM((A_KV_HEADS, past, 2 * HEAD_DIM), BF16)]
    return pl.pallas_call(
        functools.partial(_attn_kernel, ctx=ctx, n_seq=n_seq),
        out_shape=jax.ShapeDtypeStruct((n_batch * seq, A_WIDTH), BF16),
        grid=(n_batch // n_seq, nq),
        in_specs=in_specs,
        out_specs=pl.BlockSpec((qb, A_WIDTH), lambda b, qi: (b * nq + qi, 0)),
        scratch_shapes=scratch,
        compiler_params=_params("arbitrary", "arbitrary"),
        name="attention_lat" if ctx else "attention_ctx",
    )(*args)


def _mix_kernel(t_ref, z_ref, snw_ref, ws_ref, bs_ref, wp_ref, ps_ref, band_ref, inv_ref,
                bo_ref, co_ref, zb_scr, *, n_seq):
    u_ref, vb_ref, gb_ref, gc_ref = (t_ref.at[:, (k - U_TILE) * COL_TILE:(k - U_TILE + 1) * COL_TILE]
                                     for k in (U_TILE, VB_TILE, GB_TILE, GC_TILE))
    r = t_ref.shape[0] // n_seq
    c = pl.program_id(1)
    nc = pl.num_programs(1)
    base = pl.multiple_of(c * r, r)
    n_chunks = r // CHUNK
    z_rows = z_ref.shape[0]
    for j in range(n_seq):
        row0 = j * r

        vbn = _rms(vb_ref[row0:row0 + r, :].astype(F32), snw_ref[...]).astype(BF16)
        for g in range(B_GROUPS):
            cs = slice(g * HEAD_DIM, (g + 1) * HEAD_DIM)
            wide = jnp.concatenate([vbn[cc * CHUNK:(cc + 1) * CHUNK, cs] for cc in range(n_chunks)], axis=1)
            mixed = jnp.dot(ws_ref[g], wide, preferred_element_type=F32) + bs_ref[g]
            for cc in range(n_chunks):
                rs = slice(row0 + cc * CHUNK, row0 + (cc + 1) * CHUNK)
                mx = mixed[:, cc * HEAD_DIM:(cc + 1) * HEAD_DIM]
                bo_ref[rs, cs] = (u_ref[rs, cs].astype(F32) * mx * gb_ref[rs, cs].astype(F32)).astype(BF16)

        zb = zb_scr.at[j]
        z0 = base + row0
        zb[POOL_HALO:POOL_HALO + r, :] = z_ref[pl.ds(pl.multiple_of(z0, POOL_HALO), r), :]
        above = z_ref[pl.ds(pl.multiple_of(jnp.maximum(z0 - POOL_HALO, 0), POOL_HALO), POOL_HALO), :]
        below = z_ref[pl.ds(pl.multiple_of(jnp.minimum(z0 + r, z_rows - POOL_HALO), POOL_HALO), POOL_HALO), :]
        zeros = jnp.zeros((POOL_HALO, C_WIDTH), BF16)
        zb[0:POOL_HALO, :] = jnp.where(c > 0, above, zeros)
        zb[POOL_HALO + r:, :] = jnp.where(c < nc - 1, below, zeros)

        for g in range(len(POOL_WINDOWS)):
            cs = slice(g * HEAD_DIM, (g + 1) * HEAD_DIM)
            ds = []
            for blk in range(n_chunks):
                lo = blk * CHUNK
                wsum = jnp.dot(band_ref[g], zb[lo:lo + CHUNK + 2 * POOL_HALO, cs], preferred_element_type=F32)
                zc = zb[POOL_HALO + lo:POOL_HALO + lo + CHUNK, cs].astype(F32)
                ds.append((wsum * inv_ref[lo:lo + CHUNK, cs] - zc).astype(BF16))
            dm = jnp.dot(jnp.concatenate(ds, axis=0), wp_ref[g], preferred_element_type=F32)
            rows = slice(row0, row0 + r)
            co_ref[rows, cs] = (dm * ps_ref[:, cs] * gc_ref[rows, cs].astype(F32)).astype(BF16)


def _pool_band():
    t = np.arange(CHUNK)[:, None] + POOL_HALO
    j = np.arange(CHUNK + 2 * POOL_HALO)[None, :]
    return jnp.asarray(np.stack([(j >= t - w // 2) & (j < t + w - w // 2) for w in POOL_WINDOWS]), dtype=BF16)


def _pool_inv_count(seq):
    t = np.arange(seq)
    cols = [1.0 / (np.clip(t + w - w // 2, 0, seq) - np.clip(t - w // 2, 0, seq)) for w in POOL_WINDOWS]
    return jnp.asarray(np.repeat(np.stack(cols, axis=1), HEAD_DIM, axis=1), dtype=F32)


def _mix_call(proj, snw, ws, bs, wp, ps, *, row0, n_batch, seq):
    r = min(MIX_TILE, seq)
    nc = seq // r
    n_seq = max(SHORT_SEQ_ROWS // seq, 1) if nc == 1 else 1
    assert n_batch % n_seq == 0
    rb, sb = r * n_seq, seq * n_seq
    r0, s0 = row0 // rb, row0 // sb
    n_mix_cols = IN_COLS - U_TILE * COL_TILE
    assert (U_TILE * COL_TILE) % n_mix_cols == 0
    const3 = lambda shape: pl.BlockSpec(shape, lambda b, c: (0, 0, 0))
    out_spec = pl.BlockSpec((rb, COL_TILE), lambda b, c: (b * nc + c, 0))
    return pl.pallas_call(
        functools.partial(_mix_kernel, n_seq=n_seq),
        out_shape=[jax.ShapeDtypeStruct((n_batch * seq, B_WIDTH), BF16),
                   jax.ShapeDtypeStruct((n_batch * seq, C_WIDTH), BF16)],
        grid=(n_batch // n_seq, nc),
        in_specs=[
            pl.BlockSpec((rb, n_mix_cols), lambda b, c: (r0 + b * nc + c, U_TILE * COL_TILE // n_mix_cols)),
            pl.BlockSpec((sb, COL_TILE), lambda b, c: (s0 + b, Z_TILE)),
            pl.BlockSpec((1, B_WIDTH), lambda b, c: (0, 0)),
            const3((B_GROUPS, CHUNK, CHUNK)),
            const3((B_GROUPS, CHUNK, 1)),
            const3((B_GROUPS, HEAD_DIM, HEAD_DIM)),
            pl.BlockSpec((1, C_WIDTH), lambda b, c: (0, 0)),
            const3((len(POOL_WINDOWS), CHUNK, CHUNK + 2 * POOL_HALO)),
            pl.BlockSpec((r, C_WIDTH), lambda b, c: (c, 0)),
        ],
        out_specs=[out_spec, out_spec],
        scratch_shapes=[pltpu.VMEM((n_seq, r + 2 * POOL_HALO, C_WIDTH), BF16)],
        compiler_params=_params("arbitrary", "arbitrary"),
        name="sgu_pool",
    )(proj, proj, snw, ws, bs, wp, ps, _pool_band(), _pool_inv_count(seq))


def _outproj_kernel(*refs, first, final, ctx_tiles, layer):
    (ac_ref, al_ref, bc_ref, bl_ref, cc_ref, cl_ref, g_ref) = refs[:7]
    refs = refs[7:]
    if first:
        xc_ref, xl_ref = refs[:2]
        refs = refs[2:]
    else:
        x_ref = refs[0]
        refs = refs[1:]
    mod_ref, wa_hbm, wb_hbm, wc_hbm, wo_hbm, nw_ref = refs[:6]
    refs = refs[6:]
    if final:
        yc_ref, yl_ref = refs[:2]
        refs = refs[2:]
    else:
        modn_ref, y_ref, hn_ref = refs[:3]
        refs = refs[3:]
    a_scr, b_scr, c_scr, m_scr, wa_ref, wb_ref, wc_ref, wo_ref, stage, sem = refs[:10]
    y_scr = refs[10] if final else y_ref

    @pl.when(pl.program_id(0) == 0)
    def _():
        for w_hbm, w_ref in ((wa_hbm, wa_ref), (wb_hbm, wb_ref), (wc_hbm, wc_ref), (wo_hbm, wo_ref)):
            _load_weight_bf16(w_hbm, layer, w_ref, stage, sem)

    is_ctx = pl.program_id(0) < ctx_tiles
    a_scr[...] = _pick(is_ctx, ac_ref, al_ref)
    b_scr[...] = _pick(is_ctx, bc_ref, bl_ref)
    c_scr[...] = _pick(is_ctx, cc_ref, cl_ref)
    for n in range(D_MODEL // COL_TILE):
        cs = slice(n * COL_TILE, (n + 1) * COL_TILE)
        a = jnp.dot(a_scr[...], wa_ref[:, cs], preferred_element_type=F32)
        b = jnp.dot(b_scr[...], wb_ref[:, cs], preferred_element_type=F32)
        c = jnp.dot(c_scr[...], wc_ref[:, cs], preferred_element_type=F32)
        g0, g1, g2 = (g_ref[:, k * D_MODEL + n * COL_TILE:k * D_MODEL + (n + 1) * COL_TILE].astype(F32)
                      for k in range(N_BRANCH))
        m = g0 * a + g1 * b + g2 * c
        m_scr[:, cs] = m.astype(BF16)
    ssq = jnp.zeros((m_scr.shape[0], 1), F32)
    for n in range(D_MODEL // COL_TILE):
        cs = slice(n * COL_TILE, (n + 1) * COL_TILE)
        out = jnp.dot(m_scr[...], wo_ref[:, cs], preferred_element_type=F32)
        x = jnp.where(is_ctx, xc_ref[:, cs], xl_ref[:, cs]) if first else x_ref[:, cs]
        y = x + mod_ref[:, 2 * D_MODEL + n * COL_TILE:2 * D_MODEL + (n + 1) * COL_TILE] * out
        y_scr[:, cs] = y
        ssq = ssq + jnp.sum(y * y, axis=-1, keepdims=True)
    yn = y_scr[...] * lax.rsqrt(ssq * (1.0 / D_MODEL) + EPS) * nw_ref[...]
    if final:
        @pl.when(is_ctx)
        def _():
            yc_ref[...] = yn

        @pl.when(jnp.logical_not(is_ctx))
        def _():
            yl_ref[...] = yn
    else:
        hn_ref[...] = (yn * (1.0 + modn_ref[:, D_MODEL:2 * D_MODEL]) + modn_ref[:, 0:D_MODEL]).astype(BF16)


def _outproj_call(attn, bout, cout, gates, x, mod3, wa, wb, wc, wo, nw, modn3, *, st, layer):
    tm = st.tm
    first = isinstance(x, tuple)
    final = modn3 is None
    row = pl.BlockSpec((tm, D_MODEL), lambda i: (i, 0))
    in_specs = (st.two_source_specs(A_WIDTH) + st.two_source_specs(B_WIDTH) + st.two_source_specs(C_WIDTH)
                + [pl.BlockSpec((tm, GATE_COLS), lambda i: (i, 0))]
                + (st.two_source_specs(D_MODEL) if first else [row])
                + [pl.BlockSpec((None, 1, GATE_COLS), st.mod_index),
                   HBM, HBM, HBM, HBM,
                   pl.BlockSpec((1, D_MODEL), lambda i: (0, 0))])
    args = [*attn, *bout, *cout, gates, *(x if first else (x,)), mod3, wa, wb, wc, wo, nw]
    scratch = [pltpu.VMEM((tm, A_WIDTH), BF16), pltpu.VMEM((tm, B_WIDTH), BF16),
               pltpu.VMEM((tm, C_WIDTH), BF16), pltpu.VMEM((tm, D_MODEL), BF16),
               pltpu.VMEM((A_WIDTH, D_MODEL), BF16), pltpu.VMEM((B_WIDTH, D_MODEL), BF16),
               pltpu.VMEM((C_WIDTH, D_MODEL), BF16), pltpu.VMEM((D_MODEL, D_MODEL), BF16),
               pltpu.VMEM((2, OUT_CAST_ROWS, D_MODEL), F32), pltpu.SemaphoreType.DMA((2,))]
    if final:
        out_shape = [jax.ShapeDtypeStruct((st.ctx_tiles * tm, D_MODEL), F32),
                     jax.ShapeDtypeStruct(((st.tiles - st.ctx_tiles) * tm, D_MODEL), F32)]
        out_specs = st.two_source_specs(D_MODEL)
        scratch.append(pltpu.VMEM((tm, D_MODEL), F32))
    else:
        in_specs.append(pl.BlockSpec((None, 1, GATE_COLS), st.mod_index))
        args.append(modn3)
        out_shape = [jax.ShapeDtypeStruct((st.tiles * tm, D_MODEL), F32),
                     jax.ShapeDtypeStruct((st.tiles * tm, D_MODEL), BF16)]
        out_specs = [row, row]
    return pl.pallas_call(
        functools.partial(_outproj_kernel, first=first, final=final, ctx_tiles=st.ctx_tiles, layer=layer),
        out_shape=out_shape,
        grid=(st.tiles,),
        in_specs=in_specs,
        out_specs=out_specs,
        scratch_shapes=scratch,
        compiler_params=_params("arbitrary"),
        name="out_proj_final" if final else "out_proj",
    )(*args)


def _rope_tables(n_identity, n_tokens):
    rows = n_tokens // GRID_W
    row = np.repeat(np.arange(rows), GRID_W).astype(np.float64)
    col = np.tile(np.arange(GRID_W), rows).astype(np.float64)
    n_freq = HEAD_DIM // 4
    inv = ROPE_THETA ** (-np.arange(n_freq, dtype=np.float64) / n_freq)
    ar = row[:, None] * inv[None, :]
    ac = col[:, None] * inv[None, :]
    zero = np.zeros_like(ar)
    cos = np.concatenate([np.cos(ar), np.cos(ar), np.cos(ac), np.cos(ac)], axis=-1)
    sa = np.concatenate([-np.sin(ar), zero, -np.sin(ac), zero], axis=-1)
    sb = np.concatenate([zero, np.sin(ar), zero, np.sin(ac)], axis=-1)
    ident = (np.ones((n_identity, HEAD_DIM)), np.zeros((n_identity, HEAD_DIM)), np.zeros((n_identity, HEAD_DIM)))
    return tuple(jnp.asarray(np.concatenate([i, t], axis=0), dtype=F32) for i, t in zip(ident, (cos, sa, sb)))


def kernel(x_prompt, x_sample, cache_k, cache_v, c, c_ctx, norm_w, w_ada, b_ada, w_in, q_norm_w,
           k_norm_w, sgu_norm_w, w_sgu, b_sgu, w_pool, pool_scale, w_br_a, w_br_b, w_br_c, w_merge,
           b_merge, w_out, final_norm_w):
    nb_p, seq_p, d = x_prompt.shape
    nb_s, seq_s, _ = x_sample.shape
    n_ctx, n_lat = nb_p * seq_p, nb_s * seq_s
    assert d == D_MODEL and nb_s + 1 <= MOD_ROWS

    cv = jnp.concatenate([c_ctx[None, :], c, jnp.zeros((MOD_ROWS - 1 - nb_s, d), F32)], axis=0)
    mod = _ada_call(cv, w_ada, b_ada.reshape(DEPTH, 1, GATE_COLS))
    mod3 = [mod[l].reshape(MOD_ROWS, 1, GATE_COLS) for l in range(DEPTH)]
    st_in = _Stream(n_ctx, n_lat, seq_s, IN_TOKEN_TILE)
    st_out = _Stream(n_ctx, n_lat, seq_s, OUT_TOKEN_TILE)
    rope_tabs = _rope_tables(IN_TOKEN_TILE, seq_s)
    cache = (cache_k, cache_v)

    x = (x_prompt.reshape(n_ctx, d), x_sample.reshape(n_lat, d))
    h = _normmod_call(*x, mod3[0], norm_w[0].reshape(1, d), seq=seq_s)
    states = []
    w_mg, w_i, wa, wb, wc, wo = w_merge, w_in, w_br_a, w_br_b, w_br_c, w_out
    for l in range(DEPTH):
        last = l == DEPTH - 1
        bm = b_merge[l].reshape(1, GATE_COLS)
        qnw = q_norm_w[l].reshape(1, HEAD_DIM)
        knw = k_norm_w[l].reshape(1, HEAD_DIM)
        mix_w = (sgu_norm_w[l].reshape(1, B_WIDTH), w_sgu[l].astype(BF16),
                 b_sgu[l].reshape(B_GROUPS, CHUNK, 1), w_pool[l].astype(BF16),
                 pool_scale[l].reshape(1, C_WIDTH))
        nw_next = (final_norm_w if last else norm_w[l + 1]).reshape(1, d)

        gates = _gates_call(h, w_mg, bm, layer=l)
        proj, ks, vs = _proj_call(h, w_i, qnw, knw, rope_tabs, st=st_in, layer=l)
        states.append((ks, vs))
        attn = (_attn_call(proj, None, row0=0, n_batch=nb_p, seq=seq_p, layer=l),
                _attn_call(proj, cache, row0=n_ctx, n_batch=nb_s, seq=seq_s, layer=l))
        mix_c = _mix_call(proj, *mix_w, row0=0, n_batch=nb_p, seq=seq_p)
        mix_l = _mix_call(proj, *mix_w, row0=n_ctx, n_batch=nb_s, seq=seq_s)
        res = _outproj_call(attn, (mix_c[0], mix_l[0]), (mix_c[1], mix_l[1]), gates, x, mod3[l],
                            wa, wb, wc, wo, nw_next, None if last else mod3[l + 1], st=st_out, layer=l)
        if last:
            y_ctx, y_lat = res
        else:
            x, h = res

    state_k, state_v = (
        jnp.stack([s[i][:n_ctx].reshape(nb_p, seq_p, A_KV_HEADS, HEAD_DIM) for s in states], axis=1)
        for i in range(2))
    return (y_ctx.reshape(nb_p, seq_p, d), y_lat.reshape(nb_s, seq_s, d), state_k, state_v)
```

```python
import functools

import jax
import jax.numpy as jnp
import numpy as np
from jax import lax
from jax.experimental import pallas as pl
from jax.experimental.pallas import tpu as pltpu

F32 = jnp.float32
BF16 = jnp.bfloat16

D_MODEL = 2048
DEPTH = 2
GRID_W = 64
EPS = 1e-6
HEAD_DIM = 128
A_HEADS = 8
A_KV_HEADS = 2
A_WIDTH = A_HEADS * HEAD_DIM
KV_WIDTH = A_KV_HEADS * HEAD_DIM
ROPE_THETA = 10000.0
ROPE_PARTNER = HEAD_DIM // 4
ATTN_SCALE = HEAD_DIM ** -0.5
LOG2_E = 1.4426950408889634
CHUNK = 128
B_GROUPS = 4
B_WIDTH = 512
C_WIDTH = 512
POOL_WINDOWS = (2, 4, 8, 16)
POOL_HALO = 64
N_BRANCH = 3
GATE_COLS = N_BRANCH * D_MODEL
IN_COLS = 2 * A_WIDTH + 2 * KV_WIDTH + 3 * B_WIDTH + 2 * C_WIDTH

COL_TILE = 512
Q_TILE0 = 0
KV_TILE = Q_TILE0 + A_WIDTH // COL_TILE
GA_TILE0 = KV_TILE + 1
U_TILE = GA_TILE0 + A_WIDTH // COL_TILE
VB_TILE = U_TILE + 1
GB_TILE = VB_TILE + 1
Z_TILE = GB_TILE + 1
GC_TILE = Z_TILE + 1
HEADS_PER_TILE = COL_TILE // HEAD_DIM
K_COL = KV_TILE * COL_TILE
V_COL = K_COL + KV_WIDTH

SUB_COLS = 256
MOD_ROWS = 8
ADA_TILE = 1024
IN_TOKEN_TILE = 512
NORM_TOKEN_TILE = 1024
NORM_ROWS = 16
OUT_TOKEN_TILE = 256
SEQ_TILE = 512
SHORT_SEQ_ROWS = 1024
CAST_ROWS = 128
OUT_CAST_ROWS = 256
VMEM_LIMIT = 56 * 1024 * 1024


def _params(*sem):
    return pltpu.CompilerParams(dimension_semantics=sem, vmem_limit_bytes=VMEM_LIMIT)


def _resident(shape):
    return pl.BlockSpec(shape, lambda *_: (0,) * len(shape), pipeline_mode=pl.Buffered(1))


def _silu(x):
    return x * jax.nn.sigmoid(x)


def _rms(x, w):
    ms = jnp.mean(x * x, axis=-1, keepdims=True)
    return x * lax.rsqrt(ms + EPS) * w


class _Stream:
    def __init__(self, n_ctx, n_lat, seq, tm):
        assert n_ctx % tm == 0 and seq % tm == 0
        self.tm = tm
        self.ctx_tiles = n_ctx // tm
        self.tiles = (n_ctx + n_lat) // tm
        self.tiles_per_seq = seq // tm

    def mod_index(self, i):
        lat = 1 + (i - self.ctx_tiles) // self.tiles_per_seq
        return (jnp.where(i < self.ctx_tiles, 0, lat), 0, 0)

    def two_source_specs(self, width):
        ctx = pl.BlockSpec((self.tm, width), lambda i: (jnp.minimum(i, self.ctx_tiles - 1), 0))
        lat = pl.BlockSpec((self.tm, width), lambda i: (jnp.maximum(i - self.ctx_tiles, 0), 0))
        return [ctx, lat]


def _pick(is_ctx, ctx_ref, lat_ref):
    return jnp.where(is_ctx, ctx_ref[...], lat_ref[...])


def _load_weight_bf16(w_hbm, layer, w_scr, stage, sem):
    chunk = stage.shape[1]
    n = w_scr.shape[0] // chunk

    def copy(c):
        return pltpu.make_async_copy(w_hbm.at[layer, pl.ds(c * chunk, chunk), :], stage.at[c % 2], sem.at[c % 2])

    copy(0).start()
    for c in range(n):
        if c + 1 < n:
            copy(c + 1).start()
        copy(c).wait()
        w_scr[c * chunk:(c + 1) * chunk, :] = stage[c % 2].astype(BF16)


def _weight_scratch(rows, cols, chunk):
    return [pltpu.VMEM((rows, cols), BF16), pltpu.VMEM((2, chunk, cols), F32), pltpu.SemaphoreType.DMA((2,))]


HBM = pl.BlockSpec(memory_space=pl.ANY)


def _ada_kernel(cv_ref, w_ref, b_ref, o_ref):
    a = _silu(cv_ref[...]).astype(BF16)
    o_ref[...] = jnp.dot(a, w_ref[...].astype(BF16), preferred_element_type=F32) + b_ref[...]


def _ada_call(cv, w_ada, b_ada):
    return pl.pallas_call(
        _ada_kernel,
        out_shape=jax.ShapeDtypeStruct((DEPTH, MOD_ROWS, GATE_COLS), F32),
        grid=(DEPTH, GATE_COLS // ADA_TILE),
        in_specs=[
            pl.BlockSpec((MOD_ROWS, D_MODEL), lambda l, j: (0, 0)),
            pl.BlockSpec((None, D_MODEL, ADA_TILE), lambda l, j: (l, 0, j)),
            pl.BlockSpec((None, 1, ADA_TILE), lambda l, j: (l, 0, j)),
        ],
        out_specs=pl.BlockSpec((None, MOD_ROWS, ADA_TILE), lambda l, j: (l, 0, j)),
        compiler_params=_params("arbitrary", "arbitrary"),
        name="ada_mod",
    )(cv, w_ada, b_ada)


def _normmod_kernel(xc_ref, xl_ref, mod_ref, nw_ref, h_ref, *, ctx_tiles):
    def emit(x_ref):
        def chunk(r, carry):
            rows = pl.ds(pl.multiple_of(r * NORM_ROWS, NORM_ROWS), NORM_ROWS)
            y = _rms(x_ref[rows, :], nw_ref[...])
            h_ref[rows, :] = (y * (1.0 + mod_ref[:, D_MODEL:2 * D_MODEL]) + mod_ref[:, 0:D_MODEL]).astype(BF16)
            return carry

        lax.fori_loop(0, h_ref.shape[0] // NORM_ROWS, chunk, 0, unroll=8)

    @pl.when(pl.program_id(0) < ctx_tiles)
    def _():
        emit(xc_ref)

    @pl.when(pl.program_id(0) >= ctx_tiles)
    def _():
        emit(xl_ref)


def _normmod_call(x_ctx, x_lat, mod3, norm_w, *, seq):
    st = _Stream(x_ctx.shape[0], x_lat.shape[0], seq, NORM_TOKEN_TILE)
    return pl.pallas_call(
        functools.partial(_normmod_kernel, ctx_tiles=st.ctx_tiles),
        out_shape=jax.ShapeDtypeStruct((st.tiles * st.tm, D_MODEL), BF16),
        grid=(st.tiles,),
        in_specs=st.two_source_specs(D_MODEL) + [
            pl.BlockSpec((None, 1, GATE_COLS), st.mod_index),
            pl.BlockSpec((1, D_MODEL), lambda i: (0, 0)),
        ],
        out_specs=pl.BlockSpec((st.tm, D_MODEL), lambda i: (i, 0)),
        compiler_params=_params("arbitrary"),
        name="norm_mod",
    )(x_ctx, x_lat, mod3, norm_w)


def _gates_kernel(h_ref, w_hbm, b_ref, o_ref, w_ref, stage, sem, *, layer):
    @pl.when(pl.program_id(0) == 0)
    def _():
        _load_weight_bf16(w_hbm, layer, w_ref, stage, sem)

    for s in range(GATE_COLS // SUB_COLS):
        cs = slice(s * SUB_COLS, (s + 1) * SUB_COLS)
        acc = jnp.dot(h_ref[...], w_ref[:, cs], preferred_element_type=F32)
        o_ref[:, cs] = jax.nn.sigmoid(acc + b_ref[:, cs]).astype(BF16)


def _gates_call(h, w_merge, b_merge, *, layer):
    t = h.shape[0]
    tm = IN_TOKEN_TILE
    return pl.pallas_call(
        functools.partial(_gates_kernel, layer=layer),
        out_shape=jax.ShapeDtypeStruct((t, GATE_COLS), BF16),
        grid=(t // tm,),
        in_specs=[
            pl.BlockSpec((tm, D_MODEL), lambda i: (i, 0)),
            HBM,
            _resident((1, GATE_COLS)),
        ],
        out_specs=pl.BlockSpec((tm, GATE_COLS), lambda i: (i, 0)),
        scratch_shapes=_weight_scratch(D_MODEL, GATE_COLS, CAST_ROWS),
        compiler_params=_params("arbitrary"),
        name="gates",
    )(h, w_merge, b_merge)


def _rope(y, cos, sa, sb):
    return (y * cos + pltpu.roll(y, HEAD_DIM - ROPE_PARTNER, 1) * sa + pltpu.roll(y, ROPE_PARTNER, 1) * sb)


def _proj_kernel(h_ref, w_hbm, qnw_ref, knw_ref, cos_ref, sa_ref, sb_ref, proj_ref, ks_ref, vs_ref,
                 w_ref, stage, sem, *, layer):
    @pl.when(pl.program_id(0) == 0)
    def _():
        _load_weight_bf16(w_hbm, layer, w_ref, stage, sem)

    def head(xh, w):
        return _rope(_rms(xh, w), cos_ref[...], sa_ref[...], sb_ref[...])

    qw = qnw_ref[...] * (ATTN_SCALE * LOG2_E)
    silu_cols = ((GA_TILE0 * COL_TILE, U_TILE * COL_TILE), (GB_TILE * COL_TILE, Z_TILE * COL_TILE),
                 (GC_TILE * COL_TILE, IN_COLS))
    for s in range(IN_COLS // SUB_COLS):
        acc = jnp.dot(h_ref[...], w_ref[:, s * SUB_COLS:(s + 1) * SUB_COLS], preferred_element_type=F32)
        for hb in range(SUB_COLS // HEAD_DIM):
            c0 = s * SUB_COLS + hb * HEAD_DIM
            cs = slice(c0, c0 + HEAD_DIM)
            a = acc[:, hb * HEAD_DIM:(hb + 1) * HEAD_DIM]
            if c0 < K_COL:
                proj_ref[:, cs] = head(a, qw).astype(BF16)
            elif c0 < V_COL:
                ks_ref[:, c0 - K_COL:c0 - K_COL + HEAD_DIM] = _rms(a, knw_ref[...])
                proj_ref[:, cs] = head(a, knw_ref[...]).astype(BF16)
            elif c0 < V_COL + KV_WIDTH:
                vs_ref[:, c0 - V_COL:c0 - V_COL + HEAD_DIM] = a
                proj_ref[:, cs] = a.astype(BF16)
            elif any(lo <= c0 < hi for lo, hi in silu_cols):
                proj_ref[:, cs] = _silu(a).astype(BF16)
            else:
                proj_ref[:, cs] = a.astype(BF16)


def _proj_call(h, w_in, qnw, knw, rope_tabs, *, st, layer):
    t = h.shape[0]
    tm = st.tm

    def tab_index(i):
        return (jnp.where(i < st.ctx_tiles, 0, 1 + (i - st.ctx_tiles) % st.tiles_per_seq), 0)

    in_specs = [
        pl.BlockSpec((tm, D_MODEL), lambda i: (i, 0)),
        HBM,
        pl.BlockSpec((1, HEAD_DIM), lambda i: (0, 0)),
        pl.BlockSpec((1, HEAD_DIM), lambda i: (0, 0)),
    ] + [pl.BlockSpec((tm, HEAD_DIM), tab_index) for _ in rope_tabs]
    out_shape = [jax.ShapeDtypeStruct((t, IN_COLS), BF16),
                 jax.ShapeDtypeStruct((t, KV_WIDTH), F32), jax.ShapeDtypeStruct((t, KV_WIDTH), F32)]
    out_specs = [pl.BlockSpec((tm, IN_COLS), lambda i: (i, 0)),
                 pl.BlockSpec((tm, KV_WIDTH), lambda i: (i, 0)), pl.BlockSpec((tm, KV_WIDTH), lambda i: (i, 0))]
    return pl.pallas_call(
        functools.partial(_proj_kernel, layer=layer),
        out_shape=out_shape,
        grid=(t // tm,),
        in_specs=in_specs,
        out_specs=out_specs,
        scratch_shapes=_weight_scratch(D_MODEL, IN_COLS, CAST_ROWS),
        compiler_params=_params("arbitrary"),
        name="proj",
    )(h, w_in, qnw, knw, *rope_tabs)


def _attn_kernel(*refs, ctx, n_seq):
    q_ref, k_ref, v_ref, ga0_ref, ga1_ref = refs[:5]
    ga_refs = (ga0_ref, ga1_ref)
    refs = refs[5:]
    if ctx:
        ck_ref, cv_ref = refs[:2]
        refs = refs[2:]
    o_ref, vx_scr = refs[:2]
    if ctx:
        ckx_scr, cvx_scr = refs[2:]
    seq = k_ref.shape[0] // n_seq
    tq = q_ref.shape[0] // n_seq

    @pl.when(pl.program_id(1) == 0)
    def _():
        for j in range(n_seq):
            for kh in range(A_KV_HEADS):
                hs = slice(kh * HEAD_DIM, (kh + 1) * HEAD_DIM)
                vx_scr[j, kh, :, :HEAD_DIM] = v_ref[j * seq:(j + 1) * seq, hs]
                vx_scr[j, kh, :, HEAD_DIM:] = jnp.ones((seq, HEAD_DIM), BF16)
        if ctx:
            for kh in range(A_KV_HEADS):
                ckx_scr[kh] = ck_ref[:, kh, :].astype(BF16)
                cvx_scr[kh, :, :HEAD_DIM] = cv_ref[:, kh, :].astype(BF16)
                cvx_scr[kh, :, HEAD_DIM:] = jnp.ones((cvx_scr.shape[1], HEAD_DIM), BF16)

    nt = (((1,), (1,)), ((), ()))
    for j in range(n_seq):
        qrows = slice(j * tq, (j + 1) * tq)
        for head in range(A_HEADS):
            kh, hh = divmod(head, HEADS_PER_TILE)
            cs = slice(head * HEAD_DIM, (head + 1) * HEAD_DIM)
            q = q_ref[qrows, cs]
            k = k_ref[j * seq:(j + 1) * seq, kh * HEAD_DIM:(kh + 1) * HEAD_DIM]
            s1 = lax.dot_general(q, k, nt, preferred_element_type=F32)
            m = jnp.max(s1, axis=-1, keepdims=True)
            if ctx:
                s2 = lax.dot_general(q, ckx_scr[kh], nt, preferred_element_type=F32)
                m = jnp.maximum(m, jnp.max(s2, axis=-1, keepdims=True))
            ox = jnp.dot(jnp.exp2(s1 - m).astype(BF16), vx_scr[j, kh], preferred_element_type=F32)
            if ctx:
                ox = ox + jnp.dot(jnp.exp2(s2 - m).astype(BF16), cvx_scr[kh], preferred_element_type=F32)
            o = ox[:, :HEAD_DIM] / ox[:, HEAD_DIM:]
            ga = ga_refs[kh][qrows, hh * HEAD_DIM:(hh + 1) * HEAD_DIM]
            o_ref[qrows, cs] = (o * ga.astype(F32)).astype(BF16)


def _attn_specs(proj, cache, *, row0, n_batch, seq, layer, n_seq):
    tq = min(SEQ_TILE, seq)
    nq = seq // tq
    qb, sb = tq * n_seq, seq * n_seq
    q0, s0 = row0 // qb, row0 // sb
    in_specs = [
        pl.BlockSpec((qb, A_WIDTH), lambda b, qi: (q0 + b * nq + qi, Q_TILE0 * COL_TILE // A_WIDTH)),
        pl.BlockSpec((sb, KV_WIDTH), lambda b, qi: (s0 + b, K_COL // KV_WIDTH)),
        pl.BlockSpec((sb, KV_WIDTH), lambda b, qi: (s0 + b, V_COL // KV_WIDTH)),
        pl.BlockSpec((qb, COL_TILE), lambda b, qi: (q0 + b * nq + qi, GA_TILE0)),
        pl.BlockSpec((qb, COL_TILE), lambda b, qi: (q0 + b * nq + qi, GA_TILE0 + 1)),
    ]
    args = [proj, proj, proj, proj, proj]
    scratch = [pltpu.VMEM((n_seq, A_KV_HEADS, seq, 2 * HEAD_DIM), BF16)]
    if cache is not None:
        past = cache[0].shape[2]
        for c in cache:
            in_specs.append(pl.BlockSpec((None, None, past, A_KV_HEADS, HEAD_DIM),
                                         lambda b, qi: (b, layer, 0, 0, 0)))
            args.append(c)
        scratch += [pltpu.VMEM((A_KV_HEADS, past, HEAD_DIM), BF16),
                    pltpu.VMEM((A_KV_HEADS, past, 2 * HEAD_DIM), BF16)]
    out_shape = [jax.ShapeDtypeStruct((n_batch * seq, A_WIDTH), BF16)]
    out_specs = [pl.BlockSpec((qb, A_WIDTH), lambda b, qi: (b * nq + qi, 0))]
    return in_specs, args, out_shape, out_specs, scratch


def _mix_kernel(u_ref, vb_ref, gb_ref, z_ref, gc_ref, snw_ref, ws_ref, bs_ref, wp_ref, ps_ref, band_ref, inv_ref,
                bo_ref, co_ref, zb_scr, *, n_seq):
    r = u_ref.shape[0] // n_seq
    c = pl.program_id(1)
    nc = pl.num_programs(1)
    base = pl.multiple_of(c * r, r)
    n_chunks = r // CHUNK
    z_rows = z_ref.shape[0]
    for j in range(n_seq):
        row0 = j * r

        vbn = _rms(vb_ref[row0:row0 + r, :].astype(F32), snw_ref[...]).astype(BF16)
        for g in range(B_GROUPS):
            cs = slice(g * HEAD_DIM, (g + 1) * HEAD_DIM)
            wide = jnp.concatenate([vbn[cc * CHUNK:(cc + 1) * CHUNK, cs] for cc in range(n_chunks)], axis=1)
            mixed = jnp.dot(ws_ref[g], wide, preferred_element_type=F32) + bs_ref[g]
            for cc in range(n_chunks):
                rs = slice(row0 + cc * CHUNK, row0 + (cc + 1) * CHUNK)
                mx = mixed[:, cc * HEAD_DIM:(cc + 1) * HEAD_DIM]
                bo_ref[rs, cs] = (u_ref[rs, cs].astype(F32) * mx * gb_ref[rs, cs].astype(F32)).astype(BF16)

        zb = zb_scr.at[j]
        z0 = base + row0
        zb[POOL_HALO:POOL_HALO + r, :] = z_ref[pl.ds(pl.multiple_of(z0, POOL_HALO), r), :]
        above = z_ref[pl.ds(pl.multiple_of(jnp.maximum(z0 - POOL_HALO, 0), POOL_HALO), POOL_HALO), :]
        below = z_ref[pl.ds(pl.multiple_of(jnp.minimum(z0 + r, z_rows - POOL_HALO), POOL_HALO), POOL_HALO), :]
        zeros = jnp.zeros((POOL_HALO, C_WIDTH), BF16)
        zb[0:POOL_HALO, :] = jnp.where(c > 0, above, zeros)
        zb[POOL_HALO + r:, :] = jnp.where(c < nc - 1, below, zeros)

        for g in range(len(POOL_WINDOWS)):
            cs = slice(g * HEAD_DIM, (g + 1) * HEAD_DIM)
            ds = []
            for blk in range(n_chunks):
                lo = blk * CHUNK
                wsum = jnp.dot(band_ref[g], zb[lo:lo + CHUNK + 2 * POOL_HALO, cs], preferred_element_type=F32)
                zc = zb[POOL_HALO + lo:POOL_HALO + lo + CHUNK, cs].astype(F32)
                ds.append((wsum * inv_ref[lo:lo + CHUNK, cs] - zc).astype(BF16))
            dm = jnp.dot(jnp.concatenate(ds, axis=0), wp_ref[g], preferred_element_type=F32)
            rows = slice(row0, row0 + r)
            co_ref[rows, cs] = (dm * ps_ref[:, cs] * gc_ref[rows, cs].astype(F32)).astype(BF16)


def _pool_band():
    t = np.arange(CHUNK)[:, None] + POOL_HALO
    j = np.arange(CHUNK + 2 * POOL_HALO)[None, :]
    return jnp.asarray(np.stack([(j >= t - w // 2) & (j < t + w - w // 2) for w in POOL_WINDOWS]), dtype=BF16)


def _pool_inv_count(seq):
    t = np.arange(seq)
    cols = [1.0 / (np.clip(t + w - w // 2, 0, seq) - np.clip(t - w // 2, 0, seq)) for w in POOL_WINDOWS]
    return jnp.asarray(np.repeat(np.stack(cols, axis=1), HEAD_DIM, axis=1), dtype=F32)


def _mix_specs(proj, snw, ws, bs, wp, ps, *, row0, n_batch, seq, n_seq):
    r = min(SEQ_TILE, seq)
    nc = seq // r
    rb, sb = r * n_seq, seq * n_seq
    r0, s0 = row0 // rb, row0 // sb
    tile = lambda col: pl.BlockSpec((rb, COL_TILE), lambda b, c: (r0 + b * nc + c, col))
    const3 = lambda shape: pl.BlockSpec(shape, lambda b, c: (0, 0, 0))
    in_specs = [
        tile(U_TILE), tile(VB_TILE), tile(GB_TILE),
        pl.BlockSpec((sb, COL_TILE), lambda b, c: (s0 + b, Z_TILE)),
        tile(GC_TILE),
        pl.BlockSpec((1, B_WIDTH), lambda b, c: (0, 0)),
        const3((B_GROUPS, CHUNK, CHUNK)),
        const3((B_GROUPS, CHUNK, 1)),
        const3((B_GROUPS, HEAD_DIM, HEAD_DIM)),
        pl.BlockSpec((1, C_WIDTH), lambda b, c: (0, 0)),
        const3((len(POOL_WINDOWS), CHUNK, CHUNK + 2 * POOL_HALO)),
        pl.BlockSpec((r, C_WIDTH), lambda b, c: (c, 0)),
    ]
    args = [proj, proj, proj, proj, proj, snw, ws, bs, wp, ps, _pool_band(), _pool_inv_count(seq)]
    out_shape = [jax.ShapeDtypeStruct((n_batch * seq, B_WIDTH), BF16),
                 jax.ShapeDtypeStruct((n_batch * seq, C_WIDTH), BF16)]
    out_spec = pl.BlockSpec((rb, COL_TILE), lambda b, c: (b * nc + c, 0))
    scratch = [pltpu.VMEM((n_seq, r + 2 * POOL_HALO, C_WIDTH), BF16)]
    return in_specs, args, out_shape, [out_spec, out_spec], scratch


def _seq_kernel(*refs, ctx, n_seq, n_attn_in, n_mix_in, n_attn_scr):
    attn_in, refs = refs[:n_attn_in], refs[n_attn_in:]
    mix_in, refs = refs[:n_mix_in], refs[n_mix_in:]
    o_ref, bo_ref, co_ref = refs[:3]
    attn_scr, (zb_scr,) = refs[3:3 + n_attn_scr], refs[3 + n_attn_scr:]
    _attn_kernel(*attn_in, o_ref, *attn_scr, ctx=ctx, n_seq=n_seq)
    _mix_kernel(*mix_in, bo_ref, co_ref, zb_scr, n_seq=n_seq)


def _seq_call(proj, cache, mix_w, *, row0, n_batch, seq, layer):
    whole = seq <= SEQ_TILE
    n_seq = max(SHORT_SEQ_ROWS // seq, 1) if whole else 1
    assert n_batch % n_seq == 0 and (whole or seq % SEQ_TILE == 0)
    a_in, a_args, a_shape, a_out, a_scr = _attn_specs(proj, cache, row0=row0, n_batch=n_batch, seq=seq,
                                                      layer=layer, n_seq=n_seq)
    m_in, m_args, m_shape, m_out, m_scr = _mix_specs(proj, *mix_w, row0=row0, n_batch=n_batch, seq=seq, n_seq=n_seq)
    ctx = cache is not None
    return pl.pallas_call(
        functools.partial(_seq_kernel, ctx=ctx, n_seq=n_seq, n_attn_in=len(a_in), n_mix_in=len(m_in),
                          n_attn_scr=len(a_scr)),
        out_shape=a_shape + m_shape,
        grid=(n_batch // n_seq, max(seq // SEQ_TILE, 1)),
        in_specs=a_in + m_in,
        out_specs=a_out + m_out,
        scratch_shapes=a_scr + m_scr,
        compiler_params=_params("arbitrary", "arbitrary"),
        name="attn_mix_lat" if ctx else "attn_mix_ctx",
    )(*a_args, *m_args)


def _outproj_kernel(*refs, first, final, ctx_tiles, layer):
    (ac_ref, al_ref, bc_ref, bl_ref, cc_ref, cl_ref, g0_ref, g1_ref, g2_ref) = refs[:9]
    refs = refs[9:]
    if first:
        xc_ref, xl_ref = refs[:2]
        refs = refs[2:]
    else:
        x_ref = refs[0]
        refs = refs[1:]
    mod_ref, wa_hbm, wb_hbm, wc_hbm, wo_hbm, nw_ref = refs[:6]
    refs = refs[6:]
    if final:
        yc_ref, yl_ref = refs[:2]
        refs = refs[2:]
    else:
        modn_ref, y_ref, hn_ref = refs[:3]
        refs = refs[3:]
    a_scr, b_scr, c_scr, m_scr, wa_ref, wb_ref, wc_ref, wo_ref, stage, sem = refs[:10]
    y_scr = refs[10] if final else y_ref

    @pl.when(pl.program_id(0) == 0)
    def _():
        for w_hbm, w_ref in ((wa_hbm, wa_ref), (wb_hbm, wb_ref), (wc_hbm, wc_ref), (wo_hbm, wo_ref)):
            _load_weight_bf16(w_hbm, layer, w_ref, stage, sem)

    is_ctx = pl.program_id(0) < ctx_tiles
    a_scr[...] = _pick(is_ctx, ac_ref, al_ref)
    b_scr[...] = _pick(is_ctx, bc_ref, bl_ref)
    c_scr[...] = _pick(is_ctx, cc_ref, cl_ref)
    for n in range(D_MODEL // COL_TILE):
        cs = slice(n * COL_TILE, (n + 1) * COL_TILE)
        a = jnp.dot(a_scr[...], wa_ref[:, cs], preferred_element_type=F32)
        b = jnp.dot(b_scr[...], wb_ref[:, cs], preferred_element_type=F32)
        c = jnp.dot(c_scr[...], wc_ref[:, cs], preferred_element_type=F32)
        m = (g0_ref[:, cs].astype(F32) * a + g1_ref[:, cs].astype(F32) * b
             + g2_ref[:, cs].astype(F32) * c)
        m_scr[:, cs] = m.astype(BF16)
    ssq = jnp.zeros((m_scr.shape[0], 1), F32)
    for n in range(D_MODEL // COL_TILE):
        cs = slice(n * COL_TILE, (n + 1) * COL_TILE)
        out = jnp.dot(m_scr[...], wo_ref[:, cs], preferred_element_type=F32)
        x = jnp.where(is_ctx, xc_ref[:, cs], xl_ref[:, cs]) if first else x_ref[:, cs]
        y = x + mod_ref[:, 2 * D_MODEL + n * COL_TILE:2 * D_MODEL + (n + 1) * COL_TILE] * out
        y_scr[:, cs] = y
        ssq = ssq + jnp.sum(y * y, axis=-1, keepdims=True)
    yn = y_scr[...] * lax.rsqrt(ssq * (1.0 / D_MODEL) + EPS) * nw_ref[...]
    if final:
        @pl.when(is_ctx)
        def _():
            yc_ref[...] = yn

        @pl.when(jnp.logical_not(is_ctx))
        def _():
            yl_ref[...] = yn
    else:
        hn_ref[...] = (yn * (1.0 + modn_ref[:, D_MODEL:2 * D_MODEL]) + modn_ref[:, 0:D_MODEL]).astype(BF16)


def _outproj_call(attn, bout, cout, gates, x, mod3, wa, wb, wc, wo, nw, modn3, *, st, layer):
    tm = st.tm
    first = isinstance(x, tuple)
    final = modn3 is None
    gate = lambda g: pl.BlockSpec((tm, D_MODEL), lambda i: (i, g))
    row = pl.BlockSpec((tm, D_MODEL), lambda i: (i, 0))
    in_specs = (st.two_source_specs(A_WIDTH) + st.two_source_specs(B_WIDTH) + st.two_source_specs(C_WIDTH)
                + [gate(0), gate(1), gate(2)]
                + (st.two_source_specs(D_MODEL) if first else [row])
                + [pl.BlockSpec((None, 1, GATE_COLS), st.mod_index),
                   HBM, HBM, HBM, HBM,
                   pl.BlockSpec((1, D_MODEL), lambda i: (0, 0))])
    args = [*attn, *bout, *cout, gates, gates, gates, *(x if first else (x,)), mod3, wa, wb, wc, wo, nw]
    scratch = [pltpu.VMEM((tm, A_WIDTH), BF16), pltpu.VMEM((tm, B_WIDTH), BF16),
               pltpu.VMEM((tm, C_WIDTH), BF16), pltpu.VMEM((tm, D_MODEL), BF16),
               pltpu.VMEM((A_WIDTH, D_MODEL), BF16), pltpu.VMEM((B_WIDTH, D_MODEL), BF16),
               pltpu.VMEM((C_WIDTH, D_MODEL), BF16), pltpu.VMEM((D_MODEL, D_MODEL), BF16),
               pltpu.VMEM((2, OUT_CAST_ROWS, D_MODEL), F32), pltpu.SemaphoreType.DMA((2,))]
    if final:
        out_shape = [jax.ShapeDtypeStruct((st.ctx_tiles * tm, D_MODEL), F32),
                     jax.ShapeDtypeStruct(((st.tiles - st.ctx_tiles) * tm, D_MODEL), F32)]
        out_specs = st.two_source_specs(D_MODEL)
        scratch.append(pltpu.VMEM((tm, D_MODEL), F32))
    else:
        in_specs.append(pl.BlockSpec((None, 1, GATE_COLS), st.mod_index))
        args.append(modn3)
        out_shape = [jax.ShapeDtypeStruct((st.tiles * tm, D_MODEL), F32),
                     jax.ShapeDtypeStruct((st.tiles * tm, D_MODEL), BF16)]
        out_specs = [row, row]
    return pl.pallas_call(
        functools.partial(_outproj_kernel, first=first, final=final, ctx_tiles=st.ctx_tiles, layer=layer),
        out_shape=out_shape,
        grid=(st.tiles,),
        in_specs=in_specs,
        out_specs=out_specs,
        scratch_shapes=scratch,
        compiler_params=_params("arbitrary"),
        name="out_proj_final" if final else "out_proj",
    )(*args)


def _rope_tables(n_identity, n_tokens):
    rows = n_tokens // GRID_W
    row = np.repeat(np.arange(rows), GRID_W).astype(np.float64)
    col = np.tile(np.arange(GRID_W), rows).astype(np.float64)
    n_freq = HEAD_DIM // 4
    inv = ROPE_THETA ** (-np.arange(n_freq, dtype=np.float64) / n_freq)
    ar = row[:, None] * inv[None, :]
    ac = col[:, None] * inv[None, :]
    zero = np.zeros_like(ar)
    cos = np.concatenate([np.cos(ar), np.cos(ar), np.cos(ac), np.cos(ac)], axis=-1)
    sa = np.concatenate([-np.sin(ar), zero, -np.sin(ac), zero], axis=-1)
    sb = np.concatenate([zero, np.sin(ar), zero, np.sin(ac)], axis=-1)
    ident = (np.ones((n_identity, HEAD_DIM)), np.zeros((n_identity, HEAD_DIM)), np.zeros((n_identity, HEAD_DIM)))
    return tuple(jnp.asarray(np.concatenate([i, t], axis=0), dtype=F32) for i, t in zip(ident, (cos, sa, sb)))


def kernel(x_prompt, x_sample, cache_k, cache_v, c, c_ctx, norm_w, w_ada, b_ada, w_in, q_norm_w,
           k_norm_w, sgu_norm_w, w_sgu, b_sgu, w_pool, pool_scale, w_br_a, w_br_b, w_br_c, w_merge,
           b_merge, w_out, final_norm_w):
    nb_p, seq_p, d = x_prompt.shape
    nb_s, seq_s, _ = x_sample.shape
    n_ctx, n_lat = nb_p * seq_p, nb_s * seq_s
    assert d == D_MODEL and nb_s + 1 <= MOD_ROWS

    cv = jnp.concatenate([c_ctx[None, :], c, jnp.zeros((MOD_ROWS - 1 - nb_s, d), F32)], axis=0)
    mod = _ada_call(cv, w_ada, b_ada.reshape(DEPTH, 1, GATE_COLS))
    mod3 = [mod[l].reshape(MOD_ROWS, 1, GATE_COLS) for l in range(DEPTH)]
    st_in = _Stream(n_ctx, n_lat, seq_s, IN_TOKEN_TILE)
    st_out = _Stream(n_ctx, n_lat, seq_s, OUT_TOKEN_TILE)
    rope_tabs = _rope_tables(IN_TOKEN_TILE, seq_s)
    cache = (cache_k, cache_v)

    x = (x_prompt.reshape(n_ctx, d), x_sample.reshape(n_lat, d))
    h = _normmod_call(*x, mod3[0], norm_w[0].reshape(1, d), seq=seq_s)
    states = []
    w_mg, w_i, wa, wb, wc, wo = w_merge, w_in, w_br_a, w_br_b, w_br_c, w_out
    for l in range(DEPTH):
        last = l == DEPTH - 1
        bm = b_merge[l].reshape(1, GATE_COLS)
        qnw = q_norm_w[l].reshape(1, HEAD_DIM)
        knw = k_norm_w[l].reshape(1, HEAD_DIM)
        mix_w = (sgu_norm_w[l].reshape(1, B_WIDTH), w_sgu[l].astype(BF16),
                 b_sgu[l].reshape(B_GROUPS, CHUNK, 1), w_pool[l].astype(BF16),
                 pool_scale[l].reshape(1, C_WIDTH))
        nw_next = (final_norm_w if last else norm_w[l + 1]).reshape(1, d)

        gates = _gates_call(h, w_mg, bm, layer=l)
        proj, ks, vs = _proj_call(h, w_i, qnw, knw, rope_tabs, st=st_in, layer=l)
        states.append((ks, vs))
        seq_c = _seq_call(proj, None, mix_w, row0=0, n_batch=nb_p, seq=seq_p, layer=l)
        seq_l = _seq_call(proj, cache, mix_w, row0=n_ctx, n_batch=nb_s, seq=seq_s, layer=l)
        attn, bout, cout = zip(seq_c, seq_l)
        res = _outproj_call(attn, bout, cout, gates, x, mod3[l],
                            wa, wb, wc, wo, nw_next, None if last else mod3[l + 1], st=st_out, layer=l)
        if last:
            y_ctx, y_lat = res
        else:
            x, h = res

    state_k, state_v = (
        jnp.stack([s[i][:n_ctx].reshape(nb_p, seq_p, A_KV_HEADS, HEAD_DIM) for s in states], axis=1)
        for i in range(2))
    return (y_ctx.reshape(nb_p, seq_p, d), y_lat.reshape(nb_s, seq_s, d), state_k, state_v)
```

```python
import functools

import jax
import jax.numpy as jnp
import numpy as np
from jax import lax
from jax.experimental import pallas as pl
from jax.experimental.pallas import tpu as pltpu

F32 = jnp.float32
BF16 = jnp.bfloat16

D_MODEL = 2048
DEPTH = 2
GRID_W = 64
EPS = 1e-6
HEAD_DIM = 128
A_HEADS = 8
A_KV_HEADS = 2
A_WIDTH = A_HEADS * HEAD_DIM
KV_WIDTH = A_KV_HEADS * HEAD_DIM
ROPE_THETA = 10000.0
ROPE_PARTNER = HEAD_DIM // 4
ATTN_SCALE = HEAD_DIM ** -0.5
LOG2_E = 1.4426950408889634
CHUNK = 128
B_GROUPS = 4
B_WIDTH = 512
C_WIDTH = 512
POOL_WINDOWS = (2, 4, 8, 16)
POOL_HALO = 64
N_BRANCH = 3
GATE_COLS = N_BRANCH * D_MODEL
IN_COLS = 2 * A_WIDTH + 2 * KV_WIDTH + 3 * B_WIDTH + 2 * C_WIDTH

COL_TILE = 512
Q_TILE0 = 0
KV_TILE = Q_TILE0 + A_WIDTH // COL_TILE
GA_TILE0 = KV_TILE + 1
U_TILE = GA_TILE0 + A_WIDTH // COL_TILE
VB_TILE = U_TILE + 1
GB_TILE = VB_TILE + 1
Z_TILE = GB_TILE + 1
GC_TILE = Z_TILE + 1
HEADS_PER_TILE = COL_TILE // HEAD_DIM
K_COL = KV_TILE * COL_TILE
V_COL = K_COL + KV_WIDTH

SUB_COLS = 256
MOD_ROWS = 8
ADA_TILE = 1024
IN_TOKEN_TILE = 512
NORM_TOKEN_TILE = 1024
NORM_ROWS = 16
OUT_TOKEN_TILE = 256
SEQ_TILE = 512
SHORT_SEQ_ROWS = 1024
CAST_ROWS = 128
OUT_CAST_ROWS = 256
VMEM_LIMIT = 56 * 1024 * 1024


def _params(*sem):
    return pltpu.CompilerParams(dimension_semantics=sem, vmem_limit_bytes=VMEM_LIMIT)


def _resident(shape):
    return pl.BlockSpec(shape, lambda *_: (0,) * len(shape), pipeline_mode=pl.Buffered(1))


def _silu(x):
    return x * jax.nn.sigmoid(x)


def _rms(x, w):
    ms = jnp.mean(x * x, axis=-1, keepdims=True)
    return x * lax.rsqrt(ms + EPS) * w


class _Stream:
    def __init__(self, n_ctx, n_lat, seq, tm):
        assert n_ctx % tm == 0 and seq % tm == 0
        self.tm = tm
        self.ctx_tiles = n_ctx // tm
        self.tiles = (n_ctx + n_lat) // tm
        self.tiles_per_seq = seq // tm

    def mod_index(self, i):
        lat = 1 + (i - self.ctx_tiles) // self.tiles_per_seq
        return (jnp.where(i < self.ctx_tiles, 0, lat), 0, 0)

    def two_source_specs(self, width):
        ctx = pl.BlockSpec((self.tm, width), lambda i: (jnp.minimum(i, self.ctx_tiles - 1), 0))
        lat = pl.BlockSpec((self.tm, width), lambda i: (jnp.maximum(i - self.ctx_tiles, 0), 0))
        return [ctx, lat]


def _pick(is_ctx, ctx_ref, lat_ref):
    return jnp.where(is_ctx, ctx_ref[...], lat_ref[...])


def _load_weight_bf16(w_hbm, layer, w_scr, stage, sem):
    chunk = stage.shape[1]
    n = w_scr.shape[0] // chunk

    def copy(c):
        return pltpu.make_async_copy(w_hbm.at[layer, pl.ds(c * chunk, chunk), :], stage.at[c % 2], sem.at[c % 2])

    copy(0).start()
    for c in range(n):
        if c + 1 < n:
            copy(c + 1).start()
        copy(c).wait()
        w_scr[c * chunk:(c + 1) * chunk, :] = stage[c % 2].astype(BF16)


def _weight_scratch(rows, cols, chunk):
    return [pltpu.VMEM((rows, cols), BF16), pltpu.VMEM((2, chunk, cols), F32), pltpu.SemaphoreType.DMA((2,))]


HBM = pl.BlockSpec(memory_space=pl.ANY)


def _ada_kernel(cv_ref, w_ref, b_ref, o_ref):
    a = _silu(cv_ref[...]).astype(BF16)
    o_ref[...] = jnp.dot(a, w_ref[...].astype(BF16), preferred_element_type=F32) + b_ref[...]


def _ada_call(cv, w_ada, b_ada):
    return pl.pallas_call(
        _ada_kernel,
        out_shape=jax.ShapeDtypeStruct((DEPTH, MOD_ROWS, GATE_COLS), F32),
        grid=(DEPTH, GATE_COLS // ADA_TILE),
        in_specs=[
            pl.BlockSpec((MOD_ROWS, D_MODEL), lambda l, j: (0, 0)),
            pl.BlockSpec((None, D_MODEL, ADA_TILE), lambda l, j: (l, 0, j)),
            pl.BlockSpec((None, 1, ADA_TILE), lambda l, j: (l, 0, j)),
        ],
        out_specs=pl.BlockSpec((None, MOD_ROWS, ADA_TILE), lambda l, j: (l, 0, j)),
        compiler_params=_params("arbitrary", "arbitrary"),
        name="ada_mod",
    )(cv, w_ada, b_ada)


def _normmod_kernel(xc_ref, xl_ref, mod_ref, nw_ref, h_ref, *, ctx_tiles):
    def emit(x_ref):
        def chunk(r, carry):
            rows = pl.ds(pl.multiple_of(r * NORM_ROWS, NORM_ROWS), NORM_ROWS)
            y = _rms(x_ref[rows, :], nw_ref[...])
            h_ref[rows, :] = (y * (1.0 + mod_ref[:, D_MODEL:2 * D_MODEL]) + mod_ref[:, 0:D_MODEL]).astype(BF16)
            return carry

        lax.fori_loop(0, h_ref.shape[0] // NORM_ROWS, chunk, 0, unroll=8)

    @pl.when(pl.program_id(0) < ctx_tiles)
    def _():
        emit(xc_ref)

    @pl.when(pl.program_id(0) >= ctx_tiles)
    def _():
        emit(xl_ref)


def _normmod_call(x_ctx, x_lat, mod3, norm_w, *, seq):
    st = _Stream(x_ctx.shape[0], x_lat.shape[0], seq, NORM_TOKEN_TILE)
    return pl.pallas_call(
        functools.partial(_normmod_kernel, ctx_tiles=st.ctx_tiles),
        out_shape=jax.ShapeDtypeStruct((st.tiles * st.tm, D_MODEL), BF16),
        grid=(st.tiles,),
        in_specs=st.two_source_specs(D_MODEL) + [
            pl.BlockSpec((None, 1, GATE_COLS), st.mod_index),
            pl.BlockSpec((1, D_MODEL), lambda i: (0, 0)),
        ],
        out_specs=pl.BlockSpec((st.tm, D_MODEL), lambda i: (i, 0)),
        compiler_params=_params("arbitrary"),
        name="norm_mod",
    )(x_ctx, x_lat, mod3, norm_w)


def _gates_kernel(h_ref, w_hbm, b_ref, o_ref, w_ref, stage, sem, *, layer):
    @pl.when(pl.program_id(0) == 0)
    def _():
        _load_weight_bf16(w_hbm, layer, w_ref, stage, sem)

    for s in range(GATE_COLS // SUB_COLS):
        cs = slice(s * SUB_COLS, (s + 1) * SUB_COLS)
        acc = jnp.dot(h_ref[...], w_ref[:, cs], preferred_element_type=F32)
        o_ref[:, cs] = jax.nn.sigmoid(acc + b_ref[:, cs]).astype(BF16)


def _gates_call(h, w_merge, b_merge, *, layer):
    t = h.shape[0]
    tm = IN_TOKEN_TILE
    return pl.pallas_call(
        functools.partial(_gates_kernel, layer=layer),
        out_shape=jax.ShapeDtypeStruct((t, GATE_COLS), BF16),
        grid=(t // tm,),
        in_specs=[
            pl.BlockSpec((tm, D_MODEL), lambda i: (i, 0)),
            HBM,
            _resident((1, GATE_COLS)),
        ],
        out_specs=pl.BlockSpec((tm, GATE_COLS), lambda i: (i, 0)),
        scratch_shapes=_weight_scratch(D_MODEL, GATE_COLS, CAST_ROWS),
        compiler_params=_params("arbitrary"),
        name="gates",
    )(h, w_merge, b_merge)


def _rope(y, cos, sa, sb):
    return (y * cos + pltpu.roll(y, HEAD_DIM - ROPE_PARTNER, 1) * sa + pltpu.roll(y, ROPE_PARTNER, 1) * sb)


def _proj_kernel(h_ref, w_hbm, qnw_ref, knw_ref, cos_ref, sa_ref, sb_ref, proj_ref, ks_ref, vs_ref,
                 w_ref, stage, sem, *, layer):
    @pl.when(pl.program_id(0) == 0)
    def _():
        _load_weight_bf16(w_hbm, layer, w_ref, stage, sem)

    def head(xh, w):
        return _rope(_rms(xh, w), cos_ref[...], sa_ref[...], sb_ref[...])

    qw = qnw_ref[...] * (ATTN_SCALE * LOG2_E)
    silu_cols = ((GA_TILE0 * COL_TILE, U_TILE * COL_TILE), (GB_TILE * COL_TILE, Z_TILE * COL_TILE),
                 (GC_TILE * COL_TILE, IN_COLS))
    for s in range(IN_COLS // SUB_COLS):
        acc = jnp.dot(h_ref[...], w_ref[:, s * SUB_COLS:(s + 1) * SUB_COLS], preferred_element_type=F32)
        for hb in range(SUB_COLS // HEAD_DIM):
            c0 = s * SUB_COLS + hb * HEAD_DIM
            cs = slice(c0, c0 + HEAD_DIM)
            a = acc[:, hb * HEAD_DIM:(hb + 1) * HEAD_DIM]
            if c0 < K_COL:
                proj_ref[:, cs] = head(a, qw).astype(BF16)
            elif c0 < V_COL:
                ks_ref[:, c0 - K_COL:c0 - K_COL + HEAD_DIM] = _rms(a, knw_ref[...])
                proj_ref[:, cs] = head(a, knw_ref[...]).astype(BF16)
            elif c0 < V_COL + KV_WIDTH:
                vs_ref[:, c0 - V_COL:c0 - V_COL + HEAD_DIM] = a
                proj_ref[:, cs] = a.astype(BF16)
            elif any(lo <= c0 < hi for lo, hi in silu_cols):
                proj_ref[:, cs] = _silu(a).astype(BF16)
            else:
                proj_ref[:, cs] = a.astype(BF16)


def _proj_call(h, w_in, qnw, knw, rope_tabs, *, st, layer):
    t = h.shape[0]
    tm = st.tm

    def tab_index(i):
        return (jnp.where(i < st.ctx_tiles, 0, 1 + (i - st.ctx_tiles) % st.tiles_per_seq), 0)

    in_specs = [
        pl.BlockSpec((tm, D_MODEL), lambda i: (i, 0)),
        HBM,
        pl.BlockSpec((1, HEAD_DIM), lambda i: (0, 0)),
        pl.BlockSpec((1, HEAD_DIM), lambda i: (0, 0)),
    ] + [pl.BlockSpec((tm, HEAD_DIM), tab_index) for _ in rope_tabs]
    out_shape = [jax.ShapeDtypeStruct((t, IN_COLS), BF16),
                 jax.ShapeDtypeStruct((t, KV_WIDTH), F32), jax.ShapeDtypeStruct((t, KV_WIDTH), F32)]
    out_specs = [pl.BlockSpec((tm, IN_COLS), lambda i: (i, 0)),
                 pl.BlockSpec((tm, KV_WIDTH), lambda i: (i, 0)), pl.BlockSpec((tm, KV_WIDTH), lambda i: (i, 0))]
    return pl.pallas_call(
        functools.partial(_proj_kernel, layer=layer),
        out_shape=out_shape,
        grid=(t // tm,),
        in_specs=in_specs,
        out_specs=out_specs,
        scratch_shapes=_weight_scratch(D_MODEL, IN_COLS, CAST_ROWS),
        compiler_params=_params("arbitrary"),
        name="proj",
    )(h, w_in, qnw, knw, *rope_tabs)


def _attn_kernel(*refs, ctx, n_seq):
    q_ref, k_ref, v_ref, ga0_ref, ga1_ref = refs[:5]
    ga_refs = (ga0_ref, ga1_ref)
    refs = refs[5:]
    if ctx:
        ck_ref, cv_ref = refs[:2]
        refs = refs[2:]
    o_ref, vx_scr = refs[:2]
    if ctx:
        ckx_scr, cvx_scr = refs[2:]
    seq = k_ref.shape[0] // n_seq
    tq = q_ref.shape[0] // n_seq

    @pl.when(pl.program_id(1) == 0)
    def _():
        for j in range(n_seq):
            for kh in range(A_KV_HEADS):
                hs = slice(kh * HEAD_DIM, (kh + 1) * HEAD_DIM)
                vx_scr[j, kh, :, :HEAD_DIM] = v_ref[j * seq:(j + 1) * seq, hs]
                vx_scr[j, kh, :, HEAD_DIM:] = jnp.ones((seq, HEAD_DIM), BF16)
        if ctx:
            for kh in range(A_KV_HEADS):
                ckx_scr[kh] = ck_ref[:, kh, :].astype(BF16)
                cvx_scr[kh, :, :HEAD_DIM] = cv_ref[:, kh, :].astype(BF16)
                cvx_scr[kh, :, HEAD_DIM:] = jnp.ones((cvx_scr.shape[1], HEAD_DIM), BF16)

    nt = (((1,), (1,)), ((), ()))
    for j in range(n_seq):
        qrows = slice(j * tq, (j + 1) * tq)
        for head in range(A_HEADS):
            kh, hh = divmod(head, HEADS_PER_TILE)
            cs = slice(head * HEAD_DIM, (head + 1) * HEAD_DIM)
            q = q_ref[qrows, cs]
            k = k_ref[j * seq:(j + 1) * seq, kh * HEAD_DIM:(kh + 1) * HEAD_DIM]
            s1 = lax.dot_general(q, k, nt, preferred_element_type=F32)
            m = jnp.max(s1, axis=-1, keepdims=True)
            if ctx:
                s2 = lax.dot_general(q, ckx_scr[kh], nt, preferred_element_type=F32)
                m = jnp.maximum(m, jnp.max(s2, axis=-1, keepdims=True))
            ox = jnp.dot(jnp.exp2(s1 - m).astype(BF16), vx_scr[j, kh], preferred_element_type=F32)
            if ctx:
                ox = ox + jnp.dot(jnp.exp2(s2 - m).astype(BF16), cvx_scr[kh], preferred_element_type=F32)
            o = ox[:, :HEAD_DIM] / ox[:, HEAD_DIM:]
            ga = ga_refs[kh][qrows, hh * HEAD_DIM:(hh + 1) * HEAD_DIM]
            o_ref[qrows, cs] = (o * ga.astype(F32)).astype(BF16)


def _attn_specs(proj, cache, *, row0, n_batch, seq, layer, n_seq):
    tq = min(SEQ_TILE, seq)
    nq = seq // tq
    qb, sb = tq * n_seq, seq * n_seq
    q0, s0 = row0 // qb, row0 // sb
    in_specs = [
        pl.BlockSpec((qb, A_WIDTH), lambda b, qi: (q0 + b * nq + qi, Q_TILE0 * COL_TILE // A_WIDTH)),
        pl.BlockSpec((sb, KV_WIDTH), lambda b, qi: (s0 + b, K_COL // KV_WIDTH)),
        pl.BlockSpec((sb, KV_WIDTH), lambda b, qi: (s0 + b, V_COL // KV_WIDTH)),
        pl.BlockSpec((qb, COL_TILE), lambda b, qi: (q0 + b * nq + qi, GA_TILE0)),
        pl.BlockSpec((qb, COL_TILE), lambda b, qi: (q0 + b * nq + qi, GA_TILE0 + 1)),
    ]
    args = [proj, proj, proj, proj, proj]
    scratch = [pltpu.VMEM((n_seq, A_KV_HEADS, seq, 2 * HEAD_DIM), BF16)]
    if cache is not None:
        past = cache[0].shape[2]
        for c in cache:
            in_specs.append(pl.BlockSpec((None, None, past, A_KV_HEADS, HEAD_DIM),
                                         lambda b, qi: (b, layer, 0, 0, 0)))
            args.append(c)
        scratch += [pltpu.VMEM((A_KV_HEADS, past, HEAD_DIM), BF16),
                    pltpu.VMEM((A_KV_HEADS, past, 2 * HEAD_DIM), BF16)]
    out_shape = [jax.ShapeDtypeStruct((n_batch * seq, A_WIDTH), BF16)]
    out_specs = [pl.BlockSpec((qb, A_WIDTH), lambda b, qi: (b * nq + qi, 0))]
    return in_specs, args, out_shape, out_specs, scratch


def _mix_kernel(u_ref, vb_ref, gb_ref, z_ref, gc_ref, snw_ref, ws_ref, bs_ref, wp_ref, ps_ref, band_ref, inv_ref,
                bo_ref, co_ref, zb_scr, *, n_seq):
    r = u_ref.shape[0] // n_seq
    c = pl.program_id(1)
    nc = pl.num_programs(1)
    base = pl.multiple_of(c * r, r)
    n_chunks = r // CHUNK
    z_rows = z_ref.shape[0]
    for j in range(n_seq):
        row0 = j * r

        vbn = _rms(vb_ref[row0:row0 + r, :].astype(F32), snw_ref[...]).astype(BF16)
        for g in range(B_GROUPS):
            cs = slice(g * HEAD_DIM, (g + 1) * HEAD_DIM)
            wide = jnp.concatenate([vbn[cc * CHUNK:(cc + 1) * CHUNK, cs] for cc in range(n_chunks)], axis=1)
            mixed = jnp.dot(ws_ref[g], wide, preferred_element_type=F32) + bs_ref[g]
            for cc in range(n_chunks):
                rs = slice(row0 + cc * CHUNK, row0 + (cc + 1) * CHUNK)
                mx = mixed[:, cc * HEAD_DIM:(cc + 1) * HEAD_DIM]
                bo_ref[rs, cs] = (u_ref[rs, cs].astype(F32) * mx * gb_ref[rs, cs].astype(F32)).astype(BF16)

        zb = zb_scr.at[j]
        z0 = base + row0
        zb[POOL_HALO:POOL_HALO + r, :] = z_ref[pl.ds(pl.multiple_of(z0, POOL_HALO), r), :]
        above = z_ref[pl.ds(pl.multiple_of(jnp.maximum(z0 - POOL_HALO, 0), POOL_HALO), POOL_HALO), :]
        below = z_ref[pl.ds(pl.multiple_of(jnp.minimum(z0 + r, z_rows - POOL_HALO), POOL_HALO), POOL_HALO), :]
        zeros = jnp.zeros((POOL_HALO, C_WIDTH), BF16)
        zb[0:POOL_HALO, :] = jnp.where(c > 0, above, zeros)
        zb[POOL_HALO + r:, :] = jnp.where(c < nc - 1, below, zeros)

        for g in range(len(POOL_WINDOWS)):
            cs = slice(g * HEAD_DIM, (g + 1) * HEAD_DIM)
            ds = []
            for blk in range(n_chunks):
                lo = blk * CHUNK
                wsum = jnp.dot(band_ref[g], zb[lo:lo + CHUNK + 2 * POOL_HALO, cs], preferred_element_type=F32)
                zc = zb[POOL_HALO + lo:POOL_HALO + lo + CHUNK, cs].astype(F32)
                ds.append((wsum * inv_ref[lo:lo + CHUNK, cs] - zc).astype(BF16))
            dm = jnp.dot(jnp.concatenate(ds, axis=0), wp_ref[g], preferred_element_type=F32)
            rows = slice(row0, row0 + r)
            co_ref[rows, cs] = (dm * ps_ref[:, cs] * gc_ref[rows, cs].astype(F32)).astype(BF16)


def _pool_band():
    t = np.arange(CHUNK)[:, None] + POOL_HALO
    j = np.arange(CHUNK + 2 * POOL_HALO)[None, :]
    return jnp.asarray(np.stack([(j >= t - w // 2) & (j < t + w - w // 2) for w in POOL_WINDOWS]), dtype=BF16)


def _pool_inv_count(seq):
    t = np.arange(seq)
    cols = [1.0 / (np.clip(t + w - w // 2, 0, seq) - np.clip(t - w // 2, 0, seq)) for w in POOL_WINDOWS]
    return jnp.asarray(np.repeat(np.stack(cols, axis=1), HEAD_DIM, axis=1), dtype=F32)


def _mix_specs(proj, snw, ws, bs, wp, ps, *, row0, n_batch, seq, n_seq):
    r = min(SEQ_TILE, seq)
    nc = seq // r
    rb, sb = r * n_seq, seq * n_seq
    r0, s0 = row0 // rb, row0 // sb
    tile = lambda col: pl.BlockSpec((rb, COL_TILE), lambda b, c: (r0 + b * nc + c, col))
    const3 = lambda shape: pl.BlockSpec(shape, lambda b, c: (0, 0, 0))
    in_specs = [
        tile(U_TILE), tile(VB_TILE), tile(GB_TILE),
        pl.BlockSpec((sb, COL_TILE), lambda b, c: (s0 + b, Z_TILE)),
        tile(GC_TILE),
        pl.BlockSpec((1, B_WIDTH), lambda b, c: (0, 0)),
        const3((B_GROUPS, CHUNK, CHUNK)),
        const3((B_GROUPS, CHUNK, 1)),
        const3((B_GROUPS, HEAD_DIM, HEAD_DIM)),
        pl.BlockSpec((1, C_WIDTH), lambda b, c: (0, 0)),
        const3((len(POOL_WINDOWS), CHUNK, CHUNK + 2 * POOL_HALO)),
        pl.BlockSpec((r, C_WIDTH), lambda b, c: (c, 0)),
    ]
    args = [proj, proj, proj, proj, proj, snw, ws, bs, wp, ps, _pool_band(), _pool_inv_count(seq)]
    out_shape = [jax.ShapeDtypeStruct((n_batch * seq, B_WIDTH), BF16),
                 jax.ShapeDtypeStruct((n_batch * seq, C_WIDTH), BF16)]
    out_spec = pl.BlockSpec((rb, COL_TILE), lambda b, c: (b * nc + c, 0))
    scratch = [pltpu.VMEM((n_seq, r + 2 * POOL_HALO, C_WIDTH), BF16)]
    return in_specs, args, out_shape, [out_spec, out_spec], scratch


def _seq_kernel(*refs, ctx, n_seq, n_attn_in, n_mix_in, n_attn_scr, state):
    attn_in, refs = refs[:n_attn_in], refs[n_attn_in:]
    mix_in, refs = refs[:n_mix_in], refs[n_mix_in:]
    if state:
        (ks_ref, vs_ref), refs = refs[:2], refs[2:]
    o_ref, bo_ref, co_ref = refs[:3]
    refs = refs[3:]
    if state:
        (sk_ref, sv_ref), refs = refs[:2], refs[2:]
    attn_scr, (zb_scr,) = refs[:n_attn_scr], refs[n_attn_scr:]
    _attn_kernel(*attn_in, o_ref, *attn_scr, ctx=ctx, n_seq=n_seq)
    _mix_kernel(*mix_in, bo_ref, co_ref, zb_scr, n_seq=n_seq)
    if state:
        seq = sk_ref.shape[1]
        for src, dst in ((ks_ref, sk_ref), (vs_ref, sv_ref)):
            for j in range(n_seq):
                for kh in range(A_KV_HEADS):
                    dst[j, :, kh, :] = src[j * seq:(j + 1) * seq, kh * HEAD_DIM:(kh + 1) * HEAD_DIM]


def _seq_call(proj, cache, mix_w, kv_rows, *, row0, n_batch, seq, layer):
    whole = seq <= SEQ_TILE
    n_seq = max(SHORT_SEQ_ROWS // seq, 1) if whole else 1
    assert n_batch % n_seq == 0 and (whole or seq % SEQ_TILE == 0)
    a_in, a_args, a_shape, a_out, a_scr = _attn_specs(proj, cache, row0=row0, n_batch=n_batch, seq=seq,
                                                      layer=layer, n_seq=n_seq)
    m_in, m_args, m_shape, m_out, m_scr = _mix_specs(proj, *mix_w, row0=row0, n_batch=n_batch, seq=seq, n_seq=n_seq)
    ctx = cache is not None
    state = kv_rows is not None
    s_in, s_args, s_shape, s_out = [], [], [], []
    if state:
        assert whole
        sb = seq * n_seq
        s_in = [pl.BlockSpec((sb, KV_WIDTH), lambda b, c: (row0 // sb + b, 0))] * 2
        s_args = list(kv_rows)
        s_shape = [jax.ShapeDtypeStruct((n_batch, seq, A_KV_HEADS, HEAD_DIM), F32)] * 2
        s_out = [pl.BlockSpec((n_seq, seq, A_KV_HEADS, HEAD_DIM), lambda b, c: (b, 0, 0, 0))] * 2
    return pl.pallas_call(
        functools.partial(_seq_kernel, ctx=ctx, n_seq=n_seq, n_attn_in=len(a_in), n_mix_in=len(m_in),
                          n_attn_scr=len(a_scr), state=state),
        out_shape=a_shape + m_shape + s_shape,
        grid=(n_batch // n_seq, max(seq // SEQ_TILE, 1)),
        in_specs=a_in + m_in + s_in,
        out_specs=a_out + m_out + s_out,
        scratch_shapes=a_scr + m_scr,
        compiler_params=_params("arbitrary", "arbitrary"),
        name="attn_mix_lat" if ctx else "attn_mix_ctx",
    )(*a_args, *m_args, *s_args)


def _outproj_kernel(*refs, first, final, ctx_tiles, layer):
    (ac_ref, al_ref, bc_ref, bl_ref, cc_ref, cl_ref, g0_ref, g1_ref, g2_ref) = refs[:9]
    refs = refs[9:]
    if first:
        xc_ref, xl_ref = refs[:2]
        refs = refs[2:]
    else:
        x_ref = refs[0]
        refs = refs[1:]
    mod_ref, wa_hbm, wb_hbm, wc_hbm, wo_hbm, nw_ref = refs[:6]
    refs = refs[6:]
    if final:
        yc_ref, yl_ref = refs[:2]
        refs = refs[2:]
    else:
        modn_ref, y_ref, hn_ref = refs[:3]
        refs = refs[3:]
    a_scr, b_scr, c_scr, m_scr, wa_ref, wb_ref, wc_ref, wo_ref, stage, sem = refs[:10]
    y_scr = refs[10] if final else y_ref

    @pl.when(pl.program_id(0) == 0)
    def _():
        for w_hbm, w_ref in ((wa_hbm, wa_ref), (wb_hbm, wb_ref), (wc_hbm, wc_ref), (wo_hbm, wo_ref)):
            _load_weight_bf16(w_hbm, layer, w_ref, stage, sem)

    is_ctx = pl.program_id(0) < ctx_tiles
    a_scr[...] = _pick(is_ctx, ac_ref, al_ref)
    b_scr[...] = _pick(is_ctx, bc_ref, bl_ref)
    c_scr[...] = _pick(is_ctx, cc_ref, cl_ref)
    for n in range(D_MODEL // COL_TILE):
        cs = slice(n * COL_TILE, (n + 1) * COL_TILE)
        a = jnp.dot(a_scr[...], wa_ref[:, cs], preferred_element_type=F32)
        b = jnp.dot(b_scr[...], wb_ref[:, cs], preferred_element_type=F32)
        c = jnp.dot(c_scr[...], wc_ref[:, cs], preferred_element_type=F32)
        m = (g0_ref[:, cs].astype(F32) * a + g1_ref[:, cs].astype(F32) * b
             + g2_ref[:, cs].astype(F32) * c)
        m_scr[:, cs] = m.astype(BF16)
    ssq = jnp.zeros((m_scr.shape[0], 1), F32)
    for n in range(D_MODEL // COL_TILE):
        cs = slice(n * COL_TILE, (n + 1) * COL_TILE)
        out = jnp.dot(m_scr[...], wo_ref[:, cs], preferred_element_type=F32)
        x = jnp.where(is_ctx, xc_ref[:, cs], xl_ref[:, cs]) if first else x_ref[:, cs]
        y = x + mod_ref[:, 2 * D_MODEL + n * COL_TILE:2 * D_MODEL + (n + 1) * COL_TILE] * out
        y_scr[:, cs] = y
        ssq = ssq + jnp.sum(y * y, axis=-1, keepdims=True)
    yn = y_scr[...] * lax.rsqrt(ssq * (1.0 / D_MODEL) + EPS) * nw_ref[...]
    if final:
        @pl.when(is_ctx)
        def _():
            yc_ref[...] = yn

        @pl.when(jnp.logical_not(is_ctx))
        def _():
            yl_ref[...] = yn
    else:
        hn_ref[...] = (yn * (1.0 + modn_ref[:, D_MODEL:2 * D_MODEL]) + modn_ref[:, 0:D_MODEL]).astype(BF16)


def _outproj_call(attn, bout, cout, gates, x, mod3, wa, wb, wc, wo, nw, modn3, *, st, layer):
    tm = st.tm
    first = isinstance(x, tuple)
    final = modn3 is None
    gate = lambda g: pl.BlockSpec((tm, D_MODEL), lambda i: (i, g))
    row = pl.BlockSpec((tm, D_MODEL), lambda i: (i, 0))
    in_specs = (st.two_source_specs(A_WIDTH) + st.two_source_specs(B_WIDTH) + st.two_source_specs(C_WIDTH)
                + [gate(0), gate(1), gate(2)]
                + (st.two_source_specs(D_MODEL) if first else [row])
                + [pl.BlockSpec((None, 1, GATE_COLS), st.mod_index),
                   HBM, HBM, HBM, HBM,
                   pl.BlockSpec((1, D_MODEL), lambda i: (0, 0))])
    args = [*attn, *bout, *cout, gates, gates, gates, *(x if first else (x,)), mod3, wa, wb, wc, wo, nw]
    scratch = [pltpu.VMEM((tm, A_WIDTH), BF16), pltpu.VMEM((tm, B_WIDTH), BF16),
               pltpu.VMEM((tm, C_WIDTH), BF16), pltpu.VMEM((tm, D_MODEL), BF16),
               pltpu.VMEM((A_WIDTH, D_MODEL), BF16), pltpu.VMEM((B_WIDTH, D_MODEL), BF16),
               pltpu.VMEM((C_WIDTH, D_MODEL), BF16), pltpu.VMEM((D_MODEL, D_MODEL), BF16),
               pltpu.VMEM((2, OUT_CAST_ROWS, D_MODEL), F32), pltpu.SemaphoreType.DMA((2,))]
    if final:
        out_shape = [jax.ShapeDtypeStruct((st.ctx_tiles * tm, D_MODEL), F32),
                     jax.ShapeDtypeStruct(((st.tiles - st.ctx_tiles) * tm, D_MODEL), F32)]
        out_specs = st.two_source_specs(D_MODEL)
        scratch.append(pltpu.VMEM((tm, D_MODEL), F32))
    else:
        in_specs.append(pl.BlockSpec((None, 1, GATE_COLS), st.mod_index))
        args.append(modn3)
        out_shape = [jax.ShapeDtypeStruct((st.tiles * tm, D_MODEL), F32),
                     jax.ShapeDtypeStruct((st.tiles * tm, D_MODEL), BF16)]
        out_specs = [row, row]
    return pl.pallas_call(
        functools.partial(_outproj_kernel, first=first, final=final, ctx_tiles=st.ctx_tiles, layer=layer),
        out_shape=out_shape,
        grid=(st.tiles,),
        in_specs=in_specs,
        out_specs=out_specs,
        scratch_shapes=scratch,
        compiler_params=_params("arbitrary"),
        name="out_proj_final" if final else "out_proj",
    )(*args)


def _rope_tables(n_identity, n_tokens):
    rows = n_tokens // GRID_W
    row = np.repeat(np.arange(rows), GRID_W).astype(np.float64)
    col = np.tile(np.arange(GRID_W), rows).astype(np.float64)
    n_freq = HEAD_DIM // 4
    inv = ROPE_THETA ** (-np.arange(n_freq, dtype=np.float64) / n_freq)
    ar = row[:, None] * inv[None, :]
    ac = col[:, None] * inv[None, :]
    zero = np.zeros_like(ar)
    cos = np.concatenate([np.cos(ar), np.cos(ar), np.cos(ac), np.cos(ac)], axis=-1)
    sa = np.concatenate([-np.sin(ar), zero, -np.sin(ac), zero], axis=-1)
    sb = np.concatenate([zero, np.sin(ar), zero, np.sin(ac)], axis=-1)
    ident = (np.ones((n_identity, HEAD_DIM)), np.zeros((n_identity, HEAD_DIM)), np.zeros((n_identity, HEAD_DIM)))
    return tuple(jnp.asarray(np.concatenate([i, t], axis=0), dtype=F32) for i, t in zip(ident, (cos, sa, sb)))


def kernel(x_prompt, x_sample, cache_k, cache_v, c, c_ctx, norm_w, w_ada, b_ada, w_in, q_norm_w,
           k_norm_w, sgu_norm_w, w_sgu, b_sgu, w_pool, pool_scale, w_br_a, w_br_b, w_br_c, w_merge,
           b_merge, w_out, final_norm_w):
    nb_p, seq_p, d = x_prompt.shape
    nb_s, seq_s, _ = x_sample.shape
    n_ctx, n_lat = nb_p * seq_p, nb_s * seq_s
    assert d == D_MODEL and nb_s + 1 <= MOD_ROWS

    cv = jnp.concatenate([c_ctx[None, :], c, jnp.zeros((MOD_ROWS - 1 - nb_s, d), F32)], axis=0)
    mod = _ada_call(cv, w_ada, b_ada.reshape(DEPTH, 1, GATE_COLS))
    mod3 = [mod[l].reshape(MOD_ROWS, 1, GATE_COLS) for l in range(DEPTH)]
    st_in = _Stream(n_ctx, n_lat, seq_s, IN_TOKEN_TILE)
    st_out = _Stream(n_ctx, n_lat, seq_s, OUT_TOKEN_TILE)
    rope_tabs = _rope_tables(IN_TOKEN_TILE, seq_s)
    cache = (cache_k, cache_v)

    x = (x_prompt.reshape(n_ctx, d), x_sample.reshape(n_lat, d))
    h = _normmod_call(*x, mod3[0], norm_w[0].reshape(1, d), seq=seq_s)
    states = []
    w_mg, w_i, wa, wb, wc, wo = w_merge, w_in, w_br_a, w_br_b, w_br_c, w_out
    for l in range(DEPTH):
        last = l == DEPTH - 1
        bm = b_merge[l].reshape(1, GATE_COLS)
        qnw = q_norm_w[l].reshape(1, HEAD_DIM)
        knw = k_norm_w[l].reshape(1, HEAD_DIM)
        mix_w = (sgu_norm_w[l].reshape(1, B_WIDTH), w_sgu[l].astype(BF16),
                 b_sgu[l].reshape(B_GROUPS, CHUNK, 1), w_pool[l].astype(BF16),
                 pool_scale[l].reshape(1, C_WIDTH))
        nw_next = (final_norm_w if last else norm_w[l + 1]).reshape(1, d)

        gates = _gates_call(h, w_mg, bm, layer=l)
        proj, ks, vs = _proj_call(h, w_i, qnw, knw, rope_tabs, st=st_in, layer=l)
        *seq_c, sk, sv = _seq_call(proj, None, mix_w, (ks, vs), row0=0, n_batch=nb_p, seq=seq_p, layer=l)
        seq_l = _seq_call(proj, cache, mix_w, None, row0=n_ctx, n_batch=nb_s, seq=seq_s, layer=l)
        states.append((sk, sv))
        attn, bout, cout = zip(seq_c, seq_l)
        res = _outproj_call(attn, bout, cout, gates, x, mod3[l],
                            wa, wb, wc, wo, nw_next, None if last else mod3[l + 1], st=st_out, layer=l)
        if last:
            y_ctx, y_lat = res
        else:
            x, h = res

    state_k, state_v = (jnp.stack([s[i] for s in states], axis=1) for i in range(2))
    return (y_ctx.reshape(nb_p, seq_p, d), y_lat.reshape(nb_s, seq_s, d), state_k, state_v)
```

```python
import functools

import jax
import jax.numpy as jnp
import numpy as np
from jax import lax
from jax.experimental import pallas as pl
from jax.experimental.pallas import tpu as pltpu

F32 = jnp.float32
BF16 = jnp.bfloat16

D_MODEL = 2048
DEPTH = 2
GRID_W = 64
EPS = 1e-6
HEAD_DIM = 128
A_HEADS = 8
A_KV_HEADS = 2
A_WIDTH = A_HEADS * HEAD_DIM
KV_WIDTH = A_KV_HEADS * HEAD_DIM
ROPE_THETA = 10000.0
ROPE_PARTNER = HEAD_DIM // 4
ATTN_SCALE = HEAD_DIM ** -0.5
LOG2_E = 1.4426950408889634
CHUNK = 128
B_GROUPS = 4
B_WIDTH = 512
C_WIDTH = 512
POOL_WINDOWS = (2, 4, 8, 16)
POOL_HALO = 64
N_BRANCH = 3
GATE_COLS = N_BRANCH * D_MODEL
IN_COLS = 2 * A_WIDTH + 2 * KV_WIDTH + 3 * B_WIDTH + 2 * C_WIDTH

COL_TILE = 512
Q_TILE0 = 0
KV_TILE = Q_TILE0 + A_WIDTH // COL_TILE
GA_TILE0 = KV_TILE + 1
U_TILE = GA_TILE0 + A_WIDTH // COL_TILE
VB_TILE = U_TILE + 1
GB_TILE = VB_TILE + 1
Z_TILE = GB_TILE + 1
GC_TILE = Z_TILE + 1
HEADS_PER_TILE = COL_TILE // HEAD_DIM
K_COL = KV_TILE * COL_TILE
V_COL = K_COL + KV_WIDTH

SUB_COLS = 256
MOD_ROWS = 8
ADA_TILE = 1024
ADA_STEP_COLS = 512
IN_TOKEN_TILE = 512
NORM_TOKEN_TILE = 1024
NORM_ROWS = 16
OUT_TOKEN_TILE = 256
SEQ_TILE = 512
SHORT_SEQ_ROWS = 1024
CAST_ROWS = 128
OUT_CAST_ROWS = 256
VMEM_LIMIT = 56 * 1024 * 1024


def _params(*sem):
    return pltpu.CompilerParams(dimension_semantics=sem, vmem_limit_bytes=VMEM_LIMIT)


def _resident(shape):
    return pl.BlockSpec(shape, lambda *_: (0,) * len(shape), pipeline_mode=pl.Buffered(1))


def _silu(x):
    return x * jax.nn.sigmoid(x)


def _rms(x, w):
    ms = jnp.mean(x * x, axis=-1, keepdims=True)
    return x * lax.rsqrt(ms + EPS) * w


class _Stream:
    def __init__(self, n_ctx, n_lat, seq, tm):
        assert n_ctx % tm == 0 and seq % tm == 0
        self.tm = tm
        self.ctx_tiles = n_ctx // tm
        self.tiles = (n_ctx + n_lat) // tm
        self.tiles_per_seq = seq // tm

    def mod_index(self, i):
        lat = 1 + (i - self.ctx_tiles) // self.tiles_per_seq
        return (jnp.where(i < self.ctx_tiles, 0, lat), 0, 0)

    def two_source_specs(self, width):
        ctx = pl.BlockSpec((self.tm, width), lambda i: (jnp.minimum(i, self.ctx_tiles - 1), 0))
        lat = pl.BlockSpec((self.tm, width), lambda i: (jnp.maximum(i - self.ctx_tiles, 0), 0))
        return [ctx, lat]


def _pick(is_ctx, ctx_ref, lat_ref):
    return jnp.where(is_ctx, ctx_ref[...], lat_ref[...])


def _load_weight_bf16(w_hbm, layer, w_scr, stage, sem):
    chunk = stage.shape[1]
    n = w_scr.shape[0] // chunk

    def copy(c):
        return pltpu.make_async_copy(w_hbm.at[layer, pl.ds(c * chunk, chunk), :], stage.at[c % 2], sem.at[c % 2])

    copy(0).start()
    for c in range(n):
        if c + 1 < n:
            copy(c + 1).start()
        copy(c).wait()
        w_scr[c * chunk:(c + 1) * chunk, :] = stage[c % 2].astype(BF16)


def _weight_scratch(rows, cols, chunk):
    return [pltpu.VMEM((rows, cols), BF16), pltpu.VMEM((2, chunk, cols), F32), pltpu.SemaphoreType.DMA((2,))]


HBM = pl.BlockSpec(memory_space=pl.ANY)


def _ada_kernel(cv_ref, w_ref, b_ref, o_ref):
    a = _silu(cv_ref[...]).astype(BF16)
    o_ref[...] = jnp.dot(a, w_ref[...].astype(BF16), preferred_element_type=F32) + b_ref[...]


def _ada_call(cv, w_ada, b_ada):
    return pl.pallas_call(
        _ada_kernel,
        out_shape=jax.ShapeDtypeStruct((MOD_ROWS, GATE_COLS), F32),
        grid=(GATE_COLS // ADA_TILE,),
        in_specs=[
            pl.BlockSpec((MOD_ROWS, D_MODEL), lambda j: (0, 0)),
            pl.BlockSpec((None, D_MODEL, ADA_TILE), lambda j: (0, 0, j)),
            pl.BlockSpec((None, 1, ADA_TILE), lambda j: (0, 0, j)),
        ],
        out_specs=pl.BlockSpec((MOD_ROWS, ADA_TILE), lambda j: (0, j)),
        compiler_params=_params("arbitrary"),
        name="ada_mod",
    )(cv, w_ada, b_ada)


def _normmod_kernel(xc_ref, xl_ref, mod_ref, nw_ref, h_ref, *, ctx_tiles):
    def emit(x_ref):
        def chunk(r, carry):
            rows = pl.ds(pl.multiple_of(r * NORM_ROWS, NORM_ROWS), NORM_ROWS)
            y = _rms(x_ref[rows, :], nw_ref[...])
            h_ref[rows, :] = (y * (1.0 + mod_ref[:, D_MODEL:2 * D_MODEL]) + mod_ref[:, 0:D_MODEL]).astype(BF16)
            return carry

        lax.fori_loop(0, h_ref.shape[0] // NORM_ROWS, chunk, 0, unroll=8)

    @pl.when(pl.program_id(0) < ctx_tiles)
    def _():
        emit(xc_ref)

    @pl.when(pl.program_id(0) >= ctx_tiles)
    def _():
        emit(xl_ref)


def _normmod_call(x_ctx, x_lat, mod3, norm_w, *, seq):
    st = _Stream(x_ctx.shape[0], x_lat.shape[0], seq, NORM_TOKEN_TILE)
    return pl.pallas_call(
        functools.partial(_normmod_kernel, ctx_tiles=st.ctx_tiles),
        out_shape=jax.ShapeDtypeStruct((st.tiles * st.tm, D_MODEL), BF16),
        grid=(st.tiles,),
        in_specs=st.two_source_specs(D_MODEL) + [
            pl.BlockSpec((None, 1, GATE_COLS), st.mod_index),
            pl.BlockSpec((1, D_MODEL), lambda i: (0, 0)),
        ],
        out_specs=pl.BlockSpec((st.tm, D_MODEL), lambda i: (i, 0)),
        compiler_params=_params("arbitrary"),
        name="norm_mod",
    )(x_ctx, x_lat, mod3, norm_w)


def _gates_kernel(h_ref, w_hbm, b_ref, o_ref, w_ref, stage, sem, *, layer):
    @pl.when(pl.program_id(0) == 0)
    def _():
        _load_weight_bf16(w_hbm, layer, w_ref, stage, sem)

    for s in range(GATE_COLS // SUB_COLS):
        cs = slice(s * SUB_COLS, (s + 1) * SUB_COLS)
        acc = jnp.dot(h_ref[...], w_ref[:, cs], preferred_element_type=F32)
        o_ref[:, cs] = jax.nn.sigmoid(acc + b_ref[:, cs]).astype(BF16)


def _gates_call(h, w_merge, b_merge, *, layer):
    t = h.shape[0]
    tm = IN_TOKEN_TILE
    return pl.pallas_call(
        functools.partial(_gates_kernel, layer=layer),
        out_shape=jax.ShapeDtypeStruct((t, GATE_COLS), BF16),
        grid=(t // tm,),
        in_specs=[
            pl.BlockSpec((tm, D_MODEL), lambda i: (i, 0)),
            HBM,
            _resident((1, GATE_COLS)),
        ],
        out_specs=pl.BlockSpec((tm, GATE_COLS), lambda i: (i, 0)),
        scratch_shapes=_weight_scratch(D_MODEL, GATE_COLS, CAST_ROWS),
        compiler_params=_params("arbitrary"),
        name="gates",
    )(h, w_merge, b_merge)


def _rope(y, cos, sa, sb):
    return (y * cos + pltpu.roll(y, HEAD_DIM - ROPE_PARTNER, 1) * sa + pltpu.roll(y, ROPE_PARTNER, 1) * sb)


def _proj_kernel(h_ref, w_hbm, qnw_ref, knw_ref, cos_ref, sa_ref, sb_ref, proj_ref, ks_ref, vs_ref,
                 w_ref, stage, sem, *, layer):
    @pl.when(pl.program_id(0) == 0)
    def _():
        _load_weight_bf16(w_hbm, layer, w_ref, stage, sem)

    def head(xh, w):
        return _rope(_rms(xh, w), cos_ref[...], sa_ref[...], sb_ref[...])

    qw = qnw_ref[...] * (ATTN_SCALE * LOG2_E)
    silu_cols = ((GA_TILE0 * COL_TILE, U_TILE * COL_TILE), (GB_TILE * COL_TILE, Z_TILE * COL_TILE),
                 (GC_TILE * COL_TILE, IN_COLS))
    for s in range(IN_COLS // SUB_COLS):
        acc = jnp.dot(h_ref[...], w_ref[:, s * SUB_COLS:(s + 1) * SUB_COLS], preferred_element_type=F32)
        for hb in range(SUB_COLS // HEAD_DIM):
            c0 = s * SUB_COLS + hb * HEAD_DIM
            cs = slice(c0, c0 + HEAD_DIM)
            a = acc[:, hb * HEAD_DIM:(hb + 1) * HEAD_DIM]
            if c0 < K_COL:
                proj_ref[:, cs] = head(a, qw).astype(BF16)
            elif c0 < V_COL:
                ks_ref[:, c0 - K_COL:c0 - K_COL + HEAD_DIM] = _rms(a, knw_ref[...])
                proj_ref[:, cs] = head(a, knw_ref[...]).astype(BF16)
            elif c0 < V_COL + KV_WIDTH:
                vs_ref[:, c0 - V_COL:c0 - V_COL + HEAD_DIM] = a
                proj_ref[:, cs] = a.astype(BF16)
            elif any(lo <= c0 < hi for lo, hi in silu_cols):
                proj_ref[:, cs] = _silu(a).astype(BF16)
            else:
                proj_ref[:, cs] = a.astype(BF16)


def _proj_call(h, w_in, qnw, knw, rope_tabs, *, st, layer):
    t = h.shape[0]
    tm = st.tm

    def tab_index(i):
        return (jnp.where(i < st.ctx_tiles, 0, 1 + (i - st.ctx_tiles) % st.tiles_per_seq), 0)

    in_specs = [
        pl.BlockSpec((tm, D_MODEL), lambda i: (i, 0)),
        HBM,
        pl.BlockSpec((1, HEAD_DIM), lambda i: (0, 0)),
        pl.BlockSpec((1, HEAD_DIM), lambda i: (0, 0)),
    ] + [pl.BlockSpec((tm, HEAD_DIM), tab_index) for _ in rope_tabs]
    out_shape = [jax.ShapeDtypeStruct((t, IN_COLS), BF16),
                 jax.ShapeDtypeStruct((t, KV_WIDTH), F32), jax.ShapeDtypeStruct((t, KV_WIDTH), F32)]
    out_specs = [pl.BlockSpec((tm, IN_COLS), lambda i: (i, 0)),
                 pl.BlockSpec((tm, KV_WIDTH), lambda i: (i, 0)), pl.BlockSpec((tm, KV_WIDTH), lambda i: (i, 0))]
    return pl.pallas_call(
        functools.partial(_proj_kernel, layer=layer),
        out_shape=out_shape,
        grid=(t // tm,),
        in_specs=in_specs,
        out_specs=out_specs,
        scratch_shapes=_weight_scratch(D_MODEL, IN_COLS, CAST_ROWS),
        compiler_params=_params("arbitrary"),
        name="proj",
    )(h, w_in, qnw, knw, *rope_tabs)


def _attn_kernel(*refs, ctx, n_seq):
    q_ref, k_ref, v_ref, ga0_ref, ga1_ref = refs[:5]
    ga_refs = (ga0_ref, ga1_ref)
    refs = refs[5:]
    if ctx:
        ck_ref, cv_ref = refs[:2]
        refs = refs[2:]
    o_ref, vx_scr = refs[:2]
    if ctx:
        ckx_scr, cvx_scr = refs[2:]
    seq = k_ref.shape[0] // n_seq
    tq = q_ref.shape[0] // n_seq

    @pl.when(pl.program_id(1) == 0)
    def _():
        for j in range(n_seq):
            for kh in range(A_KV_HEADS):
                hs = slice(kh * HEAD_DIM, (kh + 1) * HEAD_DIM)
                vx_scr[j, kh, :, :HEAD_DIM] = v_ref[j * seq:(j + 1) * seq, hs]
                vx_scr[j, kh, :, HEAD_DIM:] = jnp.ones((seq, HEAD_DIM), BF16)
        if ctx:
            for kh in range(A_KV_HEADS):
                ckx_scr[kh] = ck_ref[:, kh, :].astype(BF16)
                cvx_scr[kh, :, :HEAD_DIM] = cv_ref[:, kh, :].astype(BF16)
                cvx_scr[kh, :, HEAD_DIM:] = jnp.ones((cvx_scr.shape[1], HEAD_DIM), BF16)

    nt = (((1,), (1,)), ((), ()))
    for j in range(n_seq):
        qrows = slice(j * tq, (j + 1) * tq)
        for head in range(A_HEADS):
            kh, hh = divmod(head, HEADS_PER_TILE)
            cs = slice(head * HEAD_DIM, (head + 1) * HEAD_DIM)
            q = q_ref[qrows, cs]
            k = k_ref[j * seq:(j + 1) * seq, kh * HEAD_DIM:(kh + 1) * HEAD_DIM]
            s1 = lax.dot_general(q, k, nt, preferred_element_type=F32)
            m = jnp.max(s1, axis=-1, keepdims=True)
            if ctx:
                s2 = lax.dot_general(q, ckx_scr[kh], nt, preferred_element_type=F32)
                m = jnp.maximum(m, jnp.max(s2, axis=-1, keepdims=True))
            ox = jnp.dot(jnp.exp2(s1 - m).astype(BF16), vx_scr[j, kh], preferred_element_type=F32)
            if ctx:
                ox = ox + jnp.dot(jnp.exp2(s2 - m).astype(BF16), cvx_scr[kh], preferred_element_type=F32)
            o = ox[:, :HEAD_DIM] / ox[:, HEAD_DIM:]
            ga = ga_refs[kh][qrows, hh * HEAD_DIM:(hh + 1) * HEAD_DIM]
            o_ref[qrows, cs] = (o * ga.astype(F32)).astype(BF16)


def _attn_specs(proj, cache, *, row0, n_batch, seq, layer, n_seq):
    tq = min(SEQ_TILE, seq)
    nq = seq // tq
    qb, sb = tq * n_seq, seq * n_seq
    q0, s0 = row0 // qb, row0 // sb
    in_specs = [
        pl.BlockSpec((qb, A_WIDTH), lambda b, qi: (q0 + b * nq + qi, Q_TILE0 * COL_TILE // A_WIDTH)),
        pl.BlockSpec((sb, KV_WIDTH), lambda b, qi: (s0 + b, K_COL // KV_WIDTH)),
        pl.BlockSpec((sb, KV_WIDTH), lambda b, qi: (s0 + b, V_COL // KV_WIDTH)),
        pl.BlockSpec((qb, COL_TILE), lambda b, qi: (q0 + b * nq + qi, GA_TILE0)),
        pl.BlockSpec((qb, COL_TILE), lambda b, qi: (q0 + b * nq + qi, GA_TILE0 + 1)),
    ]
    args = [proj, proj, proj, proj, proj]
    scratch = [pltpu.VMEM((n_seq, A_KV_HEADS, seq, 2 * HEAD_DIM), BF16)]
    if cache is not None:
        past = cache[0].shape[2]
        for c in cache:
            in_specs.append(pl.BlockSpec((None, None, past, A_KV_HEADS, HEAD_DIM),
                                         lambda b, qi: (b, layer, 0, 0, 0)))
            args.append(c)
        scratch += [pltpu.VMEM((A_KV_HEADS, past, HEAD_DIM), BF16),
                    pltpu.VMEM((A_KV_HEADS, past, 2 * HEAD_DIM), BF16)]
    out_shape = [jax.ShapeDtypeStruct((n_batch * seq, A_WIDTH), BF16)]
    out_specs = [pl.BlockSpec((qb, A_WIDTH), lambda b, qi: (b * nq + qi, 0))]
    return in_specs, args, out_shape, out_specs, scratch


def _mix_kernel(u_ref, vb_ref, gb_ref, z_ref, gc_ref, snw_ref, ws_ref, bs_ref, wp_ref, ps_ref, band_ref, inv_ref,
                bo_ref, co_ref, zb_scr, *, n_seq):
    r = u_ref.shape[0] // n_seq
    c = pl.program_id(1)
    nc = pl.num_programs(1)
    base = pl.multiple_of(c * r, r)
    n_chunks = r // CHUNK
    z_rows = z_ref.shape[0]
    for j in range(n_seq):
        row0 = j * r

        vbn = _rms(vb_ref[row0:row0 + r, :].astype(F32), snw_ref[...]).astype(BF16)
        for g in range(B_GROUPS):
            cs = slice(g * HEAD_DIM, (g + 1) * HEAD_DIM)
            wide = jnp.concatenate([vbn[cc * CHUNK:(cc + 1) * CHUNK, cs] for cc in range(n_chunks)], axis=1)
            mixed = jnp.dot(ws_ref[g], wide, preferred_element_type=F32) + bs_ref[g]
            for cc in range(n_chunks):
                rs = slice(row0 + cc * CHUNK, row0 + (cc + 1) * CHUNK)
                mx = mixed[:, cc * HEAD_DIM:(cc + 1) * HEAD_DIM]
                bo_ref[rs, cs] = (u_ref[rs, cs].astype(F32) * mx * gb_ref[rs, cs].astype(F32)).astype(BF16)

        zb = zb_scr.at[j]
        z0 = base + row0
        zb[POOL_HALO:POOL_HALO + r, :] = z_ref[pl.ds(pl.multiple_of(z0, POOL_HALO), r), :]
        above = z_ref[pl.ds(pl.multiple_of(jnp.maximum(z0 - POOL_HALO, 0), POOL_HALO), POOL_HALO), :]
        below = z_ref[pl.ds(pl.multiple_of(jnp.minimum(z0 + r, z_rows - POOL_HALO), POOL_HALO), POOL_HALO), :]
        zeros = jnp.zeros((POOL_HALO, C_WIDTH), BF16)
        zb[0:POOL_HALO, :] = jnp.where(c > 0, above, zeros)
        zb[POOL_HALO + r:, :] = jnp.where(c < nc - 1, below, zeros)

        for g in range(len(POOL_WINDOWS)):
            cs = slice(g * HEAD_DIM, (g + 1) * HEAD_DIM)
            ds = []
            for blk in range(n_chunks):
                lo = blk * CHUNK
                wsum = jnp.dot(band_ref[g], zb[lo:lo + CHUNK + 2 * POOL_HALO, cs], preferred_element_type=F32)
                zc = zb[POOL_HALO + lo:POOL_HALO + lo + CHUNK, cs].astype(F32)
                ds.append((wsum * inv_ref[lo:lo + CHUNK, cs] - zc).astype(BF16))
            dm = jnp.dot(jnp.concatenate(ds, axis=0), wp_ref[g], preferred_element_type=F32)
            rows = slice(row0, row0 + r)
            co_ref[rows, cs] = (dm * ps_ref[:, cs] * gc_ref[rows, cs].astype(F32)).astype(BF16)


def _pool_band():
    t = np.arange(CHUNK)[:, None] + POOL_HALO
    j = np.arange(CHUNK + 2 * POOL_HALO)[None, :]
    return jnp.asarray(np.stack([(j >= t - w // 2) & (j < t + w - w // 2) for w in POOL_WINDOWS]), dtype=BF16)


def _pool_inv_count(seq):
    t = np.arange(seq)
    cols = [1.0 / (np.clip(t + w - w // 2, 0, seq) - np.clip(t - w // 2, 0, seq)) for w in POOL_WINDOWS]
    return jnp.asarray(np.repeat(np.stack(cols, axis=1), HEAD_DIM, axis=1), dtype=F32)


def _mix_specs(proj, snw, ws, bs, wp, ps, *, row0, n_batch, seq, n_seq):
    r = min(SEQ_TILE, seq)
    nc = seq // r
    rb, sb = r * n_seq, seq * n_seq
    r0, s0 = row0 // rb, row0 // sb
    tile = lambda col: pl.BlockSpec((rb, COL_TILE), lambda b, c: (r0 + b * nc + c, col))
    const3 = lambda shape: pl.BlockSpec(shape, lambda b, c: (0, 0, 0))
    in_specs = [
        tile(U_TILE), tile(VB_TILE), tile(GB_TILE),
        pl.BlockSpec((sb, COL_TILE), lambda b, c: (s0 + b, Z_TILE)),
        tile(GC_TILE),
        pl.BlockSpec((1, B_WIDTH), lambda b, c: (0, 0)),
        const3((B_GROUPS, CHUNK, CHUNK)),
        const3((B_GROUPS, CHUNK, 1)),
        const3((B_GROUPS, HEAD_DIM, HEAD_DIM)),
        pl.BlockSpec((1, C_WIDTH), lambda b, c: (0, 0)),
        const3((len(POOL_WINDOWS), CHUNK, CHUNK + 2 * POOL_HALO)),
        pl.BlockSpec((r, C_WIDTH), lambda b, c: (c, 0)),
    ]
    args = [proj, proj, proj, proj, proj, snw, ws, bs, wp, ps, _pool_band(), _pool_inv_count(seq)]
    out_shape = [jax.ShapeDtypeStruct((n_batch * seq, B_WIDTH), BF16),
                 jax.ShapeDtypeStruct((n_batch * seq, C_WIDTH), BF16)]
    out_spec = pl.BlockSpec((rb, COL_TILE), lambda b, c: (b * nc + c, 0))
    scratch = [pltpu.VMEM((n_seq, r + 2 * POOL_HALO, C_WIDTH), BF16)]
    return in_specs, args, out_shape, [out_spec, out_spec], scratch


def _seq_kernel(*refs, ctx, n_seq, n_attn_in, n_mix_in, n_attn_scr, ada):
    attn_in, refs = refs[:n_attn_in], refs[n_attn_in:]
    mix_in, refs = refs[:n_mix_in], refs[n_mix_in:]
    if ada:
        ada_in, refs = refs[:3], refs[3:]
    o_ref, bo_ref, co_ref = refs[:3]
    refs = refs[3:]
    if ada:
        mo_ref, refs = refs[0], refs[1:]
    attn_scr, (zb_scr,) = refs[:n_attn_scr], refs[n_attn_scr:]
    _attn_kernel(*attn_in, o_ref, *attn_scr, ctx=ctx, n_seq=n_seq)
    _mix_kernel(*mix_in, bo_ref, co_ref, zb_scr, n_seq=n_seq)
    if ada:
        _ada_kernel(*ada_in, mo_ref)


def _seq_call(proj, cache, mix_w, ada_next, *, row0, n_batch, seq, layer):
    whole = seq <= SEQ_TILE
    n_seq = max(SHORT_SEQ_ROWS // seq, 1) if whole else 1
    assert n_batch % n_seq == 0 and (whole or seq % SEQ_TILE == 0)
    a_in, a_args, a_shape, a_out, a_scr = _attn_specs(proj, cache, row0=row0, n_batch=n_batch, seq=seq,
                                                      layer=layer, n_seq=n_seq)
    m_in, m_args, m_shape, m_out, m_scr = _mix_specs(proj, *mix_w, row0=row0, n_batch=n_batch, seq=seq, n_seq=n_seq)
    ctx = cache is not None
    grid = (n_batch // n_seq, max(seq // SEQ_TILE, 1))
    d_in, d_args, d_shape, d_out = [], [], [], []
    if ada_next is not None:
        n_blocks = GATE_COLS // ADA_STEP_COLS
        assert grid[0] * grid[1] >= n_blocks
        blk = lambda b, c: jnp.minimum(b * grid[1] + c, n_blocks - 1)
        d_in = [pl.BlockSpec((MOD_ROWS, D_MODEL), lambda b, c: (0, 0)),
                pl.BlockSpec((None, D_MODEL, ADA_STEP_COLS), lambda b, c: (layer + 1, 0, blk(b, c))),
                pl.BlockSpec((None, 1, ADA_STEP_COLS), lambda b, c: (layer + 1, 0, blk(b, c)))]
        d_args = list(ada_next)
        d_shape = [jax.ShapeDtypeStruct((MOD_ROWS, GATE_COLS), F32)]
        d_out = [pl.BlockSpec((MOD_ROWS, ADA_STEP_COLS), lambda b, c: (0, blk(b, c)))]
    return pl.pallas_call(
        functools.partial(_seq_kernel, ctx=ctx, n_seq=n_seq, n_attn_in=len(a_in), n_mix_in=len(m_in),
                          n_attn_scr=len(a_scr), ada=bool(d_in)),
        out_shape=a_shape + m_shape + d_shape,
        grid=grid,
        in_specs=a_in + m_in + d_in,
        out_specs=a_out + m_out + d_out,
        scratch_shapes=a_scr + m_scr,
        compiler_params=_params("arbitrary", "arbitrary"),
        name="attn_mix_lat" if ctx else "attn_mix_ctx",
    )(*a_args, *m_args, *d_args)


def _outproj_kernel(*refs, first, final, ctx_tiles, layer):
    (ac_ref, al_ref, bc_ref, bl_ref, cc_ref, cl_ref, g0_ref, g1_ref, g2_ref) = refs[:9]
    refs = refs[9:]
    if first:
        xc_ref, xl_ref = refs[:2]
        refs = refs[2:]
    else:
        x_ref = refs[0]
        refs = refs[1:]
    mod_ref, wa_hbm, wb_hbm, wc_hbm, wo_hbm, nw_ref = refs[:6]
    refs = refs[6:]
    if final:
        yc_ref, yl_ref = refs[:2]
        refs = refs[2:]
    else:
        modn_ref, y_ref, hn_ref = refs[:3]
        refs = refs[3:]
    a_scr, b_scr, c_scr, m_scr, wa_ref, wb_ref, wc_ref, wo_ref, stage, sem = refs[:10]
    y_scr = refs[10] if final else y_ref

    @pl.when(pl.program_id(0) == 0)
    def _():
        for w_hbm, w_ref in ((wa_hbm, wa_ref), (wb_hbm, wb_ref), (wc_hbm, wc_ref), (wo_hbm, wo_ref)):
            _load_weight_bf16(w_hbm, layer, w_ref, stage, sem)

    is_ctx = pl.program_id(0) < ctx_tiles
    a_scr[...] = _pick(is_ctx, ac_ref, al_ref)
    b_scr[...] = _pick(is_ctx, bc_ref, bl_ref)
    c_scr[...] = _pick(is_ctx, cc_ref, cl_ref)
    for n in range(D_MODEL // COL_TILE):
        cs = slice(n * COL_TILE, (n + 1) * COL_TILE)
        a = jnp.dot(a_scr[...], wa_ref[:, cs], preferred_element_type=F32)
        b = jnp.dot(b_scr[...], wb_ref[:, cs], preferred_element_type=F32)
        c = jnp.dot(c_scr[...], wc_ref[:, cs], preferred_element_type=F32)
        m = (g0_ref[:, cs].astype(F32) * a + g1_ref[:, cs].astype(F32) * b
             + g2_ref[:, cs].astype(F32) * c)
        m_scr[:, cs] = m.astype(BF16)
    ssq = jnp.zeros((m_scr.shape[0], 1), F32)
    for n in range(D_MODEL // COL_TILE):
        cs = slice(n * COL_TILE, (n + 1) * COL_TILE)
        out = jnp.dot(m_scr[...], wo_ref[:, cs], preferred_element_type=F32)
        x = jnp.where(is_ctx, xc_ref[:, cs], xl_ref[:, cs]) if first else x_ref[:, cs]
        y = x + mod_ref[:, 2 * D_MODEL + n * COL_TILE:2 * D_MODEL + (n + 1) * COL_TILE] * out
        y_scr[:, cs] = y
        ssq = ssq + jnp.sum(y * y, axis=-1, keepdims=True)
    yn = y_scr[...] * lax.rsqrt(ssq * (1.0 / D_MODEL) + EPS) * nw_ref[...]
    if final:
        @pl.when(is_ctx)
        def _():
            yc_ref[...] = yn

        @pl.when(jnp.logical_not(is_ctx))
        def _():
            yl_ref[...] = yn
    else:
        hn_ref[...] = (yn * (1.0 + modn_ref[:, D_MODEL:2 * D_MODEL]) + modn_ref[:, 0:D_MODEL]).astype(BF16)


def _outproj_call(attn, bout, cout, gates, x, mod3, wa, wb, wc, wo, nw, modn3, *, st, layer):
    tm = st.tm
    first = isinstance(x, tuple)
    final = modn3 is None
    gate = lambda g: pl.BlockSpec((tm, D_MODEL), lambda i: (i, g))
    row = pl.BlockSpec((tm, D_MODEL), lambda i: (i, 0))
    in_specs = (st.two_source_specs(A_WIDTH) + st.two_source_specs(B_WIDTH) + st.two_source_specs(C_WIDTH)
                + [gate(0), gate(1), gate(2)]
                + (st.two_source_specs(D_MODEL) if first else [row])
                + [pl.BlockSpec((None, 1, GATE_COLS), st.mod_index),
                   HBM, HBM, HBM, HBM,
                   pl.BlockSpec((1, D_MODEL), lambda i: (0, 0))])
    args = [*attn, *bout, *cout, gates, gates, gates, *(x if first else (x,)), mod3, wa, wb, wc, wo, nw]
    scratch = [pltpu.VMEM((tm, A_WIDTH), BF16), pltpu.VMEM((tm, B_WIDTH), BF16),
               pltpu.VMEM((tm, C_WIDTH), BF16), pltpu.VMEM((tm, D_MODEL), BF16),
               pltpu.VMEM((A_WIDTH, D_MODEL), BF16), pltpu.VMEM((B_WIDTH, D_MODEL), BF16),
               pltpu.VMEM((C_WIDTH, D_MODEL), BF16), pltpu.VMEM((D_MODEL, D_MODEL), BF16),
               pltpu.VMEM((2, OUT_CAST_ROWS, D_MODEL), F32), pltpu.SemaphoreType.DMA((2,))]
    if final:
        out_shape = [jax.ShapeDtypeStruct((st.ctx_tiles * tm, D_MODEL), F32),
                     jax.ShapeDtypeStruct(((st.tiles - st.ctx_tiles) * tm, D_MODEL), F32)]
        out_specs = st.two_source_specs(D_MODEL)
        scratch.append(pltpu.VMEM((tm, D_MODEL), F32))
    else:
        in_specs.append(pl.BlockSpec((None, 1, GATE_COLS), st.mod_index))
        args.append(modn3)
        out_shape = [jax.ShapeDtypeStruct((st.tiles * tm, D_MODEL), F32),
                     jax.ShapeDtypeStruct((st.tiles * tm, D_MODEL), BF16)]
        out_specs = [row, row]
    return pl.pallas_call(
        functools.partial(_outproj_kernel, first=first, final=final, ctx_tiles=st.ctx_tiles, layer=layer),
        out_shape=out_shape,
        grid=(st.tiles,),
        in_specs=in_specs,
        out_specs=out_specs,
        scratch_shapes=scratch,
        compiler_params=_params("arbitrary"),
        name="out_proj_final" if final else "out_proj",
    )(*args)


def _rope_tables(n_identity, n_tokens):
    rows = n_tokens // GRID_W
    row = np.repeat(np.arange(rows), GRID_W).astype(np.float64)
    col = np.tile(np.arange(GRID_W), rows).astype(np.float64)
    n_freq = HEAD_DIM // 4
    inv = ROPE_THETA ** (-np.arange(n_freq, dtype=np.float64) / n_freq)
    ar = row[:, None] * inv[None, :]
    ac = col[:, None] * inv[None, :]
    zero = np.zeros_like(ar)
    cos = np.concatenate([np.cos(ar), np.cos(ar), np.cos(ac), np.cos(ac)], axis=-1)
    sa = np.concatenate([-np.sin(ar), zero, -np.sin(ac), zero], axis=-1)
    sb = np.concatenate([zero, np.sin(ar), zero, np.sin(ac)], axis=-1)
    ident = (np.ones((n_identity, HEAD_DIM)), np.zeros((n_identity, HEAD_DIM)), np.zeros((n_identity, HEAD_DIM)))
    return tuple(jnp.asarray(np.concatenate([i, t], axis=0), dtype=F32) for i, t in zip(ident, (cos, sa, sb)))


def kernel(x_prompt, x_sample, cache_k, cache_v, c, c_ctx, norm_w, w_ada, b_ada, w_in, q_norm_w,
           k_norm_w, sgu_norm_w, w_sgu, b_sgu, w_pool, pool_scale, w_br_a, w_br_b, w_br_c, w_merge,
           b_merge, w_out, final_norm_w):
    nb_p, seq_p, d = x_prompt.shape
    nb_s, seq_s, _ = x_sample.shape
    n_ctx, n_lat = nb_p * seq_p, nb_s * seq_s
    assert d == D_MODEL and nb_s + 1 <= MOD_ROWS

    cv = jnp.concatenate([c_ctx[None, :], c, jnp.zeros((MOD_ROWS - 1 - nb_s, d), F32)], axis=0)
    b_ada3 = b_ada.reshape(DEPTH, 1, GATE_COLS)
    mod3 = [_ada_call(cv, w_ada, b_ada3).reshape(MOD_ROWS, 1, GATE_COLS)]
    st_in = _Stream(n_ctx, n_lat, seq_s, IN_TOKEN_TILE)
    st_out = _Stream(n_ctx, n_lat, seq_s, OUT_TOKEN_TILE)
    rope_tabs = _rope_tables(IN_TOKEN_TILE, seq_s)
    cache = (cache_k, cache_v)

    x = (x_prompt.reshape(n_ctx, d), x_sample.reshape(n_lat, d))
    h = _normmod_call(*x, mod3[0], norm_w[0].reshape(1, d), seq=seq_s)
    states = []
    w_mg, w_i, wa, wb, wc, wo = w_merge, w_in, w_br_a, w_br_b, w_br_c, w_out
    for l in range(DEPTH):
        last = l == DEPTH - 1
        bm = b_merge[l].reshape(1, GATE_COLS)
        qnw = q_norm_w[l].reshape(1, HEAD_DIM)
        knw = k_norm_w[l].reshape(1, HEAD_DIM)
        mix_w = (sgu_norm_w[l].reshape(1, B_WIDTH), w_sgu[l].astype(BF16),
                 b_sgu[l].reshape(B_GROUPS, CHUNK, 1), w_pool[l].astype(BF16),
                 pool_scale[l].reshape(1, C_WIDTH))
        nw_next = (final_norm_w if last else norm_w[l + 1]).reshape(1, d)

        gates = _gates_call(h, w_mg, bm, layer=l)
        proj, ks, vs = _proj_call(h, w_i, qnw, knw, rope_tabs, st=st_in, layer=l)
        states.append((ks, vs))
        seq_c = _seq_call(proj, None, mix_w, None, row0=0, n_batch=nb_p, seq=seq_p, layer=l)
        seq_l = _seq_call(proj, cache, mix_w, None if last else (cv, w_ada, b_ada3),
                          row0=n_ctx, n_batch=nb_s, seq=seq_s, layer=l)
        if not last:
            *seq_l, mod_next = seq_l
            mod3.append(mod_next.reshape(MOD_ROWS, 1, GATE_COLS))
        attn, bout, cout = zip(seq_c, seq_l)
        res = _outproj_call(attn, bout, cout, gates, x, mod3[l],
                            wa, wb, wc, wo, nw_next, None if last else mod3[l + 1], st=st_out, layer=l)
        if last:
            y_ctx, y_lat = res
        else:
            x, h = res

    state_k, state_v = (
        jnp.stack([s[i][:n_ctx].reshape(nb_p, seq_p, A_KV_HEADS, HEAD_DIM) for s in states], axis=1)
        for i in range(2))
    return (y_ctx.reshape(nb_p, seq_p, d), y_lat.reshape(nb_s, seq_s, d), state_k, state_v)
```

```python
import functools

import jax
import jax.numpy as jnp
import numpy as np
from jax import lax
from jax.experimental import pallas as pl
from jax.experimental.pallas import tpu as pltpu

F32 = jnp.float32
BF16 = jnp.bfloat16

D_MODEL = 2048
DEPTH = 2
GRID_W = 64
EPS = 1e-6
HEAD_DIM = 128
A_HEADS = 8
A_KV_HEADS = 2
A_WIDTH = A_HEADS * HEAD_DIM
KV_WIDTH = A_KV_HEADS * HEAD_DIM
ROPE_THETA = 10000.0
ROPE_PARTNER = HEAD_DIM // 4
ATTN_SCALE = HEAD_DIM ** -0.5
LOG2_E = 1.4426950408889634
CHUNK = 128
B_GROUPS = 4
B_WIDTH = 512
C_WIDTH = 512
POOL_WINDOWS = (2, 4, 8, 16)
POOL_HALO = 64
N_BRANCH = 3
GATE_COLS = N_BRANCH * D_MODEL
IN_COLS = 2 * A_WIDTH + 2 * KV_WIDTH + 3 * B_WIDTH + 2 * C_WIDTH

COL_TILE = 512
Q_TILE0 = 0
KV_TILE = Q_TILE0 + A_WIDTH // COL_TILE
GA_TILE0 = KV_TILE + 1
U_TILE = GA_TILE0 + A_WIDTH // COL_TILE
VB_TILE = U_TILE + 1
GB_TILE = VB_TILE + 1
Z_TILE = GB_TILE + 1
GC_TILE = Z_TILE + 1
HEADS_PER_TILE = COL_TILE // HEAD_DIM
K_COL = KV_TILE * COL_TILE
V_COL = K_COL + KV_WIDTH

SUB_COLS = 256
MOD_ROWS = 8
ADA_TILE = 1024
ADA_STEP_COLS = 384
IN_TOKEN_TILE = 512
NORM_TOKEN_TILE = 1024
NORM_ROWS = 16
OUT_TOKEN_TILE = 256
SEQ_TILE = 512
SHORT_SEQ_ROWS = 1024
CAST_ROWS = 128
OUT_CAST_ROWS = 256
VMEM_LIMIT = 56 * 1024 * 1024


def _params(*sem):
    return pltpu.CompilerParams(dimension_semantics=sem, vmem_limit_bytes=VMEM_LIMIT)


def _resident(shape):
    return pl.BlockSpec(shape, lambda *_: (0,) * len(shape), pipeline_mode=pl.Buffered(1))


def _silu(x):
    return x * jax.nn.sigmoid(x)


def _rms(x, w):
    ms = jnp.mean(x * x, axis=-1, keepdims=True)
    return x * lax.rsqrt(ms + EPS) * w


class _Stream:
    def __init__(self, n_ctx, n_lat, seq, tm):
        assert n_ctx % tm == 0 and seq % tm == 0
        self.tm = tm
        self.ctx_tiles = n_ctx // tm
        self.tiles = (n_ctx + n_lat) // tm
        self.tiles_per_seq = seq // tm

    def mod_index(self, i):
        lat = 1 + (i - self.ctx_tiles) // self.tiles_per_seq
        return (jnp.where(i < self.ctx_tiles, 0, lat), 0, 0)

    def two_source_specs(self, width):
        ctx = pl.BlockSpec((self.tm, width), lambda i: (jnp.minimum(i, self.ctx_tiles - 1), 0))
        lat = pl.BlockSpec((self.tm, width), lambda i: (jnp.maximum(i - self.ctx_tiles, 0), 0))
        return [ctx, lat]


def _pick(is_ctx, ctx_ref, lat_ref):
    return jnp.where(is_ctx, ctx_ref[...], lat_ref[...])


def _load_weight_bf16(w_hbm, layer, w_scr, stage, sem):
    chunk = stage.shape[1]
    n = w_scr.shape[0] // chunk

    def copy(c):
        return pltpu.make_async_copy(w_hbm.at[layer, pl.ds(c * chunk, chunk), :], stage.at[c % 2], sem.at[c % 2])

    copy(0).start()
    for c in range(n):
        if c + 1 < n:
            copy(c + 1).start()
        copy(c).wait()
        w_scr[c * chunk:(c + 1) * chunk, :] = stage[c % 2].astype(BF16)


def _weight_scratch(rows, cols, chunk):
    return [pltpu.VMEM((rows, cols), BF16), pltpu.VMEM((2, chunk, cols), F32), pltpu.SemaphoreType.DMA((2,))]


HBM = pl.BlockSpec(memory_space=pl.ANY)


def _ada_kernel(cv_ref, w_ref, b_ref, o_ref):
    a = _silu(cv_ref[...]).astype(BF16)
    o_ref[...] = jnp.dot(a, w_ref[...].astype(BF16), preferred_element_type=F32) + b_ref[...]


def _ada_call(cv, w_ada, b_ada):
    return pl.pallas_call(
        _ada_kernel,
        out_shape=jax.ShapeDtypeStruct((MOD_ROWS, GATE_COLS), F32),
        grid=(GATE_COLS // ADA_TILE,),
        in_specs=[
            pl.BlockSpec((MOD_ROWS, D_MODEL), lambda j: (0, 0)),
            pl.BlockSpec((None, D_MODEL, ADA_TILE), lambda j: (0, 0, j)),
            pl.BlockSpec((None, 1, ADA_TILE), lambda j: (0, 0, j)),
        ],
        out_specs=pl.BlockSpec((MOD_ROWS, ADA_TILE), lambda j: (0, j)),
        compiler_params=_params("arbitrary"),
        name="ada_mod",
    )(cv, w_ada, b_ada)


def _normmod_kernel(xc_ref, xl_ref, mod_ref, nw_ref, h_ref, *, ctx_tiles):
    def emit(x_ref):
        def chunk(r, carry):
            rows = pl.ds(pl.multiple_of(r * NORM_ROWS, NORM_ROWS), NORM_ROWS)
            y = _rms(x_ref[rows, :], nw_ref[...])
            h_ref[rows, :] = (y * (1.0 + mod_ref[:, D_MODEL:2 * D_MODEL]) + mod_ref[:, 0:D_MODEL]).astype(BF16)
            return carry

        lax.fori_loop(0, h_ref.shape[0] // NORM_ROWS, chunk, 0, unroll=8)

    @pl.when(pl.program_id(0) < ctx_tiles)
    def _():
        emit(xc_ref)

    @pl.when(pl.program_id(0) >= ctx_tiles)
    def _():
        emit(xl_ref)


def _normmod_call(x_ctx, x_lat, mod3, norm_w, *, seq):
    st = _Stream(x_ctx.shape[0], x_lat.shape[0], seq, NORM_TOKEN_TILE)
    return pl.pallas_call(
        functools.partial(_normmod_kernel, ctx_tiles=st.ctx_tiles),
        out_shape=jax.ShapeDtypeStruct((st.tiles * st.tm, D_MODEL), BF16),
        grid=(st.tiles,),
        in_specs=st.two_source_specs(D_MODEL) + [
            pl.BlockSpec((None, 1, GATE_COLS), st.mod_index),
            pl.BlockSpec((1, D_MODEL), lambda i: (0, 0)),
        ],
        out_specs=pl.BlockSpec((st.tm, D_MODEL), lambda i: (i, 0)),
        compiler_params=_params("arbitrary"),
        name="norm_mod",
    )(x_ctx, x_lat, mod3, norm_w)


def _gates_kernel(h_ref, w_hbm, b_ref, o_ref, w_ref, stage, sem, *, layer):
    @pl.when(pl.program_id(0) == 0)
    def _():
        _load_weight_bf16(w_hbm, layer, w_ref, stage, sem)

    for s in range(GATE_COLS // SUB_COLS):
        cs = slice(s * SUB_COLS, (s + 1) * SUB_COLS)
        acc = jnp.dot(h_ref[...], w_ref[:, cs], preferred_element_type=F32)
        o_ref[:, cs] = jax.nn.sigmoid(acc + b_ref[:, cs]).astype(BF16)


def _gates_call(h, w_merge, b_merge, *, layer):
    t = h.shape[0]
    tm = IN_TOKEN_TILE
    return pl.pallas_call(
        functools.partial(_gates_kernel, layer=layer),
        out_shape=jax.ShapeDtypeStruct((t, GATE_COLS), BF16),
        grid=(t // tm,),
        in_specs=[
            pl.BlockSpec((tm, D_MODEL), lambda i: (i, 0)),
            HBM,
            _resident((1, GATE_COLS)),
        ],
        out_specs=pl.BlockSpec((tm, GATE_COLS), lambda i: (i, 0)),
        scratch_shapes=_weight_scratch(D_MODEL, GATE_COLS, CAST_ROWS),
        compiler_params=_params("arbitrary"),
        name="gates",
    )(h, w_merge, b_merge)


def _rope(y, cos, sa, sb):
    return (y * cos + pltpu.roll(y, HEAD_DIM - ROPE_PARTNER, 1) * sa + pltpu.roll(y, ROPE_PARTNER, 1) * sb)


def _proj_kernel(h_ref, w_hbm, qnw_ref, knw_ref, cos_ref, sa_ref, sb_ref, proj_ref, ks_ref, vs_ref,
                 w_ref, stage, sem, *, layer):
    @pl.when(pl.program_id(0) == 0)
    def _():
        _load_weight_bf16(w_hbm, layer, w_ref, stage, sem)

    def head(xh, w):
        return _rope(_rms(xh, w), cos_ref[...], sa_ref[...], sb_ref[...])

    qw = qnw_ref[...] * (ATTN_SCALE * LOG2_E)
    silu_cols = ((GA_TILE0 * COL_TILE, U_TILE * COL_TILE), (GB_TILE * COL_TILE, Z_TILE * COL_TILE),
                 (GC_TILE * COL_TILE, IN_COLS))
    for s in range(IN_COLS // SUB_COLS):
        acc = jnp.dot(h_ref[...], w_ref[:, s * SUB_COLS:(s + 1) * SUB_COLS], preferred_element_type=F32)
        for hb in range(SUB_COLS // HEAD_DIM):
            c0 = s * SUB_COLS + hb * HEAD_DIM
            cs = slice(c0, c0 + HEAD_DIM)
            a = acc[:, hb * HEAD_DIM:(hb + 1) * HEAD_DIM]
            if c0 < K_COL:
                proj_ref[:, cs] = head(a, qw).astype(BF16)
            elif c0 < V_COL:
                ks_ref[:, c0 - K_COL:c0 - K_COL + HEAD_DIM] = _rms(a, knw_ref[...])
                proj_ref[:, cs] = head(a, knw_ref[...]).astype(BF16)
            elif c0 < V_COL + KV_WIDTH:
                vs_ref[:, c0 - V_COL:c0 - V_COL + HEAD_DIM] = a
                proj_ref[:, cs] = a.astype(BF16)
            elif any(lo <= c0 < hi for lo, hi in silu_cols):
                proj_ref[:, cs] = _silu(a).astype(BF16)
            else:
                proj_ref[:, cs] = a.astype(BF16)


def _proj_call(h, w_in, qnw, knw, rope_tabs, *, st, layer):
    t = h.shape[0]
    tm = st.tm

    def tab_index(i):
        return (jnp.where(i < st.ctx_tiles, 0, 1 + (i - st.ctx_tiles) % st.tiles_per_seq), 0)

    in_specs = [
        pl.BlockSpec((tm, D_MODEL), lambda i: (i, 0)),
        HBM,
        pl.BlockSpec((1, HEAD_DIM), lambda i: (0, 0)),
        pl.BlockSpec((1, HEAD_DIM), lambda i: (0, 0)),
    ] + [pl.BlockSpec((tm, HEAD_DIM), tab_index) for _ in rope_tabs]
    out_shape = [jax.ShapeDtypeStruct((t, IN_COLS), BF16),
                 jax.ShapeDtypeStruct((t, KV_WIDTH), F32), jax.ShapeDtypeStruct((t, KV_WIDTH), F32)]
    out_specs = [pl.BlockSpec((tm, IN_COLS), lambda i: (i, 0)),
                 pl.BlockSpec((tm, KV_WIDTH), lambda i: (i, 0)), pl.BlockSpec((tm, KV_WIDTH), lambda i: (i, 0))]
    return pl.pallas_call(
        functools.partial(_proj_kernel, layer=layer),
        out_shape=out_shape,
        grid=(t // tm,),
        in_specs=in_specs,
        out_specs=out_specs,
        scratch_shapes=_weight_scratch(D_MODEL, IN_COLS, CAST_ROWS),
        compiler_params=_params("arbitrary"),
        name="proj",
    )(h, w_in, qnw, knw, *rope_tabs)


def _attn_kernel(*refs, ctx, n_seq):
    q_ref, k_ref, v_ref, ga0_ref, ga1_ref = refs[:5]
    ga_refs = (ga0_ref, ga1_ref)
    refs = refs[5:]
    if ctx:
        ck_ref, cv_ref = refs[:2]
        refs = refs[2:]
    o_ref, vx_scr = refs[:2]
    if ctx:
        ckx_scr, cvx_scr = refs[2:]
    seq = k_ref.shape[0] // n_seq
    tq = q_ref.shape[0] // n_seq

    @pl.when(pl.program_id(1) == 0)
    def _():
        for j in range(n_seq):
            for kh in range(A_KV_HEADS):
                hs = slice(kh * HEAD_DIM, (kh + 1) * HEAD_DIM)
                vx_scr[j, kh, :, :HEAD_DIM] = v_ref[j * seq:(j + 1) * seq, hs]
                vx_scr[j, kh, :, HEAD_DIM:] = jnp.ones((seq, HEAD_DIM), BF16)
        if ctx:
            for kh in range(A_KV_HEADS):
                ckx_scr[kh] = ck_ref[:, kh, :].astype(BF16)
                cvx_scr[kh, :, :HEAD_DIM] = cv_ref[:, kh, :].astype(BF16)
                cvx_scr[kh, :, HEAD_DIM:] = jnp.ones((cvx_scr.shape[1], HEAD_DIM), BF16)

    nt = (((1,), (1,)), ((), ()))
    for j in range(n_seq):
        qrows = slice(j * tq, (j + 1) * tq)
        for head in range(A_HEADS):
            kh, hh = divmod(head, HEADS_PER_TILE)
            cs = slice(head * HEAD_DIM, (head + 1) * HEAD_DIM)
            q = q_ref[qrows, cs]
            k = k_ref[j * seq:(j + 1) * seq, kh * HEAD_DIM:(kh + 1) * HEAD_DIM]
            s1 = lax.dot_general(q, k, nt, preferred_element_type=F32)
            m = jnp.max(s1, axis=-1, keepdims=True)
            if ctx:
                s2 = lax.dot_general(q, ckx_scr[kh], nt, preferred_element_type=F32)
                m = jnp.maximum(m, jnp.max(s2, axis=-1, keepdims=True))
            ox = jnp.dot(jnp.exp2(s1 - m).astype(BF16), vx_scr[j, kh], preferred_element_type=F32)
            if ctx:
                ox = ox + jnp.dot(jnp.exp2(s2 - m).astype(BF16), cvx_scr[kh], preferred_element_type=F32)
            o = ox[:, :HEAD_DIM] / ox[:, HEAD_DIM:]
            ga = ga_refs[kh][qrows, hh * HEAD_DIM:(hh + 1) * HEAD_DIM]
            o_ref[qrows, cs] = (o * ga.astype(F32)).astype(BF16)


def _attn_specs(proj, cache, *, row0, n_batch, seq, layer, n_seq):
    tq = min(SEQ_TILE, seq)
    nq = seq // tq
    qb, sb = tq * n_seq, seq * n_seq
    q0, s0 = row0 // qb, row0 // sb
    in_specs = [
        pl.BlockSpec((qb, A_WIDTH), lambda b, qi: (q0 + b * nq + qi, Q_TILE0 * COL_TILE // A_WIDTH)),
        pl.BlockSpec((sb, KV_WIDTH), lambda b, qi: (s0 + b, K_COL // KV_WIDTH)),
        pl.BlockSpec((sb, KV_WIDTH), lambda b, qi: (s0 + b, V_COL // KV_WIDTH)),
        pl.BlockSpec((qb, COL_TILE), lambda b, qi: (q0 + b * nq + qi, GA_TILE0)),
        pl.BlockSpec((qb, COL_TILE), lambda b, qi: (q0 + b * nq + qi, GA_TILE0 + 1)),
    ]
    args = [proj, proj, proj, proj, proj]
    scratch = [pltpu.VMEM((n_seq, A_KV_HEADS, seq, 2 * HEAD_DIM), BF16)]
    if cache is not None:
        past = cache[0].shape[2]
        for c in cache:
            in_specs.append(pl.BlockSpec((None, None, past, A_KV_HEADS, HEAD_DIM),
                                         lambda b, qi: (b, layer, 0, 0, 0)))
            args.append(c)
        scratch += [pltpu.VMEM((A_KV_HEADS, past, HEAD_DIM), BF16),
                    pltpu.VMEM((A_KV_HEADS, past, 2 * HEAD_DIM), BF16)]
    out_shape = [jax.ShapeDtypeStruct((n_batch * seq, A_WIDTH), BF16)]
    out_specs = [pl.BlockSpec((qb, A_WIDTH), lambda b, qi: (b * nq + qi, 0))]
    return in_specs, args, out_shape, out_specs, scratch


def _mix_kernel(u_ref, vb_ref, gb_ref, z_ref, gc_ref, snw_ref, ws_ref, bs_ref, wp_ref, ps_ref, band_ref, inv_ref,
                bo_ref, co_ref, zb_scr, *, n_seq):
    r = u_ref.shape[0] // n_seq
    c = pl.program_id(1)
    nc = pl.num_programs(1)
    base = pl.multiple_of(c * r, r)
    n_chunks = r // CHUNK
    z_rows = z_ref.shape[0]
    for j in range(n_seq):
        row0 = j * r

        vbn = _rms(vb_ref[row0:row0 + r, :].astype(F32), snw_ref[...]).astype(BF16)
        for g in range(B_GROUPS):
            cs = slice(g * HEAD_DIM, (g + 1) * HEAD_DIM)
            wide = jnp.concatenate([vbn[cc * CHUNK:(cc + 1) * CHUNK, cs] for cc in range(n_chunks)], axis=1)
            mixed = jnp.dot(ws_ref[g], wide, preferred_element_type=F32) + bs_ref[g]
            for cc in range(n_chunks):
                rs = slice(row0 + cc * CHUNK, row0 + (cc + 1) * CHUNK)
                mx = mixed[:, cc * HEAD_DIM:(cc + 1) * HEAD_DIM]
                bo_ref[rs, cs] = (u_ref[rs, cs].astype(F32) * mx * gb_ref[rs, cs].astype(F32)).astype(BF16)

        zb = zb_scr.at[j]
        z0 = base + row0
        zb[POOL_HALO:POOL_HALO + r, :] = z_ref[pl.ds(pl.multiple_of(z0, POOL_HALO), r), :]
        above = z_ref[pl.ds(pl.multiple_of(jnp.maximum(z0 - POOL_HALO, 0), POOL_HALO), POOL_HALO), :]
        below = z_ref[pl.ds(pl.multiple_of(jnp.minimum(z0 + r, z_rows - POOL_HALO), POOL_HALO), POOL_HALO), :]
        zeros = jnp.zeros((POOL_HALO, C_WIDTH), BF16)
        zb[0:POOL_HALO, :] = jnp.where(c > 0, above, zeros)
        zb[POOL_HALO + r:, :] = jnp.where(c < nc - 1, below, zeros)

        for g in range(len(POOL_WINDOWS)):
            cs = slice(g * HEAD_DIM, (g + 1) * HEAD_DIM)
            ds = []
            for blk in range(n_chunks):
                lo = blk * CHUNK
                wsum = jnp.dot(band_ref[g], zb[lo:lo + CHUNK + 2 * POOL_HALO, cs], preferred_element_type=F32)
                zc = zb[POOL_HALO + lo:POOL_HALO + lo + CHUNK, cs].astype(F32)
                ds.append((wsum * inv_ref[lo:lo + CHUNK, cs] - zc).astype(BF16))
            dm = jnp.dot(jnp.concatenate(ds, axis=0), wp_ref[g], preferred_element_type=F32)
            rows = slice(row0, row0 + r)
            co_ref[rows, cs] = (dm * ps_ref[:, cs] * gc_ref[rows, cs].astype(F32)).astype(BF16)


def _pool_band():
    t = np.arange(CHUNK)[:, None] + POOL_HALO
    j = np.arange(CHUNK + 2 * POOL_HALO)[None, :]
    return jnp.asarray(np.stack([(j >= t - w // 2) & (j < t + w - w // 2) for w in POOL_WINDOWS]), dtype=BF16)


def _pool_inv_count(seq):
    t = np.arange(seq)
    cols = [1.0 / (np.clip(t + w - w // 2, 0, seq) - np.clip(t - w // 2, 0, seq)) for w in POOL_WINDOWS]
    return jnp.asarray(np.repeat(np.stack(cols, axis=1), HEAD_DIM, axis=1), dtype=F32)


def _mix_specs(proj, snw, ws, bs, wp, ps, *, row0, n_batch, seq, n_seq):
    r = min(SEQ_TILE, seq)
    nc = seq // r
    rb, sb = r * n_seq, seq * n_seq
    r0, s0 = row0 // rb, row0 // sb
    tile = lambda col: pl.BlockSpec((rb, COL_TILE), lambda b, c: (r0 + b * nc + c, col))
    const3 = lambda shape: pl.BlockSpec(shape, lambda b, c: (0, 0, 0))
    in_specs = [
        tile(U_TILE), tile(VB_TILE), tile(GB_TILE),
        pl.BlockSpec((sb, COL_TILE), lambda b, c: (s0 + b, Z_TILE)),
        tile(GC_TILE),
        pl.BlockSpec((1, B_WIDTH), lambda b, c: (0, 0)),
        const3((B_GROUPS, CHUNK, CHUNK)),
        const3((B_GROUPS, CHUNK, 1)),
        const3((B_GROUPS, HEAD_DIM, HEAD_DIM)),
        pl.BlockSpec((1, C_WIDTH), lambda b, c: (0, 0)),
        const3((len(POOL_WINDOWS), CHUNK, CHUNK + 2 * POOL_HALO)),
        pl.BlockSpec((r, C_WIDTH), lambda b, c: (c, 0)),
    ]
    args = [proj, proj, proj, proj, proj, snw, ws, bs, wp, ps, _pool_band(), _pool_inv_count(seq)]
    out_shape = [jax.ShapeDtypeStruct((n_batch * seq, B_WIDTH), BF16),
                 jax.ShapeDtypeStruct((n_batch * seq, C_WIDTH), BF16)]
    out_spec = pl.BlockSpec((rb, COL_TILE), lambda b, c: (b * nc + c, 0))
    scratch = [pltpu.VMEM((n_seq, r + 2 * POOL_HALO, C_WIDTH), BF16)]
    return in_specs, args, out_shape, [out_spec, out_spec], scratch


def _seq_kernel(*refs, ctx, n_seq, n_attn_in, n_mix_in, n_attn_scr, ada):
    attn_in, refs = refs[:n_attn_in], refs[n_attn_in:]
    mix_in, refs = refs[:n_mix_in], refs[n_mix_in:]
    if ada:
        ada_in, refs = refs[:3], refs[3:]
    o_ref, bo_ref, co_ref = refs[:3]
    refs = refs[3:]
    if ada:
        mo_ref, refs = refs[0], refs[1:]
    attn_scr, (zb_scr,) = refs[:n_attn_scr], refs[n_attn_scr:]
    _attn_kernel(*attn_in, o_ref, *attn_scr, ctx=ctx, n_seq=n_seq)
    _mix_kernel(*mix_in, bo_ref, co_ref, zb_scr, n_seq=n_seq)
    if ada:
        _ada_kernel(*ada_in, mo_ref)


def _seq_call(proj, cache, mix_w, ada_next, *, row0, n_batch, seq, layer):
    whole = seq <= SEQ_TILE
    n_seq = max(SHORT_SEQ_ROWS // seq, 1) if whole else 1
    assert n_batch % n_seq == 0 and (whole or seq % SEQ_TILE == 0)
    a_in, a_args, a_shape, a_out, a_scr = _attn_specs(proj, cache, row0=row0, n_batch=n_batch, seq=seq,
                                                      layer=layer, n_seq=n_seq)
    m_in, m_args, m_shape, m_out, m_scr = _mix_specs(proj, *mix_w, row0=row0, n_batch=n_batch, seq=seq, n_seq=n_seq)
    ctx = cache is not None
    grid = (n_batch // n_seq, max(seq // SEQ_TILE, 1))
    d_in, d_args, d_shape, d_out = [], [], [], []
    if ada_next is not None:
        n_blocks = GATE_COLS // ADA_STEP_COLS
        assert grid[0] * grid[1] >= n_blocks
        blk = lambda b, c: jnp.minimum(b * grid[1] + c, n_blocks - 1)
        d_in = [pl.BlockSpec((MOD_ROWS, D_MODEL), lambda b, c: (0, 0)),
                pl.BlockSpec((None, D_MODEL, ADA_STEP_COLS), lambda b, c: (layer + 1, 0, blk(b, c))),
                pl.BlockSpec((None, 1, ADA_STEP_COLS), lambda b, c: (layer + 1, 0, blk(b, c)))]
        d_args = list(ada_next)
        d_shape = [jax.ShapeDtypeStruct((MOD_ROWS, GATE_COLS), F32)]
        d_out = [pl.BlockSpec((MOD_ROWS, ADA_STEP_COLS), lambda b, c: (0, blk(b, c)))]
    return pl.pallas_call(
        functools.partial(_seq_kernel, ctx=ctx, n_seq=n_seq, n_attn_in=len(a_in), n_mix_in=len(m_in),
                          n_attn_scr=len(a_scr), ada=bool(d_in)),
        out_shape=a_shape + m_shape + d_shape,
        grid=grid,
        in_specs=a_in + m_in + d_in,
        out_specs=a_out + m_out + d_out,
        scratch_shapes=a_scr + m_scr,
        compiler_params=_params("arbitrary", "arbitrary"),
        name="attn_mix_lat" if ctx else "attn_mix_ctx",
    )(*a_args, *m_args, *d_args)


def _outproj_kernel(*refs, first, final, ctx_tiles, layer):
    (ac_ref, al_ref, bc_ref, bl_ref, cc_ref, cl_ref, g0_ref, g1_ref, g2_ref) = refs[:9]
    refs = refs[9:]
    if first:
        xc_ref, xl_ref = refs[:2]
        refs = refs[2:]
    else:
        x_ref = refs[0]
        refs = refs[1:]
    mod_ref, wa_hbm, wb_hbm, wc_hbm, wo_hbm, nw_ref = refs[:6]
    refs = refs[6:]
    if final:
        yc_ref, yl_ref = refs[:2]
        refs = refs[2:]
    else:
        modn_ref, y_ref, hn_ref = refs[:3]
        refs = refs[3:]
    a_scr, b_scr, c_scr, m_scr, wa_ref, wb_ref, wc_ref, wo_ref, stage, sem = refs[:10]
    y_scr = refs[10] if final else y_ref

    @pl.when(pl.program_id(0) == 0)
    def _():
        for w_hbm, w_ref in ((wa_hbm, wa_ref), (wb_hbm, wb_ref), (wc_hbm, wc_ref), (wo_hbm, wo_ref)):
            _load_weight_bf16(w_hbm, layer, w_ref, stage, sem)

    is_ctx = pl.program_id(0) < ctx_tiles
    a_scr[...] = _pick(is_ctx, ac_ref, al_ref)
    b_scr[...] = _pick(is_ctx, bc_ref, bl_ref)
    c_scr[...] = _pick(is_ctx, cc_ref, cl_ref)
    for n in range(D_MODEL // COL_TILE):
        cs = slice(n * COL_TILE, (n + 1) * COL_TILE)
        a = jnp.dot(a_scr[...], wa_ref[:, cs], preferred_element_type=F32)
        b = jnp.dot(b_scr[...], wb_ref[:, cs], preferred_element_type=F32)
        c = jnp.dot(c_scr[...], wc_ref[:, cs], preferred_element_type=F32)
        m = (g0_ref[:, cs].astype(F32) * a + g1_ref[:, cs].astype(F32) * b
             + g2_ref[:, cs].astype(F32) * c)
        m_scr[:, cs] = m.astype(BF16)
    ssq = jnp.zeros((m_scr.shape[0], 1), F32)
    for n in range(D_MODEL // COL_TILE):
        cs = slice(n * COL_TILE, (n + 1) * COL_TILE)
        out = jnp.dot(m_scr[...], wo_ref[:, cs], preferred_element_type=F32)
        x = jnp.where(is_ctx, xc_ref[:, cs], xl_ref[:, cs]) if first else x_ref[:, cs]
        y = x + mod_ref[:, 2 * D_MODEL + n * COL_TILE:2 * D_MODEL + (n + 1) * COL_TILE] * out
        y_scr[:, cs] = y
        ssq = ssq + jnp.sum(y * y, axis=-1, keepdims=True)
    yn = y_scr[...] * lax.rsqrt(ssq * (1.0 / D_MODEL) + EPS) * nw_ref[...]
    if final:
        @pl.when(is_ctx)
        def _():
            yc_ref[...] = yn

        @pl.when(jnp.logical_not(is_ctx))
        def _():
            yl_ref[...] = yn
    else:
        hn_ref[...] = (yn * (1.0 + modn_ref[:, D_MODEL:2 * D_MODEL]) + modn_ref[:, 0:D_MODEL]).astype(BF16)


def _outproj_call(attn, bout, cout, gates, x, mod3, wa, wb, wc, wo, nw, modn3, *, st, layer):
    tm = st.tm
    first = isinstance(x, tuple)
    final = modn3 is None
    gate = lambda g: pl.BlockSpec((tm, D_MODEL), lambda i: (i, g))
    row = pl.BlockSpec((tm, D_MODEL), lambda i: (i, 0))
    in_specs = (st.two_source_specs(A_WIDTH) + st.two_source_specs(B_WIDTH) + st.two_source_specs(C_WIDTH)
                + [gate(0), gate(1), gate(2)]
                + (st.two_source_specs(D_MODEL) if first else [row])
                + [pl.BlockSpec((None, 1, GATE_COLS), st.mod_index),
                   HBM, HBM, HBM, HBM,
                   pl.BlockSpec((1, D_MODEL), lambda i: (0, 0))])
    args = [*attn, *bout, *cout, gates, gates, gates, *(x if first else (x,)), mod3, wa, wb, wc, wo, nw]
    scratch = [pltpu.VMEM((tm, A_WIDTH), BF16), pltpu.VMEM((tm, B_WIDTH), BF16),
               pltpu.VMEM((tm, C_WIDTH), BF16), pltpu.VMEM((tm, D_MODEL), BF16),
               pltpu.VMEM((A_WIDTH, D_MODEL), BF16), pltpu.VMEM((B_WIDTH, D_MODEL), BF16),
               pltpu.VMEM((C_WIDTH, D_MODEL), BF16), pltpu.VMEM((D_MODEL, D_MODEL), BF16),
               pltpu.VMEM((2, OUT_CAST_ROWS, D_MODEL), F32), pltpu.SemaphoreType.DMA((2,))]
    if final:
        out_shape = [jax.ShapeDtypeStruct((st.ctx_tiles * tm, D_MODEL), F32),
                     jax.ShapeDtypeStruct(((st.tiles - st.ctx_tiles) * tm, D_MODEL), F32)]
        out_specs = st.two_source_specs(D_MODEL)
        scratch.append(pltpu.VMEM((tm, D_MODEL), F32))
    else:
        in_specs.append(pl.BlockSpec((None, 1, GATE_COLS), st.mod_index))
        args.append(modn3)
        out_shape = [jax.ShapeDtypeStruct((st.tiles * tm, D_MODEL), F32),
                     jax.ShapeDtypeStruct((st.tiles * tm, D_MODEL), BF16)]
        out_specs = [row, row]
    return pl.pallas_call(
        functools.partial(_outproj_kernel, first=first, final=final, ctx_tiles=st.ctx_tiles, layer=layer),
        out_shape=out_shape,
        grid=(st.tiles,),
        in_specs=in_specs,
        out_specs=out_specs,
        scratch_shapes=scratch,
        compiler_params=_params("arbitrary"),
        name="out_proj_final" if final else "out_proj",
    )(*args)


def _rope_tables(n_identity, n_tokens):
    rows = n_tokens // GRID_W
    row = np.repeat(np.arange(rows), GRID_W).astype(np.float64)
    col = np.tile(np.arange(GRID_W), rows).astype(np.float64)
    n_freq = HEAD_DIM // 4
    inv = ROPE_THETA ** (-np.arange(n_freq, dtype=np.float64) / n_freq)
    ar = row[:, None] * inv[None, :]
    ac = col[:, None] * inv[None, :]
    zero = np.zeros_like(ar)
    cos = np.concatenate([np.cos(ar), np.cos(ar), np.cos(ac), np.cos(ac)], axis=-1)
    sa = np.concatenate([-np.sin(ar), zero, -np.sin(ac), zero], axis=-1)
    sb = np.concatenate([zero, np.sin(ar), zero, np.sin(ac)], axis=-1)
    ident = (np.ones((n_identity, HEAD_DIM)), np.zeros((n_identity, HEAD_DIM)), np.zeros((n_identity, HEAD_DIM)))
    return tuple(jnp.asarray(np.concatenate([i, t], axis=0), dtype=F32) for i, t in zip(ident, (cos, sa, sb)))


def kernel(x_prompt, x_sample, cache_k, cache_v, c, c_ctx, norm_w, w_ada, b_ada, w_in, q_norm_w,
           k_norm_w, sgu_norm_w, w_sgu, b_sgu, w_pool, pool_scale, w_br_a, w_br_b, w_br_c, w_merge,
           b_merge, w_out, final_norm_w):
    nb_p, seq_p, d = x_prompt.shape
    nb_s, seq_s, _ = x_sample.shape
    n_ctx, n_lat = nb_p * seq_p, nb_s * seq_s
    assert d == D_MODEL and nb_s + 1 <= MOD_ROWS

    cv = jnp.concatenate([c_ctx[None, :], c, jnp.zeros((MOD_ROWS - 1 - nb_s, d), F32)], axis=0)
    b_ada3 = b_ada.reshape(DEPTH, 1, GATE_COLS)
    mod3 = [_ada_call(cv, w_ada, b_ada3).reshape(MOD_ROWS, 1, GATE_COLS)]
    st_in = _Stream(n_ctx, n_lat, seq_s, IN_TOKEN_TILE)
    st_out = _Stream(n_ctx, n_lat, seq_s, OUT_TOKEN_TILE)
    rope_tabs = _rope_tables(IN_TOKEN_TILE, seq_s)
    cache = (cache_k, cache_v)

    x = (x_prompt.reshape(n_ctx, d), x_sample.reshape(n_lat, d))
    h = _normmod_call(*x, mod3[0], norm_w[0].reshape(1, d), seq=seq_s)
    states = []
    w_mg, w_i, wa, wb, wc, wo = w_merge, w_in, w_br_a, w_br_b, w_br_c, w_out
    for l in range(DEPTH):
        last = l == DEPTH - 1
        bm = b_merge[l].reshape(1, GATE_COLS)
        qnw = q_norm_w[l].reshape(1, HEAD_DIM)
        knw = k_norm_w[l].reshape(1, HEAD_DIM)
        mix_w = (sgu_norm_w[l].reshape(1, B_WIDTH), w_sgu[l].astype(BF16),
                 b_sgu[l].reshape(B_GROUPS, CHUNK, 1), w_pool[l].astype(BF16),
                 pool_scale[l].reshape(1, C_WIDTH))
        nw_next = (final_norm_w if last else norm_w[l + 1]).reshape(1, d)

        gates = _gates_call(h, w_mg, bm, layer=l)
        proj, ks, vs = _proj_call(h, w_i, qnw, knw, rope_tabs, st=st_in, layer=l)
        states.append((ks, vs))
        seq_c = _seq_call(proj, None, mix_w, None, row0=0, n_batch=nb_p, seq=seq_p, layer=l)
        seq_l = _seq_call(proj, cache, mix_w, None if last else (cv, w_ada, b_ada3),
                          row0=n_ctx, n_batch=nb_s, seq=seq_s, layer=l)
        if not last:
            *seq_l, mod_next = seq_l
            mod3.append(mod_next.reshape(MOD_ROWS, 1, GATE_COLS))
        attn, bout, cout = zip(seq_c, seq_l)
        res = _outproj_call(attn, bout, cout, gates, x, mod3[l],
                            wa, wb, wc, wo, nw_next, None if last else mod3[l + 1], st=st_out, layer=l)
        if last:
            y_ctx, y_lat = res
        else:
            x, h = res

    state_k, state_v = (
        jnp.stack([s[i][:n_ctx].reshape(nb_p, seq_p, A_KV_HEADS, HEAD_DIM) for s in states], axis=1)
        for i in range(2))
    return (y_ctx.reshape(nb_p, seq_p, d), y_lat.reshape(nb_s, seq_s, d), state_k, state_v)
```

```python
import functools

import jax
import jax.numpy as jnp
import numpy as np
from jax import lax
from jax.experimental import pallas as pl
from jax.experimental.pallas import tpu as pltpu

F32 = jnp.float32
BF16 = jnp.bfloat16

D_MODEL = 2048
DEPTH = 2
GRID_W = 64
EPS = 1e-6
HEAD_DIM = 128
A_HEADS = 8
A_KV_HEADS = 2
A_WIDTH = A_HEADS * HEAD_DIM
KV_WIDTH = A_KV_HEADS * HEAD_DIM
ROPE_THETA = 10000.0
ROPE_PARTNER = HEAD_DIM // 4
ATTN_SCALE = HEAD_DIM ** -0.5
LOG2_E = 1.4426950408889634
CHUNK = 128
B_GROUPS = 4
B_WIDTH = 512
C_WIDTH = 512
POOL_WINDOWS = (2, 4, 8, 16)
POOL_HALO = 64
N_BRANCH = 3
GATE_COLS = N_BRANCH * D_MODEL
IN_COLS = 2 * A_WIDTH + 2 * KV_WIDTH + 3 * B_WIDTH + 2 * C_WIDTH

COL_TILE = 512
Q_TILE0 = 0
KV_TILE = Q_TILE0 + A_WIDTH // COL_TILE
GA_TILE0 = KV_TILE + 1
U_TILE = GA_TILE0 + A_WIDTH // COL_TILE
VB_TILE = U_TILE + 1
GB_TILE = VB_TILE + 1
Z_TILE = GB_TILE + 1
GC_TILE = Z_TILE + 1
HEADS_PER_TILE = COL_TILE // HEAD_DIM
K_COL = KV_TILE * COL_TILE
V_COL = K_COL + KV_WIDTH

SUB_COLS = 256
MOD_ROWS = 8
ADA_TILE = 1024
ADA_STEP_COLS = 384
IN_TOKEN_TILE = 512
NORM_TOKEN_TILE = 1024
NORM_RING = 3
NORM_ROWS = 16
OUT_TOKEN_TILE = 256
SEQ_TILE = 512
SHORT_SEQ_ROWS = 1024
CAST_ROWS = 128
OUT_CAST_ROWS = 256
VMEM_LIMIT = 56 * 1024 * 1024


def _params(*sem):
    return pltpu.CompilerParams(dimension_semantics=sem, vmem_limit_bytes=VMEM_LIMIT)


def _resident(shape):
    return pl.BlockSpec(shape, lambda *_: (0,) * len(shape), pipeline_mode=pl.Buffered(1))


def _silu(x):
    return x * jax.nn.sigmoid(x)


def _rms(x, w):
    ms = jnp.mean(x * x, axis=-1, keepdims=True)
    return x * lax.rsqrt(ms + EPS) * w


class _Stream:
    def __init__(self, n_ctx, n_lat, seq, tm):
        assert n_ctx % tm == 0 and seq % tm == 0
        self.tm = tm
        self.ctx_tiles = n_ctx // tm
        self.tiles = (n_ctx + n_lat) // tm
        self.tiles_per_seq = seq // tm

    def mod_index(self, i):
        lat = 1 + (i - self.ctx_tiles) // self.tiles_per_seq
        return (jnp.where(i < self.ctx_tiles, 0, lat), 0, 0)

    def two_source_specs(self, width):
        ctx = pl.BlockSpec((self.tm, width), lambda i: (jnp.minimum(i, self.ctx_tiles - 1), 0))
        lat = pl.BlockSpec((self.tm, width), lambda i: (jnp.maximum(i - self.ctx_tiles, 0), 0))
        return [ctx, lat]


def _pick(is_ctx, ctx_ref, lat_ref):
    return jnp.where(is_ctx, ctx_ref[...], lat_ref[...])


def _load_weight_bf16(w_hbm, layer, w_scr, stage, sem):
    chunk = stage.shape[1]
    n = w_scr.shape[0] // chunk

    def copy(c):
        return pltpu.make_async_copy(w_hbm.at[layer, pl.ds(c * chunk, chunk), :], stage.at[c % 2], sem.at[c % 2])

    copy(0).start()
    for c in range(n):
        if c + 1 < n:
            copy(c + 1).start()
        copy(c).wait()
        w_scr[c * chunk:(c + 1) * chunk, :] = stage[c % 2].astype(BF16)


def _weight_scratch(rows, cols, chunk):
    return [pltpu.VMEM((rows, cols), BF16), pltpu.VMEM((2, chunk, cols), F32), pltpu.SemaphoreType.DMA((2,))]


HBM = pl.BlockSpec(memory_space=pl.ANY)


def _ada_kernel(cv_ref, w_ref, b_ref, o_ref):
    a = _silu(cv_ref[...]).astype(BF16)
    o_ref[...] = jnp.dot(a, w_ref[...].astype(BF16), preferred_element_type=F32) + b_ref[...]


def _ada_call(cv, w_ada, b_ada):
    return pl.pallas_call(
        _ada_kernel,
        out_shape=jax.ShapeDtypeStruct((MOD_ROWS, GATE_COLS), F32),
        grid=(GATE_COLS // ADA_TILE,),
        in_specs=[
            pl.BlockSpec((MOD_ROWS, D_MODEL), lambda j: (0, 0)),
            pl.BlockSpec((None, D_MODEL, ADA_TILE), lambda j: (0, 0, j)),
            pl.BlockSpec((None, 1, ADA_TILE), lambda j: (0, 0, j)),
        ],
        out_specs=pl.BlockSpec((MOD_ROWS, ADA_TILE), lambda j: (0, j)),
        compiler_params=_params("arbitrary"),
        name="ada_mod",
    )(cv, w_ada, b_ada)


def _normmod_kernel(xc_hbm, xl_hbm, mod_ref, nw_ref, h_ref, xbuf, sem, *, ctx_tiles, n_tiles):
    tm = h_ref.shape[0]
    i = pl.program_id(0)

    def tile_copy(t, wait):
        slot = t % NORM_RING
        for is_ctx, src, row in ((True, xc_hbm, t * tm), (False, xl_hbm, (t - ctx_tiles) * tm)):
            @pl.when((t < ctx_tiles) == is_ctx)
            def _():
                cp = pltpu.make_async_copy(src.at[pl.ds(pl.multiple_of(row, tm), tm), :], xbuf.at[slot], sem.at[slot])
                cp.wait() if wait else cp.start()

    @pl.when(i == 0)
    def _():
        for t in range(NORM_RING - 1):
            tile_copy(jnp.int32(t), wait=False)

    @pl.when(i + NORM_RING - 1 < n_tiles)
    def _():
        tile_copy(i + NORM_RING - 1, wait=False)

    tile_copy(i, wait=True)
    x_ref = xbuf.at[i % NORM_RING]

    def chunk(r, carry):
        rows = pl.ds(pl.multiple_of(r * NORM_ROWS, NORM_ROWS), NORM_ROWS)
        y = _rms(x_ref[rows, :], nw_ref[...])
        h_ref[rows, :] = (y * (1.0 + mod_ref[:, D_MODEL:2 * D_MODEL]) + mod_ref[:, 0:D_MODEL]).astype(BF16)
        return carry

    lax.fori_loop(0, tm // NORM_ROWS, chunk, 0, unroll=8)


def _normmod_call(x_ctx, x_lat, mod3, norm_w, *, seq):
    st = _Stream(x_ctx.shape[0], x_lat.shape[0], seq, NORM_TOKEN_TILE)
    assert st.ctx_tiles >= NORM_RING - 1 and st.tiles >= NORM_RING
    return pl.pallas_call(
        functools.partial(_normmod_kernel, ctx_tiles=st.ctx_tiles, n_tiles=st.tiles),
        out_shape=jax.ShapeDtypeStruct((st.tiles * st.tm, D_MODEL), BF16),
        grid=(st.tiles,),
        in_specs=[HBM, HBM,
                  pl.BlockSpec((None, 1, GATE_COLS), st.mod_index),
                  pl.BlockSpec((1, D_MODEL), lambda i: (0, 0))],
        out_specs=pl.BlockSpec((st.tm, D_MODEL), lambda i: (i, 0)),
        scratch_shapes=[pltpu.VMEM((NORM_RING, st.tm, D_MODEL), F32), pltpu.SemaphoreType.DMA((NORM_RING,))],
        compiler_params=_params("arbitrary"),
        name="norm_mod",
    )(x_ctx, x_lat, mod3, norm_w)


def _gates_kernel(h_ref, w_hbm, b_ref, o_ref, w_ref, stage, sem, *, layer):
    @pl.when(pl.program_id(0) == 0)
    def _():
        _load_weight_bf16(w_hbm, layer, w_ref, stage, sem)

    for s in range(GATE_COLS // SUB_COLS):
        cs = slice(s * SUB_COLS, (s + 1) * SUB_COLS)
        acc = jnp.dot(h_ref[...], w_ref[:, cs], preferred_element_type=F32)
        o_ref[:, cs] = jax.nn.sigmoid(acc + b_ref[:, cs]).astype(BF16)


def _gates_call(h, w_merge, b_merge, *, layer):
    t = h.shape[0]
    tm = IN_TOKEN_TILE
    return pl.pallas_call(
        functools.partial(_gates_kernel, layer=layer),
        out_shape=jax.ShapeDtypeStruct((t, GATE_COLS), BF16),
        grid=(t // tm,),
        in_specs=[
            pl.BlockSpec((tm, D_MODEL), lambda i: (i, 0)),
            HBM,
            _resident((1, GATE_COLS)),
        ],
        out_specs=pl.BlockSpec((tm, GATE_COLS), lambda i: (i, 0)),
        scratch_shapes=_weight_scratch(D_MODEL, GATE_COLS, CAST_ROWS),
        compiler_params=_params("arbitrary"),
        name="gates",
    )(h, w_merge, b_merge)


def _rope(y, cos, sa, sb):
    return (y * cos + pltpu.roll(y, HEAD_DIM - ROPE_PARTNER, 1) * sa + pltpu.roll(y, ROPE_PARTNER, 1) * sb)


def _proj_kernel(h_ref, w_hbm, qnw_ref, knw_ref, cos_ref, sa_ref, sb_ref, proj_ref, ks_ref, vs_ref,
                 w_ref, stage, sem, *, layer):
    @pl.when(pl.program_id(0) == 0)
    def _():
        _load_weight_bf16(w_hbm, layer, w_ref, stage, sem)

    def head(xh, w):
        return _rope(_rms(xh, w), cos_ref[...], sa_ref[...], sb_ref[...])

    qw = qnw_ref[...] * (ATTN_SCALE * LOG2_E)
    silu_cols = ((GA_TILE0 * COL_TILE, U_TILE * COL_TILE), (GB_TILE * COL_TILE, Z_TILE * COL_TILE),
                 (GC_TILE * COL_TILE, IN_COLS))
    for s in range(IN_COLS // SUB_COLS):
        acc = jnp.dot(h_ref[...], w_ref[:, s * SUB_COLS:(s + 1) * SUB_COLS], preferred_element_type=F32)
        for hb in range(SUB_COLS // HEAD_DIM):
            c0 = s * SUB_COLS + hb * HEAD_DIM
            cs = slice(c0, c0 + HEAD_DIM)
            a = acc[:, hb * HEAD_DIM:(hb + 1) * HEAD_DIM]
            if c0 < K_COL:
                proj_ref[:, cs] = head(a, qw).astype(BF16)
            elif c0 < V_COL:
                ks_ref[:, c0 - K_COL:c0 - K_COL + HEAD_DIM] = _rms(a, knw_ref[...])
                proj_ref[:, cs] = head(a, knw_ref[...]).astype(BF16)
            elif c0 < V_COL + KV_WIDTH:
                vs_ref[:, c0 - V_COL:c0 - V_COL + HEAD_DIM] = a
                proj_ref[:, cs] = a.astype(BF16)
            elif any(lo <= c0 < hi for lo, hi in silu_cols):
                proj_ref[:, cs] = _silu(a).astype(BF16)
            else:
                proj_ref[:, cs] = a.astype(BF16)


def _proj_call(h, w_in, qnw, knw, rope_tabs, *, st, layer):
    t = h.shape[0]
    tm = st.tm

    def tab_index(i):
        return (jnp.where(i < st.ctx_tiles, 0, 1 + (i - st.ctx_tiles) % st.tiles_per_seq), 0)

    in_specs = [
        pl.BlockSpec((tm, D_MODEL), lambda i: (i, 0)),
        HBM,
        pl.BlockSpec((1, HEAD_DIM), lambda i: (0, 0)),
        pl.BlockSpec((1, HEAD_DIM), lambda i: (0, 0)),
    ] + [pl.BlockSpec((tm, HEAD_DIM), tab_index) for _ in rope_tabs]
    out_shape = [jax.ShapeDtypeStruct((t, IN_COLS), BF16),
                 jax.ShapeDtypeStruct((t, KV_WIDTH), F32), jax.ShapeDtypeStruct((t, KV_WIDTH), F32)]
    out_specs = [pl.BlockSpec((tm, IN_COLS), lambda i: (i, 0)),
                 pl.BlockSpec((tm, KV_WIDTH), lambda i: (i, 0)), pl.BlockSpec((tm, KV_WIDTH), lambda i: (i, 0))]
    return pl.pallas_call(
        functools.partial(_proj_kernel, layer=layer),
        out_shape=out_shape,
        grid=(t // tm,),
        in_specs=in_specs,
        out_specs=out_specs,
        scratch_shapes=_weight_scratch(D_MODEL, IN_COLS, CAST_ROWS),
        compiler_params=_params("arbitrary"),
        name="proj",
    )(h, w_in, qnw, knw, *rope_tabs)


def _attn_kernel(*refs, ctx, n_seq):
    q_ref, k_ref, v_ref, ga0_ref, ga1_ref = refs[:5]
    ga_refs = (ga0_ref, ga1_ref)
    refs = refs[5:]
    if ctx:
        ck_ref, cv_ref = refs[:2]
        refs = refs[2:]
    o_ref, vx_scr = refs[:2]
    if ctx:
        ckx_scr, cvx_scr = refs[2:]
    seq = k_ref.shape[0] // n_seq
    tq = q_ref.shape[0] // n_seq

    @pl.when(pl.program_id(1) == 0)
    def _():
        for j in range(n_seq):
            for kh in range(A_KV_HEADS):
                hs = slice(kh * HEAD_DIM, (kh + 1) * HEAD_DIM)
                vx_scr[j, kh, :, :HEAD_DIM] = v_ref[j * seq:(j + 1) * seq, hs]
                vx_scr[j, kh, :, HEAD_DIM:] = jnp.ones((seq, HEAD_DIM), BF16)
        if ctx:
            for kh in range(A_KV_HEADS):
                ckx_scr[kh] = ck_ref[:, kh, :].astype(BF16)
                cvx_scr[kh, :, :HEAD_DIM] = cv_ref[:, kh, :].astype(BF16)
                cvx_scr[kh, :, HEAD_DIM:] = jnp.ones((cvx_scr.shape[1], HEAD_DIM), BF16)

    nt = (((1,), (1,)), ((), ()))
    for j in range(n_seq):
        qrows = slice(j * tq, (j + 1) * tq)
        for head in range(A_HEADS):
            kh, hh = divmod(head, HEADS_PER_TILE)
            cs = slice(head * HEAD_DIM, (head + 1) * HEAD_DIM)
            q = q_ref[qrows, cs]
            k = k_ref[j * seq:(j + 1) * seq, kh * HEAD_DIM:(kh + 1) * HEAD_DIM]
            s1 = lax.dot_general(q, k, nt, preferred_element_type=F32)
            m = jnp.max(s1, axis=-1, keepdims=True)
            if ctx:
                s2 = lax.dot_general(q, ckx_scr[kh], nt, preferred_element_type=F32)
                m = jnp.maximum(m, jnp.max(s2, axis=-1, keepdims=True))
            ox = jnp.dot(jnp.exp2(s1 - m).astype(BF16), vx_scr[j, kh], preferred_element_type=F32)
            if ctx:
                ox = ox + jnp.dot(jnp.exp2(s2 - m).astype(BF16), cvx_scr[kh], preferred_element_type=F32)
            o = ox[:, :HEAD_DIM] / ox[:, HEAD_DIM:]
            ga = ga_refs[kh][qrows, hh * HEAD_DIM:(hh + 1) * HEAD_DIM]
            o_ref[qrows, cs] = (o * ga.astype(F32)).astype(BF16)


def _attn_specs(proj, cache, *, row0, n_batch, seq, layer, n_seq):
    tq = min(SEQ_TILE, seq)
    nq = seq // tq
    qb, sb = tq * n_seq, seq * n_seq
    q0, s0 = row0 // qb, row0 // sb
    in_specs = [
        pl.BlockSpec((qb, A_WIDTH), lambda b, qi: (q0 + b * nq + qi, Q_TILE0 * COL_TILE // A_WIDTH)),
        pl.BlockSpec((sb, KV_WIDTH), lambda b, qi: (s0 + b, K_COL // KV_WIDTH)),
        pl.BlockSpec((sb, KV_WIDTH), lambda b, qi: (s0 + b, V_COL // KV_WIDTH)),
        pl.BlockSpec((qb, COL_TILE), lambda b, qi: (q0 + b * nq + qi, GA_TILE0)),
        pl.BlockSpec((qb, COL_TILE), lambda b, qi: (q0 + b * nq + qi, GA_TILE0 + 1)),
    ]
    args = [proj, proj, proj, proj, proj]
    scratch = [pltpu.VMEM((n_seq, A_KV_HEADS, seq, 2 * HEAD_DIM), BF16)]
    if cache is not None:
        past = cache[0].shape[2]
        for c in cache:
            in_specs.append(pl.BlockSpec((None, None, past, A_KV_HEADS, HEAD_DIM),
                                         lambda b, qi: (b, layer, 0, 0, 0)))
            args.append(c)
        scratch += [pltpu.VMEM((A_KV_HEADS, past, HEAD_DIM), BF16),
                    pltpu.VMEM((A_KV_HEADS, past, 2 * HEAD_DIM), BF16)]
    out_shape = [jax.ShapeDtypeStruct((n_batch * seq, A_WIDTH), BF16)]
    out_specs = [pl.BlockSpec((qb, A_WIDTH), lambda b, qi: (b * nq + qi, 0))]
    return in_specs, args, out_shape, out_specs, scratch


def _mix_kernel(u_ref, vb_ref, gb_ref, z_ref, gc_ref, snw_ref, ws_ref, bs_ref, wp_ref, ps_ref, band_ref, inv_ref,
                bo_ref, co_ref, zb_scr, *, n_seq):
    r = u_ref.shape[0] // n_seq
    c = pl.program_id(1)
    nc = pl.num_programs(1)
    base = pl.multiple_of(c * r, r)
    n_chunks = r // CHUNK
    z_rows = z_ref.shape[0]
    for j in range(n_seq):
        row0 = j * r

        vbn = _rms(vb_ref[row0:row0 + r, :].astype(F32), snw_ref[...]).astype(BF16)
        for g in range(B_GROUPS):
            cs = slice(g * HEAD_DIM, (g + 1) * HEAD_DIM)
            wide = jnp.concatenate([vbn[cc * CHUNK:(cc + 1) * CHUNK, cs] for cc in range(n_chunks)], axis=1)
            mixed = jnp.dot(ws_ref[g], wide, preferred_element_type=F32) + bs_ref[g]
            for cc in range(n_chunks):
                rs = slice(row0 + cc * CHUNK, row0 + (cc + 1) * CHUNK)
                mx = mixed[:, cc * HEAD_DIM:(cc + 1) * HEAD_DIM]
                bo_ref[rs, cs] = (u_ref[rs, cs].astype(F32) * mx * gb_ref[rs, cs].astype(F32)).astype(BF16)

        zb = zb_scr.at[j]
        z0 = base + row0
        zb[POOL_HALO:POOL_HALO + r, :] = z_ref[pl.ds(pl.multiple_of(z0, POOL_HALO), r), :]
        above = z_ref[pl.ds(pl.multiple_of(jnp.maximum(z0 - POOL_HALO, 0), POOL_HALO), POOL_HALO), :]
        below = z_ref[pl.ds(pl.multiple_of(jnp.minimum(z0 + r, z_rows - POOL_HALO), POOL_HALO), POOL_HALO), :]
        zeros = jnp.zeros((POOL_HALO, C_WIDTH), BF16)
        zb[0:POOL_HALO, :] = jnp.where(c > 0, above, zeros)
        zb[POOL_HALO + r:, :] = jnp.where(c < nc - 1, below, zeros)

        for g in range(len(POOL_WINDOWS)):
            cs = slice(g * HEAD_DIM, (g + 1) * HEAD_DIM)
            ds = []
            for blk in range(n_chunks):
                lo = blk * CHUNK
                wsum = jnp.dot(band_ref[g], zb[lo:lo + CHUNK + 2 * POOL_HALO, cs], preferred_element_type=F32)
                zc = zb[POOL_HALO + lo:POOL_HALO + lo + CHUNK, cs].astype(F32)
                ds.append((wsum * inv_ref[lo:lo + CHUNK, cs] - zc).astype(BF16))
            dm = jnp.dot(jnp.concatenate(ds, axis=0), wp_ref[g], preferred_element_type=F32)
            rows = slice(row0, row0 + r)
            co_ref[rows, cs] = (dm * ps_ref[:, cs] * gc_ref[rows, cs].astype(F32)).astype(BF16)


def _pool_band():
    t = np.arange(CHUNK)[:, None] + POOL_HALO
    j = np.arange(CHUNK + 2 * POOL_HALO)[None, :]
    return jnp.asarray(np.stack([(j >= t - w // 2) & (j < t + w - w // 2) for w in POOL_WINDOWS]), dtype=BF16)


def _pool_inv_count(seq):
    t = np.arange(seq)
    cols = [1.0 / (np.clip(t + w - w // 2, 0, seq) - np.clip(t - w // 2, 0, seq)) for w in POOL_WINDOWS]
    return jnp.asarray(np.repeat(np.stack(cols, axis=1), HEAD_DIM, axis=1), dtype=F32)


def _mix_specs(proj, snw, ws, bs, wp, ps, *, row0, n_batch, seq, n_seq):
    r = min(SEQ_TILE, seq)
    nc = seq // r
    rb, sb = r * n_seq, seq * n_seq
    r0, s0 = row0 // rb, row0 // sb
    tile = lambda col: pl.BlockSpec((rb, COL_TILE), lambda b, c: (r0 + b * nc + c, col))
    const3 = lambda shape: pl.BlockSpec(shape, lambda b, c: (0, 0, 0))
    in_specs = [
        tile(U_TILE), tile(VB_TILE), tile(GB_TILE),
        pl.BlockSpec((sb, COL_TILE), lambda b, c: (s0 + b, Z_TILE)),
        tile(GC_TILE),
        pl.BlockSpec((1, B_WIDTH), lambda b, c: (0, 0)),
        const3((B_GROUPS, CHUNK, CHUNK)),
        const3((B_GROUPS, CHUNK, 1)),
        const3((B_GROUPS, HEAD_DIM, HEAD_DIM)),
        pl.BlockSpec((1, C_WIDTH), lambda b, c: (0, 0)),
        const3((len(POOL_WINDOWS), CHUNK, CHUNK + 2 * POOL_HALO)),
        pl.BlockSpec((r, C_WIDTH), lambda b, c: (c, 0)),
    ]
    args = [proj, proj, proj, proj, proj, snw, ws, bs, wp, ps, _pool_band(), _pool_inv_count(seq)]
    out_shape = [jax.ShapeDtypeStruct((n_batch * seq, B_WIDTH), BF16),
                 jax.ShapeDtypeStruct((n_batch * seq, C_WIDTH), BF16)]
    out_spec = pl.BlockSpec((rb, COL_TILE), lambda b, c: (b * nc + c, 0))
    scratch = [pltpu.VMEM((n_seq, r + 2 * POOL_HALO, C_WIDTH), BF16)]
    return in_specs, args, out_shape, [out_spec, out_spec], scratch


def _seq_kernel(*refs, ctx, n_seq, n_attn_in, n_mix_in, n_attn_scr, ada):
    attn_in, refs = refs[:n_attn_in], refs[n_attn_in:]
    mix_in, refs = refs[:n_mix_in], refs[n_mix_in:]
    if ada:
        ada_in, refs = refs[:3], refs[3:]
    o_ref, bo_ref, co_ref = refs[:3]
    refs = refs[3:]
    if ada:
        mo_ref, refs = refs[0], refs[1:]
    attn_scr, (zb_scr,) = refs[:n_attn_scr], refs[n_attn_scr:]
    _attn_kernel(*attn_in, o_ref, *attn_scr, ctx=ctx, n_seq=n_seq)
    _mix_kernel(*mix_in, bo_ref, co_ref, zb_scr, n_seq=n_seq)
    if ada:
        _ada_kernel(*ada_in, mo_ref)


def _seq_call(proj, cache, mix_w, ada_next, *, row0, n_batch, seq, layer):
    whole = seq <= SEQ_TILE
    n_seq = max(SHORT_SEQ_ROWS // seq, 1) if whole else 1
    assert n_batch % n_seq == 0 and (whole or seq % SEQ_TILE == 0)
    a_in, a_args, a_shape, a_out, a_scr = _attn_specs(proj, cache, row0=row0, n_batch=n_batch, seq=seq,
                                                      layer=layer, n_seq=n_seq)
    m_in, m_args, m_shape, m_out, m_scr = _mix_specs(proj, *mix_w, row0=row0, n_batch=n_batch, seq=seq, n_seq=n_seq)
    ctx = cache is not None
    grid = (n_batch // n_seq, max(seq // SEQ_TILE, 1))
    d_in, d_args, d_shape, d_out = [], [], [], []
    if ada_next is not None:
        n_blocks = GATE_COLS // ADA_STEP_COLS
        assert grid[0] * grid[1] >= n_blocks
        blk = lambda b, c: jnp.minimum(b * grid[1] + c, n_blocks - 1)
        d_in = [pl.BlockSpec((MOD_ROWS, D_MODEL), lambda b, c: (0, 0)),
                pl.BlockSpec((None, D_MODEL, ADA_STEP_COLS), lambda b, c: (layer + 1, 0, blk(b, c))),
                pl.BlockSpec((None, 1, ADA_STEP_COLS), lambda b, c: (layer + 1, 0, blk(b, c)))]
        d_args = list(ada_next)
        d_shape = [jax.ShapeDtypeStruct((MOD_ROWS, GATE_COLS), F32)]
        d_out = [pl.BlockSpec((MOD_ROWS, ADA_STEP_COLS), lambda b, c: (0, blk(b, c)))]
    return pl.pallas_call(
        functools.partial(_seq_kernel, ctx=ctx, n_seq=n_seq, n_attn_in=len(a_in), n_mix_in=len(m_in),
                          n_attn_scr=len(a_scr), ada=bool(d_in)),
        out_shape=a_shape + m_shape + d_shape,
        grid=grid,
        in_specs=a_in + m_in + d_in,
        out_specs=a_out + m_out + d_out,
        scratch_shapes=a_scr + m_scr,
        compiler_params=_params("arbitrary", "arbitrary"),
        name="attn_mix_lat" if ctx else "attn_mix_ctx",
    )(*a_args, *m_args, *d_args)


def _outproj_kernel(*refs, first, final, ctx_tiles, layer):
    (ac_ref, al_ref, bc_ref, bl_ref, cc_ref, cl_ref, g0_ref, g1_ref, g2_ref) = refs[:9]
    refs = refs[9:]
    if first:
        xc_ref, xl_ref = refs[:2]
        refs = refs[2:]
    else:
        x_ref = refs[0]
        refs = refs[1:]
    mod_ref, wa_hbm, wb_hbm, wc_hbm, wo_hbm, nw_ref = refs[:6]
    refs = refs[6:]
    if final:
        yc_ref, yl_ref = refs[:2]
        refs = refs[2:]
    else:
        modn_ref, y_ref, hn_ref = refs[:3]
        refs = refs[3:]
    a_scr, b_scr, c_scr, m_scr, wa_ref, wb_ref, wc_ref, wo_ref, stage, sem = refs[:10]
    y_scr = refs[10] if final else y_ref

    @pl.when(pl.program_id(0) == 0)
    def _():
        for w_hbm, w_ref in ((wa_hbm, wa_ref), (wb_hbm, wb_ref), (wc_hbm, wc_ref), (wo_hbm, wo_ref)):
            _load_weight_bf16(w_hbm, layer, w_ref, stage, sem)

    is_ctx = pl.program_id(0) < ctx_tiles
    a_scr[...] = _pick(is_ctx, ac_ref, al_ref)
    b_scr[...] = _pick(is_ctx, bc_ref, bl_ref)
    c_scr[...] = _pick(is_ctx, cc_ref, cl_ref)
    for n in range(D_MODEL // COL_TILE):
        cs = slice(n * COL_TILE, (n + 1) * COL_TILE)
        a = jnp.dot(a_scr[...], wa_ref[:, cs], preferred_element_type=F32)
        b = jnp.dot(b_scr[...], wb_ref[:, cs], preferred_element_type=F32)
        c = jnp.dot(c_scr[...], wc_ref[:, cs], preferred_element_type=F32)
        m = (g0_ref[:, cs].astype(F32) * a + g1_ref[:, cs].astype(F32) * b
             + g2_ref[:, cs].astype(F32) * c)
        m_scr[:, cs] = m.astype(BF16)
    ssq = jnp.zeros((m_scr.shape[0], 1), F32)
    for n in range(D_MODEL // COL_TILE):
        cs = slice(n * COL_TILE, (n + 1) * COL_TILE)
        out = jnp.dot(m_scr[...], wo_ref[:, cs], preferred_element_type=F32)
        x = jnp.where(is_ctx, xc_ref[:, cs], xl_ref[:, cs]) if first else x_ref[:, cs]
        y = x + mod_ref[:, 2 * D_MODEL + n * COL_TILE:2 * D_MODEL + (n + 1) * COL_TILE] * out
        y_scr[:, cs] = y
        ssq = ssq + jnp.sum(y * y, axis=-1, keepdims=True)
    yn = y_scr[...] * lax.rsqrt(ssq * (1.0 / D_MODEL) + EPS) * nw_ref[...]
    if final:
        @pl.when(is_ctx)
        def _():
            yc_ref[...] = yn

        @pl.when(jnp.logical_not(is_ctx))
        def _():
            yl_ref[...] = yn
    else:
        hn_ref[...] = (yn * (1.0 + modn_ref[:, D_MODEL:2 * D_MODEL]) + modn_ref[:, 0:D_MODEL]).astype(BF16)


def _outproj_call(attn, bout, cout, gates, x, mod3, wa, wb, wc, wo, nw, modn3, *, st, layer):
    tm = st.tm
    first = isinstance(x, tuple)
    final = modn3 is None
    gate = lambda g: pl.BlockSpec((tm, D_MODEL), lambda i: (i, g))
    row = pl.BlockSpec((tm, D_MODEL), lambda i: (i, 0))
    in_specs = (st.two_source_specs(A_WIDTH) + st.two_source_specs(B_WIDTH) + st.two_source_specs(C_WIDTH)
                + [gate(0), gate(1), gate(2)]
                + (st.two_source_specs(D_MODEL) if first else [row])
                + [pl.BlockSpec((None, 1, GATE_COLS), st.mod_index),
                   HBM, HBM, HBM, HBM,
                   pl.BlockSpec((1, D_MODEL), lambda i: (0, 0))])
    args = [*attn, *bout, *cout, gates, gates, gates, *(x if first else (x,)), mod3, wa, wb, wc, wo, nw]
    scratch = [pltpu.VMEM((tm, A_WIDTH), BF16), pltpu.VMEM((tm, B_WIDTH), BF16),
               pltpu.VMEM((tm, C_WIDTH), BF16), pltpu.VMEM((tm, D_MODEL), BF16),
               pltpu.VMEM((A_WIDTH, D_MODEL), BF16), pltpu.VMEM((B_WIDTH, D_MODEL), BF16),
               pltpu.VMEM((C_WIDTH, D_MODEL), BF16), pltpu.VMEM((D_MODEL, D_MODEL), BF16),
               pltpu.VMEM((2, OUT_CAST_ROWS, D_MODEL), F32), pltpu.SemaphoreType.DMA((2,))]
    if final:
        out_shape = [jax.ShapeDtypeStruct((st.ctx_tiles * tm, D_MODEL), F32),
                     jax.ShapeDtypeStruct(((st.tiles - st.ctx_tiles) * tm, D_MODEL), F32)]
        out_specs = st.two_source_specs(D_MODEL)
        scratch.append(pltpu.VMEM((tm, D_MODEL), F32))
    else:
        in_specs.append(pl.BlockSpec((None, 1, GATE_COLS), st.mod_index))
        args.append(modn3)
        out_shape = [jax.ShapeDtypeStruct((st.tiles * tm, D_MODEL), F32),
                     jax.ShapeDtypeStruct((st.tiles * tm, D_MODEL), BF16)]
        out_specs = [row, row]
    return pl.pallas_call(
        functools.partial(_outproj_kernel, first=first, final=final, ctx_tiles=st.ctx_tiles, layer=layer),
        out_shape=out_shape,
        grid=(st.tiles,),
        in_specs=in_specs,
        out_specs=out_specs,
        scratch_shapes=scratch,
        compiler_params=_params("arbitrary"),
        name="out_proj_final" if final else "out_proj",
    )(*args)


def _rope_tables(n_identity, n_tokens):
    rows = n_tokens // GRID_W
    row = np.repeat(np.arange(rows), GRID_W).astype(np.float64)
    col = np.tile(np.arange(GRID_W), rows).astype(np.float64)
    n_freq = HEAD_DIM // 4
    inv = ROPE_THETA ** (-np.arange(n_freq, dtype=np.float64) / n_freq)
    ar = row[:, None] * inv[None, :]
    ac = col[:, None] * inv[None, :]
    zero = np.zeros_like(ar)
    cos = np.concatenate([np.cos(ar), np.cos(ar), np.cos(ac), np.cos(ac)], axis=-1)
    sa = np.concatenate([-np.sin(ar), zero, -np.sin(ac), zero], axis=-1)
    sb = np.concatenate([zero, np.sin(ar), zero, np.sin(ac)], axis=-1)
    ident = (np.ones((n_identity, HEAD_DIM)), np.zeros((n_identity, HEAD_DIM)), np.zeros((n_identity, HEAD_DIM)))
    return tuple(jnp.asarray(np.concatenate([i, t], axis=0), dtype=F32) for i, t in zip(ident, (cos, sa, sb)))


def kernel(x_prompt, x_sample, cache_k, cache_v, c, c_ctx, norm_w, w_ada, b_ada, w_in, q_norm_w,
           k_norm_w, sgu_norm_w, w_sgu, b_sgu, w_pool, pool_scale, w_br_a, w_br_b, w_br_c, w_merge,
           b_merge, w_out, final_norm_w):
    nb_p, seq_p, d = x_prompt.shape
    nb_s, seq_s, _ = x_sample.shape
    n_ctx, n_lat = nb_p * seq_p, nb_s * seq_s
    assert d == D_MODEL and nb_s + 1 <= MOD_ROWS

    cv = jnp.concatenate([c_ctx[None, :], c, jnp.zeros((MOD_ROWS - 1 - nb_s, d), F32)], axis=0)
    b_ada3 = b_ada.reshape(DEPTH, 1, GATE_COLS)
    mod3 = [_ada_call(cv, w_ada, b_ada3).reshape(MOD_ROWS, 1, GATE_COLS)]
    st_in = _Stream(n_ctx, n_lat, seq_s, IN_TOKEN_TILE)
    st_out = _Stream(n_ctx, n_lat, seq_s, OUT_TOKEN_TILE)
    rope_tabs = _rope_tables(IN_TOKEN_TILE, seq_s)
    cache = (cache_k, cache_v)

    x = (x_prompt.reshape(n_ctx, d), x_sample.reshape(n_lat, d))
    h = _normmod_call(*x, mod3[0], norm_w[0].reshape(1, d), seq=seq_s)
    states = []
    w_mg, w_i, wa, wb, wc, wo = w_merge, w_in, w_br_a, w_br_b, w_br_c, w_out
    for l in range(DEPTH):
        last = l == DEPTH - 1
        bm = b_merge[l].reshape(1, GATE_COLS)
        qnw = q_norm_w[l].reshape(1, HEAD_DIM)
        knw = k_norm_w[l].reshape(1, HEAD_DIM)
        mix_w = (sgu_norm_w[l].reshape(1, B_WIDTH), w_sgu[l].astype(BF16),
                 b_sgu[l].reshape(B_GROUPS, CHUNK, 1), w_pool[l].astype(BF16),
                 pool_scale[l].reshape(1, C_WIDTH))
        nw_next = (final_norm_w if last else norm_w[l + 1]).reshape(1, d)

        gates = _gates_call(h, w_mg, bm, layer=l)
        proj, ks, vs = _proj_call(h, w_i, qnw, knw, rope_tabs, st=st_in, layer=l)
        states.append((ks, vs))
        seq_c = _seq_call(proj, None, mix_w, None, row0=0, n_batch=nb_p, seq=seq_p, layer=l)
        seq_l = _seq_call(proj, cache, mix_w, None if last else (cv, w_ada, b_ada3),
                          row0=n_ctx, n_batch=nb_s, seq=seq_s, layer=l)
        if not last:
            *seq_l, mod_next = seq_l
            mod3.append(mod_next.reshape(MOD_ROWS, 1, GATE_COLS))
        attn, bout, cout = zip(seq_c, seq_l)
        res = _outproj_call(attn, bout, cout, gates, x, mod3[l],
                            wa, wb, wc, wo, nw_next, None if last else mod3[l + 1], st=st_out, layer=l)
        if last:
            y_ctx, y_lat = res
        else:
            x, h = res

    state_k, state_v = (
        jnp.stack([s[i][:n_ctx].reshape(nb_p, seq_p, A_KV_HEADS, HEAD_DIM) for s in states], axis=1)
        for i in range(2))
    return (y_ctx.reshape(nb_p, seq_p, d), y_lat.reshape(nb_s, seq_s, d), state_k, state_v)
```

```python
import functools

import jax
import jax.numpy as jnp
import numpy as np
from jax import lax
from jax.experimental import pallas as pl
from jax.experimental.pallas import tpu as pltpu

F32 = jnp.float32
BF16 = jnp.bfloat16

D_MODEL = 2048
DEPTH = 2
GRID_W = 64
EPS = 1e-6
HEAD_DIM = 128
A_HEADS = 8
A_KV_HEADS = 2
A_WIDTH = A_HEADS * HEAD_DIM
KV_WIDTH = A_KV_HEADS * HEAD_DIM
ROPE_THETA = 10000.0
ROPE_PARTNER = HEAD_DIM // 4
ATTN_SCALE = HEAD_DIM ** -0.5
LOG2_E = 1.4426950408889634
CHUNK = 128
B_GROUPS = 4
B_WIDTH = 512
C_WIDTH = 512
POOL_WINDOWS = (2, 4, 8, 16)
POOL_HALO = 64
N_BRANCH = 3
GATE_COLS = N_BRANCH * D_MODEL
IN_COLS = 2 * A_WIDTH + 2 * KV_WIDTH + 3 * B_WIDTH + 2 * C_WIDTH

COL_TILE = 512
Q_TILE0 = 0
KV_TILE = Q_TILE0 + A_WIDTH // COL_TILE
GA_TILE0 = KV_TILE + 1
U_TILE = GA_TILE0 + A_WIDTH // COL_TILE
VB_TILE = U_TILE + 1
GB_TILE = VB_TILE + 1
Z_TILE = GB_TILE + 1
GC_TILE = Z_TILE + 1
HEADS_PER_TILE = COL_TILE // HEAD_DIM
K_COL = KV_TILE * COL_TILE
V_COL = K_COL + KV_WIDTH

SUB_COLS = 256
MOD_ROWS = 8
ADA_TILE = 1024
ADA_STEP_COLS = 384
IN_TOKEN_TILE = 512
NORM_TOKEN_TILE = 1024
NORM_RING = 3
NORM_ROWS = 16
OUT_TOKEN_TILE = 256
SEQ_TILE = 512
SHORT_SEQ_ROWS = 1024
CAST_ROWS = 128
OUT_CAST_ROWS = 256
CAST_RING = 3
VMEM_LIMIT = 56 * 1024 * 1024


def _params(*sem):
    return pltpu.CompilerParams(dimension_semantics=sem, vmem_limit_bytes=VMEM_LIMIT)


def _resident(shape):
    return pl.BlockSpec(shape, lambda *_: (0,) * len(shape), pipeline_mode=pl.Buffered(1))


def _silu(x):
    return x * jax.nn.sigmoid(x)


def _rms(x, w):
    ms = jnp.mean(x * x, axis=-1, keepdims=True)
    return x * lax.rsqrt(ms + EPS) * w


class _Stream:
    def __init__(self, n_ctx, n_lat, seq, tm):
        assert n_ctx % tm == 0 and seq % tm == 0
        self.tm = tm
        self.ctx_tiles = n_ctx // tm
        self.tiles = (n_ctx + n_lat) // tm
        self.tiles_per_seq = seq // tm

    def mod_index(self, i):
        lat = 1 + (i - self.ctx_tiles) // self.tiles_per_seq
        return (jnp.where(i < self.ctx_tiles, 0, lat), 0, 0)

    def two_source_specs(self, width):
        ctx = pl.BlockSpec((self.tm, width), lambda i: (jnp.minimum(i, self.ctx_tiles - 1), 0))
        lat = pl.BlockSpec((self.tm, width), lambda i: (jnp.maximum(i - self.ctx_tiles, 0), 0))
        return [ctx, lat]


def _pick(is_ctx, ctx_ref, lat_ref):
    return jnp.where(is_ctx, ctx_ref[...], lat_ref[...])


def _load_weight_bf16(w_hbm, layer, w_scr, stage, sem):
    ring, chunk = stage.shape[0], stage.shape[1]
    n = w_scr.shape[0] // chunk

    def copy(c):
        return pltpu.make_async_copy(w_hbm.at[layer, pl.ds(c * chunk, chunk), :], stage.at[c % ring], sem.at[c % ring])

    for c in range(min(ring - 1, n)):
        copy(c).start()
    for c in range(n):
        if c + ring - 1 < n:
            copy(c + ring - 1).start()
        copy(c).wait()
        w_scr[c * chunk:(c + 1) * chunk, :] = stage[c % ring].astype(BF16)


def _weight_scratch(rows, cols, chunk):
    return [pltpu.VMEM((rows, cols), BF16), pltpu.VMEM((CAST_RING, chunk, cols), F32),
            pltpu.SemaphoreType.DMA((CAST_RING,))]


HBM = pl.BlockSpec(memory_space=pl.ANY)


def _ada_kernel(cv_ref, w_ref, b_ref, o_ref):
    a = _silu(cv_ref[...]).astype(BF16)
    o_ref[...] = jnp.dot(a, w_ref[...].astype(BF16), preferred_element_type=F32) + b_ref[...]


def _ada_call(cv, w_ada, b_ada):
    return pl.pallas_call(
        _ada_kernel,
        out_shape=jax.ShapeDtypeStruct((MOD_ROWS, GATE_COLS), F32),
        grid=(GATE_COLS // ADA_TILE,),
        in_specs=[
            pl.BlockSpec((MOD_ROWS, D_MODEL), lambda j: (0, 0)),
            pl.BlockSpec((None, D_MODEL, ADA_TILE), lambda j: (0, 0, j)),
            pl.BlockSpec((None, 1, ADA_TILE), lambda j: (0, 0, j)),
        ],
        out_specs=pl.BlockSpec((MOD_ROWS, ADA_TILE), lambda j: (0, j)),
        compiler_params=_params("arbitrary"),
        name="ada_mod",
    )(cv, w_ada, b_ada)


def _normmod_kernel(xc_hbm, xl_hbm, mod_ref, nw_ref, h_ref, xbuf, sem, *, ctx_tiles, n_tiles):
    tm = h_ref.shape[0]
    i = pl.program_id(0)

    def tile_copy(t, wait):
        slot = t % NORM_RING
        for is_ctx, src, row in ((True, xc_hbm, t * tm), (False, xl_hbm, (t - ctx_tiles) * tm)):
            @pl.when((t < ctx_tiles) == is_ctx)
            def _():
                cp = pltpu.make_async_copy(src.at[pl.ds(pl.multiple_of(row, tm), tm), :], xbuf.at[slot], sem.at[slot])
                cp.wait() if wait else cp.start()

    @pl.when(i == 0)
    def _():
        for t in range(NORM_RING - 1):
            tile_copy(jnp.int32(t), wait=False)

    @pl.when(i + NORM_RING - 1 < n_tiles)
    def _():
        tile_copy(i + NORM_RING - 1, wait=False)

    tile_copy(i, wait=True)
    x_ref = xbuf.at[i % NORM_RING]

    def chunk(r, carry):
        rows = pl.ds(pl.multiple_of(r * NORM_ROWS, NORM_ROWS), NORM_ROWS)
        y = _rms(x_ref[rows, :], nw_ref[...])
        h_ref[rows, :] = (y * (1.0 + mod_ref[:, D_MODEL:2 * D_MODEL]) + mod_ref[:, 0:D_MODEL]).astype(BF16)
        return carry

    lax.fori_loop(0, tm // NORM_ROWS, chunk, 0, unroll=8)


def _normmod_call(x_ctx, x_lat, mod3, norm_w, *, seq):
    st = _Stream(x_ctx.shape[0], x_lat.shape[0], seq, NORM_TOKEN_TILE)
    assert st.ctx_tiles >= NORM_RING - 1 and st.tiles >= NORM_RING
    return pl.pallas_call(
        functools.partial(_normmod_kernel, ctx_tiles=st.ctx_tiles, n_tiles=st.tiles),
        out_shape=jax.ShapeDtypeStruct((st.tiles * st.tm, D_MODEL), BF16),
        grid=(st.tiles,),
        in_specs=[HBM, HBM,
                  pl.BlockSpec((None, 1, GATE_COLS), st.mod_index),
                  pl.BlockSpec((1, D_MODEL), lambda i: (0, 0))],
        out_specs=pl.BlockSpec((st.tm, D_MODEL), lambda i: (i, 0)),
        scratch_shapes=[pltpu.VMEM((NORM_RING, st.tm, D_MODEL), F32), pltpu.SemaphoreType.DMA((NORM_RING,))],
        compiler_params=_params("arbitrary"),
        name="norm_mod",
    )(x_ctx, x_lat, mod3, norm_w)


def _gates_kernel(h_ref, w_hbm, b_ref, o_ref, w_ref, stage, sem, *, layer):
    @pl.when(pl.program_id(0) == 0)
    def _():
        _load_weight_bf16(w_hbm, layer, w_ref, stage, sem)

    for s in range(GATE_COLS // SUB_COLS):
        cs = slice(s * SUB_COLS, (s + 1) * SUB_COLS)
        acc = jnp.dot(h_ref[...], w_ref[:, cs], preferred_element_type=F32)
        o_ref[:, cs] = jax.nn.sigmoid(acc + b_ref[:, cs]).astype(BF16)


def _gates_call(h, w_merge, b_merge, *, layer):
    t = h.shape[0]
    tm = IN_TOKEN_TILE
    return pl.pallas_call(
        functools.partial(_gates_kernel, layer=layer),
        out_shape=jax.ShapeDtypeStruct((t, GATE_COLS), BF16),
        grid=(t // tm,),
        in_specs=[
            pl.BlockSpec((tm, D_MODEL), lambda i: (i, 0)),
            HBM,
            _resident((1, GATE_COLS)),
        ],
        out_specs=pl.BlockSpec((tm, GATE_COLS), lambda i: (i, 0)),
        scratch_shapes=_weight_scratch(D_MODEL, GATE_COLS, CAST_ROWS),
        compiler_params=_params("arbitrary"),
        name="gates",
    )(h, w_merge, b_merge)


def _rope(y, cos, sa, sb):
    return (y * cos + pltpu.roll(y, HEAD_DIM - ROPE_PARTNER, 1) * sa + pltpu.roll(y, ROPE_PARTNER, 1) * sb)


def _proj_kernel(h_ref, w_hbm, qnw_ref, knw_ref, cos_ref, sa_ref, sb_ref, proj_ref, ks_ref, vs_ref,
                 w_ref, stage, sem, *, layer):
    @pl.when(pl.program_id(0) == 0)
    def _():
        _load_weight_bf16(w_hbm, layer, w_ref, stage, sem)

    def head(xh, w):
        return _rope(_rms(xh, w), cos_ref[...], sa_ref[...], sb_ref[...])

    qw = qnw_ref[...] * (ATTN_SCALE * LOG2_E)
    silu_cols = ((GA_TILE0 * COL_TILE, U_TILE * COL_TILE), (GB_TILE * COL_TILE, Z_TILE * COL_TILE),
                 (GC_TILE * COL_TILE, IN_COLS))
    for s in range(IN_COLS // SUB_COLS):
        acc = jnp.dot(h_ref[...], w_ref[:, s * SUB_COLS:(s + 1) * SUB_COLS], preferred_element_type=F32)
        for hb in range(SUB_COLS // HEAD_DIM):
            c0 = s * SUB_COLS + hb * HEAD_DIM
            cs = slice(c0, c0 + HEAD_DIM)
            a = acc[:, hb * HEAD_DIM:(hb + 1) * HEAD_DIM]
            if c0 < K_COL:
                proj_ref[:, cs] = head(a, qw).astype(BF16)
            elif c0 < V_COL:
                ks_ref[:, c0 - K_COL:c0 - K_COL + HEAD_DIM] = _rms(a, knw_ref[...])
                proj_ref[:, cs] = head(a, knw_ref[...]).astype(BF16)
            elif c0 < V_COL + KV_WIDTH:
                vs_ref[:, c0 - V_COL:c0 - V_COL + HEAD_DIM] = a
                proj_ref[:, cs] = a.astype(BF16)
            elif any(lo <= c0 < hi for lo, hi in silu_cols):
                proj_ref[:, cs] = _silu(a).astype(BF16)
            else:
                proj_ref[:, cs] = a.astype(BF16)


def _proj_call(h, w_in, qnw, knw, rope_tabs, *, st, layer):
    t = h.shape[0]
    tm = st.tm

    def tab_index(i):
        return (jnp.where(i < st.ctx_tiles, 0, 1 + (i - st.ctx_tiles) % st.tiles_per_seq), 0)

    in_specs = [
        pl.BlockSpec((tm, D_MODEL), lambda i: (i, 0)),
        HBM,
        pl.BlockSpec((1, HEAD_DIM), lambda i: (0, 0)),
        pl.BlockSpec((1, HEAD_DIM), lambda i: (0, 0)),
    ] + [pl.BlockSpec((tm, HEAD_DIM), tab_index) for _ in rope_tabs]
    out_shape = [jax.ShapeDtypeStruct((t, IN_COLS), BF16),
                 jax.ShapeDtypeStruct((t, KV_WIDTH), F32), jax.ShapeDtypeStruct((t, KV_WIDTH), F32)]
    out_specs = [pl.BlockSpec((tm, IN_COLS), lambda i: (i, 0)),
                 pl.BlockSpec((tm, KV_WIDTH), lambda i: (i, 0)), pl.BlockSpec((tm, KV_WIDTH), lambda i: (i, 0))]
    return pl.pallas_call(
        functools.partial(_proj_kernel, layer=layer),
        out_shape=out_shape,
        grid=(t // tm,),
        in_specs=in_specs,
        out_specs=out_specs,
        scratch_shapes=_weight_scratch(D_MODEL, IN_COLS, CAST_ROWS),
        compiler_params=_params("arbitrary"),
        name="proj",
    )(h, w_in, qnw, knw, *rope_tabs)


def _attn_kernel(*refs, ctx, n_seq):
    q_ref, k_ref, v_ref, ga0_ref, ga1_ref = refs[:5]
    ga_refs = (ga0_ref, ga1_ref)
    refs = refs[5:]
    if ctx:
        ck_ref, cv_ref = refs[:2]
        refs = refs[2:]
    o_ref, vx_scr = refs[:2]
    if ctx:
        ckx_scr, cvx_scr = refs[2:]
    seq = k_ref.shape[0] // n_seq
    tq = q_ref.shape[0] // n_seq

    @pl.when(pl.program_id(1) == 0)
    def _():
        for j in range(n_seq):
            for kh in range(A_KV_HEADS):
                hs = slice(kh * HEAD_DIM, (kh + 1) * HEAD_DIM)
                vx_scr[j, kh, :, :HEAD_DIM] = v_ref[j * seq:(j + 1) * seq, hs]
                vx_scr[j, kh, :, HEAD_DIM:] = jnp.ones((seq, HEAD_DIM), BF16)
        if ctx:
            for kh in range(A_KV_HEADS):
                ckx_scr[kh] = ck_ref[:, kh, :].astype(BF16)
                cvx_scr[kh, :, :HEAD_DIM] = cv_ref[:, kh, :].astype(BF16)
                cvx_scr[kh, :, HEAD_DIM:] = jnp.ones((cvx_scr.shape[1], HEAD_DIM), BF16)

    nt = (((1,), (1,)), ((), ()))
    for j in range(n_seq):
        qrows = slice(j * tq, (j + 1) * tq)
        for head in range(A_HEADS):
            kh, hh = divmod(head, HEADS_PER_TILE)
            cs = slice(head * HEAD_DIM, (head + 1) * HEAD_DIM)
            q = q_ref[qrows, cs]
            k = k_ref[j * seq:(j + 1) * seq, kh * HEAD_DIM:(kh + 1) * HEAD_DIM]
            s1 = lax.dot_general(q, k, nt, preferred_element_type=F32)
            m = jnp.max(s1, axis=-1, keepdims=True)
            if ctx:
                s2 = lax.dot_general(q, ckx_scr[kh], nt, preferred_element_type=F32)
                m = jnp.maximum(m, jnp.max(s2, axis=-1, keepdims=True))
            ox = jnp.dot(jnp.exp2(s1 - m).astype(BF16), vx_scr[j, kh], preferred_element_type=F32)
            if ctx:
                ox = ox + jnp.dot(jnp.exp2(s2 - m).astype(BF16), cvx_scr[kh], preferred_element_type=F32)
            o = ox[:, :HEAD_DIM] / ox[:, HEAD_DIM:]
            ga = ga_refs[kh][qrows, hh * HEAD_DIM:(hh + 1) * HEAD_DIM]
            o_ref[qrows, cs] = (o * ga.astype(F32)).astype(BF16)


def _attn_specs(proj, cache, *, row0, n_batch, seq, layer, n_seq):
    tq = min(SEQ_TILE, seq)
    nq = seq // tq
    qb, sb = tq * n_seq, seq * n_seq
    q0, s0 = row0 // qb, row0 // sb
    in_specs = [
        pl.BlockSpec((qb, A_WIDTH), lambda b, qi: (q0 + b * nq + qi, Q_TILE0 * COL_TILE // A_WIDTH)),
        pl.BlockSpec((sb, KV_WIDTH), lambda b, qi: (s0 + b, K_COL // KV_WIDTH)),
        pl.BlockSpec((sb, KV_WIDTH), lambda b, qi: (s0 + b, V_COL // KV_WIDTH)),
        pl.BlockSpec((qb, COL_TILE), lambda b, qi: (q0 + b * nq + qi, GA_TILE0)),
        pl.BlockSpec((qb, COL_TILE), lambda b, qi: (q0 + b * nq + qi, GA_TILE0 + 1)),
    ]
    args = [proj, proj, proj, proj, proj]
    scratch = [pltpu.VMEM((n_seq, A_KV_HEADS, seq, 2 * HEAD_DIM), BF16)]
    if cache is not None:
        past = cache[0].shape[2]
        for c in cache:
            in_specs.append(pl.BlockSpec((None, None, past, A_KV_HEADS, HEAD_DIM),
                                         lambda b, qi: (b, layer, 0, 0, 0)))
            args.append(c)
        scratch += [pltpu.VMEM((A_KV_HEADS, past, HEAD_DIM), BF16),
                    pltpu.VMEM((A_KV_HEADS, past, 2 * HEAD_DIM), BF16)]
    out_shape = [jax.ShapeDtypeStruct((n_batch * seq, A_WIDTH), BF16)]
    out_specs = [pl.BlockSpec((qb, A_WIDTH), lambda b, qi: (b * nq + qi, 0))]
    return in_specs, args, out_shape, out_specs, scratch


def _mix_kernel(u_ref, vb_ref, gb_ref, z_ref, gc_ref, snw_ref, ws_ref, bs_ref, wp_ref, ps_ref, band_ref, inv_ref,
                bo_ref, co_ref, zb_scr, *, n_seq):
    r = u_ref.shape[0] // n_seq
    c = pl.program_id(1)
    nc = pl.num_programs(1)
    base = pl.multiple_of(c * r, r)
    n_chunks = r // CHUNK
    z_rows = z_ref.shape[0]
    for j in range(n_seq):
        row0 = j * r

        vbn = _rms(vb_ref[row0:row0 + r, :].astype(F32), snw_ref[...]).astype(BF16)
        for g in range(B_GROUPS):
            cs = slice(g * HEAD_DIM, (g + 1) * HEAD_DIM)
            wide = jnp.concatenate([vbn[cc * CHUNK:(cc + 1) * CHUNK, cs] for cc in range(n_chunks)], axis=1)
            mixed = jnp.dot(ws_ref[g], wide, preferred_element_type=F32) + bs_ref[g]
            for cc in range(n_chunks):
                rs = slice(row0 + cc * CHUNK, row0 + (cc + 1) * CHUNK)
                mx = mixed[:, cc * HEAD_DIM:(cc + 1) * HEAD_DIM]
                bo_ref[rs, cs] = (u_ref[rs, cs].astype(F32) * mx * gb_ref[rs, cs].astype(F32)).astype(BF16)

        zb = zb_scr.at[j]
        z0 = base + row0
        zb[POOL_HALO:POOL_HALO + r, :] = z_ref[pl.ds(pl.multiple_of(z0, POOL_HALO), r), :]
        above = z_ref[pl.ds(pl.multiple_of(jnp.maximum(z0 - POOL_HALO, 0), POOL_HALO), POOL_HALO), :]
        below = z_ref[pl.ds(pl.multiple_of(jnp.minimum(z0 + r, z_rows - POOL_HALO), POOL_HALO), POOL_HALO), :]
        zeros = jnp.zeros((POOL_HALO, C_WIDTH), BF16)
        zb[0:POOL_HALO, :] = jnp.where(c > 0, above, zeros)
        zb[POOL_HALO + r:, :] = jnp.where(c < nc - 1, below, zeros)

        for g in range(len(POOL_WINDOWS)):
            cs = slice(g * HEAD_DIM, (g + 1) * HEAD_DIM)
            ds = []
            for blk in range(n_chunks):
                lo = blk * CHUNK
                wsum = jnp.dot(band_ref[g], zb[lo:lo + CHUNK + 2 * POOL_HALO, cs], preferred_element_type=F32)
                zc = zb[POOL_HALO + lo:POOL_HALO + lo + CHUNK, cs].astype(F32)
                ds.append((wsum * inv_ref[lo:lo + CHUNK, cs] - zc).astype(BF16))
            dm = jnp.dot(jnp.concatenate(ds, axis=0), wp_ref[g], preferred_element_type=F32)
            rows = slice(row0, row0 + r)
            co_ref[rows, cs] = (dm * ps_ref[:, cs] * gc_ref[rows, cs].astype(F32)).astype(BF16)


def _pool_band():
    t = np.arange(CHUNK)[:, None] + POOL_HALO
    j = np.arange(CHUNK + 2 * POOL_HALO)[None, :]
    return jnp.asarray(np.stack([(j >= t - w // 2) & (j < t + w - w // 2) for w in POOL_WINDOWS]), dtype=BF16)


def _pool_inv_count(seq):
    t = np.arange(seq)
    cols = [1.0 / (np.clip(t + w - w // 2, 0, seq) - np.clip(t - w // 2, 0, seq)) for w in POOL_WINDOWS]
    return jnp.asarray(np.repeat(np.stack(cols, axis=1), HEAD_DIM, axis=1), dtype=F32)


def _mix_specs(proj, snw, ws, bs, wp, ps, *, row0, n_batch, seq, n_seq):
    r = min(SEQ_TILE, seq)
    nc = seq // r
    rb, sb = r * n_seq, seq * n_seq
    r0, s0 = row0 // rb, row0 // sb
    tile = lambda col: pl.BlockSpec((rb, COL_TILE), lambda b, c: (r0 + b * nc + c, col))
    const3 = lambda shape: pl.BlockSpec(shape, lambda b, c: (0, 0, 0))
    in_specs = [
        tile(U_TILE), tile(VB_TILE), tile(GB_TILE),
        pl.BlockSpec((sb, COL_TILE), lambda b, c: (s0 + b, Z_TILE)),
        tile(GC_TILE),
        pl.BlockSpec((1, B_WIDTH), lambda b, c: (0, 0)),
        const3((B_GROUPS, CHUNK, CHUNK)),
        const3((B_GROUPS, CHUNK, 1)),
        const3((B_GROUPS, HEAD_DIM, HEAD_DIM)),
        pl.BlockSpec((1, C_WIDTH), lambda b, c: (0, 0)),
        const3((len(POOL_WINDOWS), CHUNK, CHUNK + 2 * POOL_HALO)),
        pl.BlockSpec((r, C_WIDTH), lambda b, c: (c, 0)),
    ]
    args = [proj, proj, proj, proj, proj, snw, ws, bs, wp, ps, _pool_band(), _pool_inv_count(seq)]
    out_shape = [jax.ShapeDtypeStruct((n_batch * seq, B_WIDTH), BF16),
                 jax.ShapeDtypeStruct((n_batch * seq, C_WIDTH), BF16)]
    out_spec = pl.BlockSpec((rb, COL_TILE), lambda b, c: (b * nc + c, 0))
    scratch = [pltpu.VMEM((n_seq, r + 2 * POOL_HALO, C_WIDTH), BF16)]
    return in_specs, args, out_shape, [out_spec, out_spec], scratch


def _seq_kernel(*refs, ctx, n_seq, n_attn_in, n_mix_in, n_attn_scr, ada):
    attn_in, refs = refs[:n_attn_in], refs[n_attn_in:]
    mix_in, refs = refs[:n_mix_in], refs[n_mix_in:]
    if ada:
        ada_in, refs = refs[:3], refs[3:]
    o_ref, bo_ref, co_ref = refs[:3]
    refs = refs[3:]
    if ada:
        mo_ref, refs = refs[0], refs[1:]
    attn_scr, (zb_scr,) = refs[:n_attn_scr], refs[n_attn_scr:]
    _attn_kernel(*attn_in, o_ref, *attn_scr, ctx=ctx, n_seq=n_seq)
    _mix_kernel(*mix_in, bo_ref, co_ref, zb_scr, n_seq=n_seq)
    if ada:
        _ada_kernel(*ada_in, mo_ref)


def _seq_call(proj, cache, mix_w, ada_next, *, row0, n_batch, seq, layer):
    whole = seq <= SEQ_TILE
    n_seq = max(SHORT_SEQ_ROWS // seq, 1) if whole else 1
    assert n_batch % n_seq == 0 and (whole or seq % SEQ_TILE == 0)
    a_in, a_args, a_shape, a_out, a_scr = _attn_specs(proj, cache, row0=row0, n_batch=n_batch, seq=seq,
                                                      layer=layer, n_seq=n_seq)
    m_in, m_args, m_shape, m_out, m_scr = _mix_specs(proj, *mix_w, row0=row0, n_batch=n_batch, seq=seq, n_seq=n_seq)
    ctx = cache is not None
    grid = (n_batch // n_seq, max(seq // SEQ_TILE, 1))
    d_in, d_args, d_shape, d_out = [], [], [], []
    if ada_next is not None:
        n_blocks = GATE_COLS // ADA_STEP_COLS
        assert grid[0] * grid[1] >= n_blocks
        blk = lambda b, c: jnp.minimum(b * grid[1] + c, n_blocks - 1)
        d_in = [pl.BlockSpec((MOD_ROWS, D_MODEL), lambda b, c: (0, 0)),
                pl.BlockSpec((None, D_MODEL, ADA_STEP_COLS), lambda b, c: (layer + 1, 0, blk(b, c))),
                pl.BlockSpec((None, 1, ADA_STEP_COLS), lambda b, c: (layer + 1, 0, blk(b, c)))]
        d_args = list(ada_next)
        d_shape = [jax.ShapeDtypeStruct((MOD_ROWS, GATE_COLS), F32)]
        d_out = [pl.BlockSpec((MOD_ROWS, ADA_STEP_COLS), lambda b, c: (0, blk(b, c)))]
    return pl.pallas_call(
        functools.partial(_seq_kernel, ctx=ctx, n_seq=n_seq, n_attn_in=len(a_in), n_mix_in=len(m_in),
                          n_attn_scr=len(a_scr), ada=bool(d_in)),
        out_shape=a_shape + m_shape + d_shape,
        grid=grid,
        in_specs=a_in + m_in + d_in,
        out_specs=a_out + m_out + d_out,
        scratch_shapes=a_scr + m_scr,
        compiler_params=_params("arbitrary", "arbitrary"),
        name="attn_mix_lat" if ctx else "attn_mix_ctx",
    )(*a_args, *m_args, *d_args)


def _outproj_kernel(*refs, first, final, ctx_tiles, layer):
    (ac_ref, al_ref, bc_ref, bl_ref, cc_ref, cl_ref, g0_ref, g1_ref, g2_ref) = refs[:9]
    refs = refs[9:]
    if first:
        xc_ref, xl_ref = refs[:2]
        refs = refs[2:]
    else:
        x_ref = refs[0]
        refs = refs[1:]
    mod_ref, wa_hbm, wb_hbm, wc_hbm, wo_hbm, nw_ref = refs[:6]
    refs = refs[6:]
    if final:
        yc_ref, yl_ref = refs[:2]
        refs = refs[2:]
    else:
        modn_ref, y_ref, hn_ref = refs[:3]
        refs = refs[3:]
    a_scr, b_scr, c_scr, m_scr, wa_ref, wb_ref, wc_ref, wo_ref, stage, sem = refs[:10]
    y_scr = refs[10] if final else y_ref

    @pl.when(pl.program_id(0) == 0)
    def _():
        for w_hbm, w_ref in ((wa_hbm, wa_ref), (wb_hbm, wb_ref), (wc_hbm, wc_ref), (wo_hbm, wo_ref)):
            _load_weight_bf16(w_hbm, layer, w_ref, stage, sem)

    is_ctx = pl.program_id(0) < ctx_tiles
    a_scr[...] = _pick(is_ctx, ac_ref, al_ref)
    b_scr[...] = _pick(is_ctx, bc_ref, bl_ref)
    c_scr[...] = _pick(is_ctx, cc_ref, cl_ref)
    for n in range(D_MODEL // COL_TILE):
        cs = slice(n * COL_TILE, (n + 1) * COL_TILE)
        a = jnp.dot(a_scr[...], wa_ref[:, cs], preferred_element_type=F32)
        b = jnp.dot(b_scr[...], wb_ref[:, cs], preferred_element_type=F32)
        c = jnp.dot(c_scr[...], wc_ref[:, cs], preferred_element_type=F32)
        m = (g0_ref[:, cs].astype(F32) * a + g1_ref[:, cs].astype(F32) * b
             + g2_ref[:, cs].astype(F32) * c)
        m_scr[:, cs] = m.astype(BF16)
    ssq = jnp.zeros((m_scr.shape[0], 1), F32)
    for n in range(D_MODEL // COL_TILE):
        cs = slice(n * COL_TILE, (n + 1) * COL_TILE)
        out = jnp.dot(m_scr[...], wo_ref[:, cs], preferred_element_type=F32)
        x = jnp.where(is_ctx, xc_ref[:, cs], xl_ref[:, cs]) if first else x_ref[:, cs]
        y = x + mod_ref[:, 2 * D_MODEL + n * COL_TILE:2 * D_MODEL + (n + 1) * COL_TILE] * out
        y_scr[:, cs] = y
        ssq = ssq + jnp.sum(y * y, axis=-1, keepdims=True)
    yn = y_scr[...] * lax.rsqrt(ssq * (1.0 / D_MODEL) + EPS) * nw_ref[...]
    if final:
        @pl.when(is_ctx)
        def _():
            yc_ref[...] = yn

        @pl.when(jnp.logical_not(is_ctx))
        def _():
            yl_ref[...] = yn
    else:
        hn_ref[...] = (yn * (1.0 + modn_ref[:, D_MODEL:2 * D_MODEL]) + modn_ref[:, 0:D_MODEL]).astype(BF16)


def _outproj_call(attn, bout, cout, gates, x, mod3, wa, wb, wc, wo, nw, modn3, *, st, layer):
    tm = st.tm
    first = isinstance(x, tuple)
    final = modn3 is None
    gate = lambda g: pl.BlockSpec((tm, D_MODEL), lambda i: (i, g))
    row = pl.BlockSpec((tm, D_MODEL), lambda i: (i, 0))
    in_specs = (st.two_source_specs(A_WIDTH) + st.two_source_specs(B_WIDTH) + st.two_source_specs(C_WIDTH)
                + [gate(0), gate(1), gate(2)]
                + (st.two_source_specs(D_MODEL) if first else [row])
                + [pl.BlockSpec((None, 1, GATE_COLS), st.mod_index),
                   HBM, HBM, HBM, HBM,
                   pl.BlockSpec((1, D_MODEL), lambda i: (0, 0))])
    args = [*attn, *bout, *cout, gates, gates, gates, *(x if first else (x,)), mod3, wa, wb, wc, wo, nw]
    scratch = [pltpu.VMEM((tm, A_WIDTH), BF16), pltpu.VMEM((tm, B_WIDTH), BF16),
               pltpu.VMEM((tm, C_WIDTH), BF16), pltpu.VMEM((tm, D_MODEL), BF16),
               pltpu.VMEM((A_WIDTH, D_MODEL), BF16), pltpu.VMEM((B_WIDTH, D_MODEL), BF16),
               pltpu.VMEM((C_WIDTH, D_MODEL), BF16), pltpu.VMEM((D_MODEL, D_MODEL), BF16),
               pltpu.VMEM((CAST_RING, OUT_CAST_ROWS, D_MODEL), F32), pltpu.SemaphoreType.DMA((CAST_RING,))]
    if final:
        out_shape = [jax.ShapeDtypeStruct((st.ctx_tiles * tm, D_MODEL), F32),
                     jax.ShapeDtypeStruct(((st.tiles - st.ctx_tiles) * tm, D_MODEL), F32)]
        out_specs = st.two_source_specs(D_MODEL)
        scratch.append(pltpu.VMEM((tm, D_MODEL), F32))
    else:
        in_specs.append(pl.BlockSpec((None, 1, GATE_COLS), st.mod_index))
        args.append(modn3)
        out_shape = [jax.ShapeDtypeStruct((st.tiles * tm, D_MODEL), F32),
                     jax.ShapeDtypeStruct((st.tiles * tm, D_MODEL), BF16)]
        out_specs = [row, row]
    return pl.pallas_call(
        functools.partial(_outproj_kernel, first=first, final=final, ctx_tiles=st.ctx_tiles, layer=layer),
        out_shape=out_shape,
        grid=(st.tiles,),
        in_specs=in_specs,
        out_specs=out_specs,
        scratch_shapes=scratch,
        compiler_params=_params("arbitrary"),
        name="out_proj_final" if final else "out_proj",
    )(*args)


def _rope_tables(n_identity, n_tokens):
    rows = n_tokens // GRID_W
    row = np.repeat(np.arange(rows), GRID_W).astype(np.float64)
    col = np.tile(np.arange(GRID_W), rows).astype(np.float64)
    n_freq = HEAD_DIM // 4
    inv = ROPE_THETA ** (-np.arange(n_freq, dtype=np.float64) / n_freq)
    ar = row[:, None] * inv[None, :]
    ac = col[:, None] * inv[None, :]
    zero = np.zeros_like(ar)
    cos = np.concatenate([np.cos(ar), np.cos(ar), np.cos(ac), np.cos(ac)], axis=-1)
    sa = np.concatenate([-np.sin(ar), zero, -np.sin(ac), zero], axis=-1)
    sb = np.concatenate([zero, np.sin(ar), zero, np.sin(ac)], axis=-1)
    ident = (np.ones((n_identity, HEAD_DIM)), np.zeros((n_identity, HEAD_DIM)), np.zeros((n_identity, HEAD_DIM)))
    return tuple(jnp.asarray(np.concatenate([i, t], axis=0), dtype=F32) for i, t in zip(ident, (cos, sa, sb)))


def kernel(x_prompt, x_sample, cache_k, cache_v, c, c_ctx, norm_w, w_ada, b_ada, w_in, q_norm_w,
           k_norm_w, sgu_norm_w, w_sgu, b_sgu, w_pool, pool_scale, w_br_a, w_br_b, w_br_c, w_merge,
           b_merge, w_out, final_norm_w):
    nb_p, seq_p, d = x_prompt.shape
    nb_s, seq_s, _ = x_sample.shape
    n_ctx, n_lat = nb_p * seq_p, nb_s * seq_s
    assert d == D_MODEL and nb_s + 1 <= MOD_ROWS

    cv = jnp.concatenate([c_ctx[None, :], c, jnp.zeros((MOD_ROWS - 1 - nb_s, d), F32)], axis=0)
    b_ada3 = b_ada.reshape(DEPTH, 1, GATE_COLS)
    mod3 = [_ada_call(cv, w_ada, b_ada3).reshape(MOD_ROWS, 1, GATE_COLS)]
    st_in = _Stream(n_ctx, n_lat, seq_s, IN_TOKEN_TILE)
    st_out = _Stream(n_ctx, n_lat, seq_s, OUT_TOKEN_TILE)
    rope_tabs = _rope_tables(IN_TOKEN_TILE, seq_s)
    cache = (cache_k, cache_v)

    x = (x_prompt.reshape(n_ctx, d), x_sample.reshape(n_lat, d))
    h = _normmod_call(*x, mod3[0], norm_w[0].reshape(1, d), seq=seq_s)
    states = []
    w_mg, w_i, wa, wb, wc, wo = w_merge, w_in, w_br_a, w_br_b, w_br_c, w_out
    for l in range(DEPTH):
        last = l == DEPTH - 1
        bm = b_merge[l].reshape(1, GATE_COLS)
        qnw = q_norm_w[l].reshape(1, HEAD_DIM)
        knw = k_norm_w[l].reshape(1, HEAD_DIM)
        mix_w = (sgu_norm_w[l].reshape(1, B_WIDTH), w_sgu[l].astype(BF16),
                 b_sgu[l].reshape(B_GROUPS, CHUNK, 1), w_pool[l].astype(BF16),
                 pool_scale[l].reshape(1, C_WIDTH))
        nw_next = (final_norm_w if last else norm_w[l + 1]).reshape(1, d)

        gates = _gates_call(h, w_mg, bm, layer=l)
        proj, ks, vs = _proj_call(h, w_i, qnw, knw, rope_tabs, st=st_in, layer=l)
        states.append((ks, vs))
        seq_c = _seq_call(proj, None, mix_w, None, row0=0, n_batch=nb_p, seq=seq_p, layer=l)
        seq_l = _seq_call(proj, cache, mix_w, None if last else (cv, w_ada, b_ada3),
                          row0=n_ctx, n_batch=nb_s, seq=seq_s, layer=l)
        if not last:
            *seq_l, mod_next = seq_l
            mod3.append(mod_next.reshape(MOD_ROWS, 1, GATE_COLS))
        attn, bout, cout = zip(seq_c, seq_l)
        res = _outproj_call(attn, bout, cout, gates, x, mod3[l],
                            wa, wb, wc, wo, nw_next, None if last else mod3[l + 1], st=st_out, layer=l)
        if last:
            y_ctx, y_lat = res
        else:
            x, h = res

    state_k, state_v = (
        jnp.stack([s[i][:n_ctx].reshape(nb_p, seq_p, A_KV_HEADS, HEAD_DIM) for s in states], axis=1)
        for i in range(2))
    return (y_ctx.reshape(nb_p, seq_p, d), y_lat.reshape(nb_s, seq_s, d), state_k, state_v)
```
